```python
import math
import jax, jax.numpy as jnp
from jax import lax
import numpy as np

D_MODEL = 1024
BATCH = 8
SEQ = 4096
DEPTH = 2

HEAD_DIM = 64
HEADS_A = 8
VDIM_A = 2 * HEAD_DIM
WIDTH_A = HEADS_A * VDIM_A
DILATED_GROUPS = ((128, 1), (512, 4), (2048, 16))
N_GROUPS_B = 3
HEADS_PER_GROUP_B = 4
HEADS_B = N_GROUPS_B * HEADS_PER_GROUP_B
WIDTH_B = HEADS_PER_GROUP_B * HEAD_DIM
D_FF = 4 * D_MODEL
ROPE_THETA = 500000.0
ROT_FRACTION = 4
Q_BLOCK = 128
EPS = 1e-6
NEG_INF = -1e30
Q_A_COLS = HEADS_A * 2 * HEAD_DIM
K_A_COLS = HEADS_A * 2 * HEAD_DIM
V_A_COLS = HEADS_A * VDIM_A
QKV_B_COLS = HEADS_B * HEAD_DIM
GATE_COLS = 2 * D_MODEL
IN_COLS = Q_A_COLS + K_A_COLS + V_A_COLS + 3 * QKV_B_COLS + GATE_COLS
IN_SPLITS = (
    Q_A_COLS,
    Q_A_COLS + K_A_COLS,
    Q_A_COLS + K_A_COLS + V_A_COLS,
    Q_A_COLS + K_A_COLS + V_A_COLS + QKV_B_COLS,
    Q_A_COLS + K_A_COLS + V_A_COLS + 2 * QKV_B_COLS,
    Q_A_COLS + K_A_COLS + V_A_COLS + 3 * QKV_B_COLS,
)

kernel_name = "hybrid_diffattn_dilated_encoder"


def rms_norm(x, g):
    xf = x.astype(jnp.float32)
    y = xf * lax.rsqrt(jnp.mean(xf * xf, axis=-1, keepdims=True) + EPS)
    return (y * g.astype(jnp.float32)).astype(x.dtype)


def rope_tables(positions):
    rot = HEAD_DIM // ROT_FRACTION
    inv_freq = ROPE_THETA ** (-jnp.arange(0, rot, 2, dtype=jnp.float32) / rot)
    ang = positions.astype(jnp.float32)[..., None] * inv_freq
    return jnp.cos(ang), jnp.sin(ang)


def apply_rope(x, cos, sin):
    half = cos.shape[-1]
    bshape = cos.shape[:2] + (1,) * (x.ndim - 3) + (half,)
    cs = cos.reshape(bshape).astype(x.dtype)
    sn = sin.reshape(bshape).astype(x.dtype)
    x1, x2, rest = x[..., :half], x[..., half:2 * half], x[..., 2 * half:]
    return jnp.concatenate([x1 * cs - x2 * sn, x2 * cs + x1 * sn, rest], axis=-1)


def diff_attention(q, k, v, lam):
    b, s, h, _, dh = q.shape
    nq = s // Q_BLOCK
    scale = dh ** -0.5
    qb = q.reshape(b, nq, Q_BLOCK, h, 2, dh).transpose(1, 0, 3, 4, 2, 5)
    kt = k.transpose(0, 2, 3, 1, 4)
    vt = v.transpose(0, 2, 1, 3)

    def block(qblk):
        sc = jnp.einsum('bhcqd,bhckd->bhcqk', qblk, kt).astype(jnp.float32) * scale
        p = jax.nn.softmax(sc, axis=-1)
        w = p[:, :, 0] - lam * p[:, :, 1]
        return jnp.einsum('bhqk,bhkd->bhqd', w.astype(vt.dtype), vt)

    o = lax.map(block, qb)
    return o.transpose(1, 0, 3, 2, 4).reshape(b, s, h, 2 * dh)


def dilated_window_attention(q, k, v, dilation, side):
    b, s, h, dh = q.shape
    L = s // dilation
    nb = -(-L // side)
    Lp = nb * side

    def split(t):
        return t.reshape(b, L, dilation, h, dh).transpose(0, 2, 3, 1, 4)

    qs = jnp.pad(split(q), ((0, 0),) * 3 + ((0, Lp - L), (0, 0)))
    pad_kv = ((0, 0),) * 3 + ((side, Lp - L + side), (0, 0))
    kp = jnp.pad(split(k), pad_kv)
    vp = jnp.pad(split(v), pad_kv)

    def band(t):
        return jnp.concatenate(
            [t[..., j * side:j * side + Lp, :].reshape(b, dilation, h, nb, side, dh) for j in range(3)],
            axis=-2)

    kb, vb = band(kp), band(vp)
    qb = qs.reshape(b, dilation, h, nb, side, dh)
    sc = jnp.einsum('bghnqd,bghnkd->bghnqk', qb, kb).astype(jnp.float32) * (dh ** -0.5)
    blk = jnp.arange(nb)[:, None, None]
    qpos = blk * side + jnp.arange(side)[None, :, None]
    kpos = blk * side - side + jnp.arange(3 * side)[None, None, :]
    valid = (jnp.abs(qpos - kpos) <= side) & (kpos >= 0) & (kpos < L)
    sc = jnp.where(valid, sc, NEG_INF)
    lse = jax.nn.logsumexp(sc, axis=-1)
    p = jnp.exp(sc - lse[..., None])
    o = jnp.einsum('bghnqk,bghnkd->bghnqd', p.astype(v.dtype), vb)
    o = o.reshape(b, dilation, h, Lp, dh)[..., :L, :].transpose(0, 3, 1, 2, 4).reshape(b, s, h, dh)
    lse = lse.reshape(b, dilation, h, Lp)[..., :L].transpose(0, 3, 1, 2).reshape(b, s, h)
    return o, lse


def setup_inputs(seed: int = 0) -> dict:
    key = jax.random.key(seed)
    ks = jax.random.split(key, 20)
    D = D_MODEL
    nrm = lambda k, shape, s: jax.random.normal(k, shape, jnp.float32) * s
    x = nrm(ks[0], (BATCH, SEQ, D), 1.0)
    c = nrm(ks[1], (BATCH, D), 1.0)
    offsets = jax.random.randint(ks[2], (BATCH, 1), 0, 1024, dtype=jnp.int32)
    positions = (jnp.arange(SEQ, dtype=jnp.int32)[None, :] + offsets).astype(jnp.int32)
    return {
        "x": x,
        "c": c,
        "positions": positions,
        "ada_w": nrm(ks[3], (DEPTH, D, 6 * D), D ** -0.5),
        "ada_b": nrm(ks[4], (DEPTH, 6 * D), 0.02),
        "norm_mix_g": 1.0 + nrm(ks[5], (DEPTH, D), 0.02),
        "norm_mlp_g": 1.0 + nrm(ks[6], (DEPTH, D), 0.02),
        "w_in": nrm(ks[7], (DEPTH, D, IN_COLS), D ** -0.5),
        "qk_gain_a": 1.0 + nrm(ks[8], (DEPTH, 2, HEAD_DIM), 0.02),
        "lambda_a": nrm(ks[9], (DEPTH, 4, HEAD_DIM), 0.1),
        "subln_g_a": 1.0 + nrm(ks[10], (DEPTH, VDIM_A), 0.02),
        "qk_gain_b": 1.0 + nrm(ks[11], (DEPTH, 2, HEAD_DIM), 0.02),
        "w_branch_a": nrm(ks[12], (DEPTH, WIDTH_A, D), WIDTH_A ** -0.5),
        "w_branch_b": nrm(ks[13], (DEPTH, WIDTH_B, D), WIDTH_B ** -0.5),
        "gate_bias": nrm(ks[14], (DEPTH, GATE_COLS), 0.02),
        "w_out": nrm(ks[15], (DEPTH, D, D), D ** -0.5),
        "w_mlp_up": nrm(ks[16], (DEPTH, D, D_FF), D ** -0.5),
        "w_mlp_down": nrm(ks[17], (DEPTH, D_FF, D), D_FF ** -0.5),
    }


def reference(x, c, positions, ada_w, ada_b, norm_mix_g, norm_mlp_g, w_in, qk_gain_a,
              lambda_a, subln_g_a, qk_gain_b, w_branch_a, w_branch_b, gate_bias, w_out,
              w_mlp_up, w_mlp_down):
    b, s, _ = x.shape
    cos, sin = rope_tables(positions)
    cond = jax.nn.silu(c)
    for l in range(DEPTH):
        mod = (cond @ ada_w[l] + ada_b[l])[:, None, :]
        shift_m, scale_m, gate_m, shift_f, scale_f, gate_f = jnp.split(mod, 6, axis=-1)

        h = rms_norm(x, norm_mix_g[l]) * (1.0 + scale_m) + shift_m
        proj = h @ w_in[l]
        qa, ka, va, qb, kb, vb, gates = jnp.split(proj, IN_SPLITS, axis=-1)

        qa = apply_rope(rms_norm(qa.reshape(b, s, HEADS_A, 2, HEAD_DIM), qk_gain_a[l, 0]), cos, sin)
        ka = apply_rope(rms_norm(ka.reshape(b, s, HEADS_A, 2, HEAD_DIM), qk_gain_a[l, 1]), cos, sin)
        va = va.reshape(b, s, HEADS_A, VDIM_A)
        lam_init = 0.8 - 0.6 * math.exp(-0.3 * l)
        lv = lambda_a[l].astype(jnp.float32)
        lam = jnp.exp(jnp.sum(lv[0] * lv[1])) - jnp.exp(jnp.sum(lv[2] * lv[3])) + lam_init
        oa = diff_attention(qa, ka, va, lam)
        oa = (rms_norm(oa, subln_g_a[l]) * (1.0 - lam_init)).reshape(b, s, WIDTH_A)

        qb = apply_rope(rms_norm(qb.reshape(b, s, N_GROUPS_B, HEADS_PER_GROUP_B, HEAD_DIM), qk_gain_b[l, 0]), cos, sin)
        kb = apply_rope(rms_norm(kb.reshape(b, s, N_GROUPS_B, HEADS_PER_GROUP_B, HEAD_DIM), qk_gain_b[l, 1]), cos, sin)
        vb = vb.reshape(b, s, N_GROUPS_B, HEADS_PER_GROUP_B, HEAD_DIM)
        outs, lses = [], []
        for g, (window, dilation) in enumerate(DILATED_GROUPS):
            o_g, lse_g = dilated_window_attention(qb[:, :, g], kb[:, :, g], vb[:, :, g],
                                                  dilation, window // (2 * dilation))
            outs.append(o_g)
            lses.append(lse_g)
        wgt = jax.nn.softmax(jnp.stack(lses, axis=0), axis=0)
        ob = jnp.sum(wgt[..., None].astype(x.dtype) * jnp.stack(outs, axis=0), axis=0)
        ob = ob.reshape(b, s, WIDTH_B)

        g_a, g_b = jnp.split(jax.nn.sigmoid(gates + gate_bias[l]), 2, axis=-1)
        y = (g_a * (oa @ w_branch_a[l]) + g_b * (ob @ w_branch_b[l])) @ w_out[l]
        x = x + gate_m * y

        h = rms_norm(x, norm_mlp_g[l]) * (1.0 + scale_f) + shift_f
        x = x + gate_f * (jnp.square(jax.nn.relu(h @ w_mlp_up[l])) @ w_mlp_down[l])
    return x
```

```python
import math
from functools import partial

import jax
import jax.numpy as jnp
from jax import lax
from jax.experimental import pallas as pl
from jax.experimental.pallas import tpu as pltpu

D_MODEL = 1024
HEAD_DIM = 64
HEADS_A = 8
VDIM_A = 2 * HEAD_DIM
WIDTH_A = HEADS_A * VDIM_A
DILATED_GROUPS = ((128, 1), (512, 4), (2048, 16))
N_GROUPS_B = 3
HEADS_PER_GROUP_B = 4
WIDTH_B = HEADS_PER_GROUP_B * HEAD_DIM
QKV_B_COLS = N_GROUPS_B * WIDTH_B
D_FF = 4 * D_MODEL
ROPE_THETA = 500000.0
ROT_HALF = HEAD_DIM // 4 // 2
EPS = 1e-6
NEG_INF = -1e30
GATE_COLS = 2 * D_MODEL
IN_COLS = 3 * WIDTH_A + 3 * QKV_B_COLS + GATE_COLS
SM_SCALE = HEAD_DIM ** -0.5

LANES = 128
VMEM_BYTES = 64 * 1024 * 1024
MIB = 1024 * 1024

BF16 = jnp.bfloat16
F32 = jnp.float32
NT_DIMS = (((1,), (1,)), ((), ()))


def _params(semantics, vmem_mib):
    return pltpu.CompilerParams(dimension_semantics=semantics,
                                vmem_limit_bytes=min(vmem_mib * MIB, VMEM_BYTES - 4 * MIB))


def _resident(shape, index_map):
    return pl.BlockSpec(shape, index_map, pipeline_mode=pl.Buffered(1))


def _rope_kernel(pos_ref, cos_ref, sin_ref):
    pos = pos_ref[...].astype(F32)
    for f in range(ROT_HALF):
        ang = pos * (ROPE_THETA ** (-(2.0 * f) / (2 * ROT_HALF)))
        cos_ref[f] = jnp.cos(ang)
        sin_ref[f] = jnp.sin(ang)


def _rope_tables(positions):
    n = positions.size
    pos2 = positions.reshape(n // LANES, LANES)
    cos, sin = pl.pallas_call(
        _rope_kernel,
        out_shape=[jax.ShapeDtypeStruct((ROT_HALF, n // LANES, LANES), F32)] * 2,
    )(pos2)
    cos = cos.reshape(ROT_HALF, n).T
    sin = sin.reshape(ROT_HALF, n).T
    one = jnp.ones((n, HEAD_DIM - 2 * ROT_HALF), F32)
    zero = jnp.zeros((n, HEAD_DIM - 2 * ROT_HALF), F32)
    z8 = jnp.zeros((n, ROT_HALF), F32)
    c_tab = jnp.concatenate([cos, cos, one], axis=1)
    s1_tab = jnp.concatenate([z8, sin, zero], axis=1)
    s2_tab = jnp.concatenate([sin, z8, zero], axis=1)
    rep = LANES // HEAD_DIM
    return tuple(jnp.tile(t, (1, rep)) for t in (c_tab, s1_tab, s2_tab))


def _ada_kernel(c_ref, w_ref, b_ref, o_ref):
    c = c_ref[...]
    cond = (c * jax.nn.sigmoid(c)).astype(BF16)
    o_ref[0] = jnp.dot(cond, w_ref[0].astype(BF16), preferred_element_type=F32) + b_ref[0]


def _ada_mod(c, ada_w, ada_b):
    depth, d, n = ada_w.shape
    b = c.shape[0]
    tn = 768
    return pl.pallas_call(
        _ada_kernel,
        grid=(depth, n // tn),
        in_specs=[pl.BlockSpec((b, d), lambda l, j: (0, 0)),
                  pl.BlockSpec((1, d, tn), lambda l, j: (l, 0, j)),
                  pl.BlockSpec((1, 1, tn), lambda l, j: (l, 0, j))],
        out_specs=pl.BlockSpec((1, b, tn), lambda l, j: (l, 0, j)),
        out_shape=jax.ShapeDtypeStruct((depth, b, n), F32),
        compiler_params=_params(("parallel", "parallel"), 32),
    )(c, ada_w, ada_b.reshape(depth, 1, n))


def _head_norm_rope(t, gain, c_tab, s1_tab, s2_tab):
    lane = lax.broadcasted_iota(jnp.int32, t.shape, 1)
    lo = lane < HEAD_DIM
    sq = t * t
    s_lo = jnp.sum(jnp.where(lo, sq, 0.0), axis=-1, keepdims=True)
    s_hi = jnp.sum(jnp.where(lo, 0.0, sq), axis=-1, keepdims=True)
    ms = jnp.where(lo, s_lo, s_hi) * (1.0 / HEAD_DIM)
    y = t * lax.rsqrt(ms + EPS) * gain
    return (y * c_tab + pltpu.roll(y, ROT_HALF, 1) * s1_tab
            - pltpu.roll(y, LANES - ROT_HALF, 1) * s2_tab)


def _inproj_kernel(x_ref, mod_ref, g_ref, w_ref, c_ref, s1_ref, s2_ref, gain_a_ref, gain_b_ref,
                   gbias_ref, qa_ref, ka_ref, va_ref, qb_ref, kb_ref, vb_ref, gates_ref, h_scr):
    x = x_ref[...]
    ms = jnp.mean(x * x, axis=-1, keepdims=True)
    shift = mod_ref[0, :, 0:D_MODEL]
    scale = mod_ref[0, :, D_MODEL:2 * D_MODEL]
    h = x * lax.rsqrt(ms + EPS) * g_ref[...] * (1.0 + scale) + shift
    h_scr[...] = h.astype(BF16)

    c_tab, s1_tab, s2_tab = c_ref[...], s1_ref[...], s2_ref[...]
    cw = 256

    def proj(col):
        return jnp.dot(h_scr[...], w_ref[:, col:col + cw], preferred_element_type=F32)

    def normed(out_ref, col0, width, gain):
        for c in range(0, width, cw):
            acc = proj(col0 + c)
            for j in range(0, cw, LANES):
                out_ref[:, c + j:c + j + LANES] = _head_norm_rope(
                    acc[:, j:j + LANES], gain, c_tab, s1_tab, s2_tab).astype(BF16)

    def plain(out_ref, col0, width):
        for c in range(0, width, cw):
            out_ref[:, c:c + cw] = proj(col0 + c).astype(BF16)

    col = 0
    normed(qa_ref, col, WIDTH_A, gain_a_ref[0:1, :] * SM_SCALE); col += WIDTH_A
    normed(ka_ref, col, WIDTH_A, gain_a_ref[1:2, :]); col += WIDTH_A
    plain(va_ref, col, WIDTH_A); col += WIDTH_A
    normed(qb_ref, col, QKV_B_COLS, gain_b_ref[0:1, :] * SM_SCALE); col += QKV_B_COLS
    normed(kb_ref, col, QKV_B_COLS, gain_b_ref[1:2, :]); col += QKV_B_COLS
    plain(vb_ref, col, QKV_B_COLS); col += QKV_B_COLS
    for c in range(0, GATE_COLS, cw):
        acc = proj(col + c) + gbias_ref[:, c:c + cw]
        gates_ref[:, c:c + cw] = jax.nn.sigmoid(acc).astype(BF16)


def _in_projection(x2, mod3, g, w_bf, tabs, gain_a, gain_b, gate_bias, seq):
    n, d = x2.shape
    tm = 512
    row = lambda i: (i, 0)
    fixed = lambda i: (0, 0)
    widths = (WIDTH_A, WIDTH_A, WIDTH_A, QKV_B_COLS, QKV_B_COLS, QKV_B_COLS, GATE_COLS)
    return pl.pallas_call(
        _inproj_kernel,
        grid=(n // tm,),
        in_specs=[pl.BlockSpec((tm, d), row),
                  pl.BlockSpec((1, 1, mod3.shape[2]), lambda i: ((i * tm) // seq, 0, 0)),
                  _resident((1, d), fixed),
                  _resident((d, IN_COLS), fixed),
                  pl.BlockSpec((tm, LANES), row),
                  pl.BlockSpec((tm, LANES), row),
                  pl.BlockSpec((tm, LANES), row),
                  _resident((2, LANES), fixed),
                  _resident((2, LANES), fixed),
                  _resident((1, GATE_COLS), fixed)],
        out_specs=[pl.BlockSpec((tm, w), row) for w in widths],
        out_shape=[jax.ShapeDtypeStruct((n, w), BF16) for w in widths],
        scratch_shapes=[pltpu.VMEM((tm, d), BF16)],
        compiler_params=_params(("parallel",), 56),
    )(x2, mod3, g, w_bf, *tabs, gain_a, gain_b, gate_bias)


def _diff_attn_kernel(lam_ref, g_ref, q_ref, k_ref, v_ref, o_ref, *, tq, tk, lam_init):
    seq = k_ref.shape[1]
    q = q_ref[0]
    lane = lax.broadcasted_iota(jnp.int32, q.shape, 1)
    zero = jnp.zeros_like(q)
    qs = jnp.concatenate([jnp.where(lane < HEAD_DIM, q, zero),
                          jnp.where(lane < HEAD_DIM, zero, q)], axis=0)

    def body(j, carry):
        m, l, acc = carry
        start = pl.multiple_of(j * tk, tk)
        kc = k_ref[0, pl.ds(start, tk), :]
        vc = v_ref[0, pl.ds(start, tk), :]
        s = lax.dot_general(qs, kc, NT_DIMS, preferred_element_type=F32)
        m_new = jnp.maximum(m, jnp.max(s, axis=-1, keepdims=True))
        p = jnp.exp(s - m_new)
        alpha = jnp.exp(m - m_new)
        l = alpha * l + jnp.sum(p, axis=-1, keepdims=True)
        acc = alpha * acc + jnp.dot(p.astype(BF16), vc, preferred_element_type=F32)
        return m_new, l, acc

    m0 = jnp.full((2 * tq, 1), NEG_INF, F32)
    l0 = jnp.zeros((2 * tq, 1), F32)
    a0 = jnp.zeros((2 * tq, VDIM_A), F32)
    _, l, acc = lax.fori_loop(0, seq // tk, body, (m0, l0, a0))
    o = acc / l

    lv = lam_ref[...]
    lam = (jnp.exp(jnp.sum(lv[0:1] * lv[1:2], axis=-1, keepdims=True))
           - jnp.exp(jnp.sum(lv[2:3] * lv[3:4], axis=-1, keepdims=True)) + lam_init)
    out = o[:tq] - lam * o[tq:]
    ms = jnp.mean(out * out, axis=-1, keepdims=True)
    y = out * lax.rsqrt(ms + EPS) * g_ref[...] * (1.0 - lam_init)
    o_ref[0] = y.astype(BF16)


def _diff_attention(qa, ka, va, lam_p, subln_g, lam_init):
    b, s, _ = qa.shape
    tq, tk = 256, 512
    kern = partial(_diff_attn_kernel, tq=tq, tk=tk, lam_init=lam_init)
    return pl.pallas_call(
        kern,
        grid=(b, HEADS_A, s // tq),
        in_specs=[pl.BlockSpec(lam_p.shape, lambda bi, h, i: (0, 0)),
                  pl.BlockSpec((1, VDIM_A), lambda bi, h, i: (0, 0)),
                  pl.BlockSpec((1, tq, VDIM_A), lambda bi, h, i: (bi, i, h)),
                  pl.BlockSpec((1, s, VDIM_A), lambda bi, h, i: (bi, 0, h)),
                  pl.BlockSpec((1, s, VDIM_A), lambda bi, h, i: (bi, 0, h))],
        out_specs=pl.BlockSpec((1, tq, VDIM_A), lambda bi, h, i: (bi, i, h)),
        out_shape=jax.ShapeDtypeStruct((b, s, WIDTH_A), BF16),
        compiler_params=_params(("parallel", "parallel", "arbitrary"), 32),
    )(lam_p, subln_g, qa, ka, va)


DIL_TQ = 128
DIL_WIN = 256


def _dil_attn_kernel(q_ref, k_ref, v_ref, o_ref, lse_ref, *, side):
    length = k_ref.shape[1]
    t0 = pl.program_id(1) * DIL_TQ
    ks = pl.multiple_of(jnp.clip(t0 - side, 0, length - DIL_WIN), side)
    q = q_ref[0]
    kw = k_ref[0, pl.ds(ks, DIL_WIN), :]
    vw = v_ref[0, pl.ds(ks, DIL_WIN), :]
    head = lax.broadcasted_iota(jnp.int32, q.shape, 1) // HEAD_DIM
    zero = jnp.zeros_like(q)
    qs = jnp.concatenate([jnp.where(head == h, q, zero) for h in range(HEADS_PER_GROUP_B)], axis=0)
    s = lax.dot_general(qs, kw, NT_DIMS, preferred_element_type=F32)
    rows = lax.broadcasted_iota(jnp.int32, s.shape, 0)
    qpos = t0 + rows % DIL_TQ
    kpos = ks + lax.broadcasted_iota(jnp.int32, s.shape, 1)
    s = jnp.where(jnp.abs(qpos - kpos) <= side, s, NEG_INF)
    m = jnp.max(s, axis=-1, keepdims=True)
    p = jnp.exp(s - m)
    l = jnp.sum(p, axis=-1, keepdims=True)
    o = jnp.dot(p.astype(BF16), vw, preferred_element_type=F32) / l
    lse = jnp.broadcast_to(m + jnp.log(l), o.shape)
    o_sel = jnp.zeros(q.shape, F32)
    lse_sel = jnp.zeros(q.shape, F32)
    for h in range(HEADS_PER_GROUP_B):
        rs = slice(h * DIL_TQ, (h + 1) * DIL_TQ)
        o_sel = jnp.where(head == h, o[rs], o_sel)
        lse_sel = jnp.where(head == h, lse[rs], lse_sel)
    o_ref[0] = o_sel
    lse_ref[0] = lse_sel


def _dilated_attention(q, k, v, side):
    n, length, w = q.shape
    seq_block = lambda i, j: (i, 0, 0)
    q_block = lambda i, j: (i, j, 0)
    return pl.pallas_call(
        partial(_dil_attn_kernel, side=side),
        grid=(n, length // DIL_TQ),
        in_specs=[pl.BlockSpec((1, DIL_TQ, w), q_block),
                  pl.BlockSpec((1, length, w), seq_block),
                  pl.BlockSpec((1, length, w), seq_block)],
        out_specs=[pl.BlockSpec((1, DIL_TQ, w), q_block)] * 2,
        out_shape=[jax.ShapeDtypeStruct((n, length, w), F32)] * 2,
        compiler_params=_params(("parallel", "arbitrary"), 32),
    )(q, k, v)


def _merge_kernel(x_ref, mod_ref, oa_ref, o0_ref, o1_ref, o2_ref, l0_ref, l1_ref, l2_ref,
                  gates_ref, wa_ref, wb_ref, wo_ref, out_ref):
    l0, l1, l2 = l0_ref[...], l1_ref[...], l2_ref[...]
    m = jnp.maximum(jnp.maximum(l0, l1), l2)
    e0, e1, e2 = jnp.exp(l0 - m), jnp.exp(l1 - m), jnp.exp(l2 - m)
    ob = (e0 * o0_ref[...] + e1 * o1_ref[...] + e2 * o2_ref[...]) / (e0 + e1 + e2)
    br_a = jnp.dot(oa_ref[...], wa_ref[...], preferred_element_type=F32)
    br_b = jnp.dot(ob.astype(BF16), wb_ref[...], preferred_element_type=F32)
    g_a = gates_ref[:, 0:D_MODEL].astype(F32)
    g_b = gates_ref[:, D_MODEL:GATE_COLS].astype(F32)
    y = jnp.dot((g_a * br_a + g_b * br_b).astype(BF16), wo_ref[...], preferred_element_type=F32)
    gate_m = mod_ref[0, :, 2 * D_MODEL:3 * D_MODEL]
    out_ref[...] = x_ref[...] + gate_m * y


def _merge_project(x2, mod3, oa, outs, lses, gates, wa, wb, wo, seq):
    n, d = x2.shape
    tm = 512
    row = lambda i: (i, 0)
    fixed = lambda i: (0, 0)
    return pl.pallas_call(
        _merge_kernel,
        grid=(n // tm,),
        in_specs=[pl.BlockSpec((tm, d), row),
                  pl.BlockSpec((1, 1, mod3.shape[2]), lambda i: ((i * tm) // seq, 0, 0)),
                  pl.BlockSpec((tm, WIDTH_A), row)]
                 + [pl.BlockSpec((tm, WIDTH_B), row)] * 6
                 + [pl.BlockSpec((tm, GATE_COLS), row),
                    _resident(wa.shape, fixed), _resident(wb.shape, fixed), _resident(wo.shape, fixed)],
        out_specs=pl.BlockSpec((tm, d), row),
        out_shape=jax.ShapeDtypeStruct((n, d), F32),
        compiler_params=_params(("parallel",), 48),
    )(x2, mod3, oa, *outs, *lses, gates, wa, wb, wo)


def _mlp_kernel(x_ref, mod_ref, g_ref, wu_ref, wd_ref, out_ref):
    x = x_ref[...]
    ms = jnp.mean(x * x, axis=-1, keepdims=True)
    shift = mod_ref[0, :, 3 * D_MODEL:4 * D_MODEL]
    scale = mod_ref[0, :, 4 * D_MODEL:5 * D_MODEL]
    gate = mod_ref[0, :, 5 * D_MODEL:6 * D_MODEL]
    h = (x * lax.rsqrt(ms + EPS) * g_ref[...] * (1.0 + scale) + shift).astype(BF16)
    u = jnp.maximum(jnp.dot(h, wu_ref[...], preferred_element_type=F32), 0.0)
    dn = jnp.dot((u * u).astype(BF16), wd_ref[...], preferred_element_type=F32)
    out_ref[...] = x + gate * dn


def _mlp(x2, mod3, g, wu, wd, seq):
    n, d = x2.shape
    tm = 512
    row = lambda i: (i, 0)
    fixed = lambda i: (0, 0)
    return pl.pallas_call(
        _mlp_kernel,
        grid=(n // tm,),
        in_specs=[pl.BlockSpec((tm, d), row),
                  pl.BlockSpec((1, 1, mod3.shape[2]), lambda i: ((i * tm) // seq, 0, 0)),
                  _resident((1, d), fixed),
                  _resident(wu.shape, fixed),
                  _resident(wd.shape, fixed)],
        out_specs=pl.BlockSpec((tm, d), row),
        out_shape=jax.ShapeDtypeStruct((n, d), F32),
        compiler_params=_params(("parallel",), 56),
    )(x2, mod3, g, wu, wd)


def _to_residues(t, dilation):
    b, s, w = t.shape
    return t.reshape(b, s // dilation, dilation, w).transpose(0, 2, 1, 3).reshape(b * dilation, s // dilation, w)


def _from_residues(t, b, dilation):
    n, length, w = t.shape
    return t.reshape(b, dilation, length, w).transpose(0, 2, 1, 3).reshape(b * length * dilation, w)


def kernel(x, c, positions, ada_w, ada_b, norm_mix_g, norm_mlp_g, w_in, qk_gain_a, lambda_a, subln_g_a,
           qk_gain_b, w_branch_a, w_branch_b, gate_bias, w_out, w_mlp_up, w_mlp_down):
    b, s, d = x.shape
    depth = ada_w.shape[0]
    tabs = _rope_tables(positions)
    mod = _ada_mod(c, ada_w, ada_b)
    rep = LANES // HEAD_DIM
    x2 = x.reshape(b * s, d)
    for l in range(depth):
        mod3 = mod[l].reshape(b, 1, 6 * d)
        qa, ka, va, qb, kb, vb, gates = _in_projection(
            x2, mod3, norm_mix_g[l].reshape(1, d), w_in[l].astype(BF16), tabs,
            jnp.tile(qk_gain_a[l], (1, rep)), jnp.tile(qk_gain_b[l], (1, rep)),
            gate_bias[l].reshape(1, GATE_COLS), s)

        lam_init = 0.8 - 0.6 * math.exp(-0.3 * l)
        oa = _diff_attention(qa.reshape(b, s, WIDTH_A), ka.reshape(b, s, WIDTH_A),
                             va.reshape(b, s, WIDTH_A), lambda_a[l], subln_g_a[l].reshape(1, VDIM_A),
                             lam_init)

        outs, lses = [], []
        for g, (window, dilation) in enumerate(DILATED_GROUPS):
            cols = slice(g * WIDTH_B, (g + 1) * WIDTH_B)
            split = lambda t: _to_residues(t.reshape(b, s, QKV_B_COLS)[:, :, cols], dilation)
            o_g, lse_g = _dilated_attention(split(qb), split(kb), split(vb), window // (2 * dilation))
            outs.append(_from_residues(o_g, b, dilation))
            lses.append(_from_residues(lse_g, b, dilation))

        x2 = _merge_project(x2, mod3, oa.reshape(b * s, WIDTH_A), outs, lses, gates,
                            w_branch_a[l].astype(BF16), w_branch_b[l].astype(BF16),
                            w_out[l].astype(BF16), s)
        x2 = _mlp(x2, mod3, norm_mlp_g[l].reshape(1, d), w_mlp_up[l].astype(BF16),
                  w_mlp_down[l].astype(BF16), s)
    return x2.reshape(b, s, d)
```

```python
import math
from functools import partial

import jax
import jax.numpy as jnp
from jax import lax
from jax.experimental import pallas as pl
from jax.experimental.pallas import tpu as pltpu

D_MODEL = 1024
HEAD_DIM = 64
HEADS_A = 8
VDIM_A = 2 * HEAD_DIM
WIDTH_A = HEADS_A * VDIM_A
DILATED_GROUPS = ((128, 1), (512, 4), (2048, 16))
N_GROUPS_B = 3
HEADS_PER_GROUP_B = 4
WIDTH_B = HEADS_PER_GROUP_B * HEAD_DIM
QKV_B_COLS = N_GROUPS_B * WIDTH_B
D_FF = 4 * D_MODEL
ROPE_THETA = 500000.0
ROT_HALF = HEAD_DIM // 4 // 2
EPS = 1e-6
NEG_INF = -1e30
GATE_COLS = 2 * D_MODEL
IN_COLS = 3 * WIDTH_A + 3 * QKV_B_COLS + GATE_COLS
SM_SCALE = HEAD_DIM ** -0.5
LOG2E = math.log2(math.e)

LANES = 128
VMEM_BYTES = 64 * 1024 * 1024
MIB = 1024 * 1024

BF16 = jnp.bfloat16
F32 = jnp.float32
NT_DIMS = (((1,), (1,)), ((), ()))


def _params(semantics, vmem_mib):
    return pltpu.CompilerParams(dimension_semantics=semantics,
                                vmem_limit_bytes=min(vmem_mib * MIB, VMEM_BYTES - 4 * MIB))


def _resident(shape, index_map):
    return pl.BlockSpec(shape, index_map, pipeline_mode=pl.Buffered(1))


def _rope_kernel(pos_ref, cos_ref, sin_ref):
    pos = pos_ref[...].astype(F32)
    for f in range(ROT_HALF):
        ang = pos * (ROPE_THETA ** (-(2.0 * f) / (2 * ROT_HALF)))
        cos_ref[f] = jnp.cos(ang)
        sin_ref[f] = jnp.sin(ang)


def _rope_tables(positions):
    n = positions.size
    pos2 = positions.reshape(n // LANES, LANES)
    cos, sin = pl.pallas_call(
        _rope_kernel,
        out_shape=[jax.ShapeDtypeStruct((ROT_HALF, n // LANES, LANES), F32)] * 2,
    )(pos2)
    cos = cos.reshape(ROT_HALF, n).T
    sin = sin.reshape(ROT_HALF, n).T
    one = jnp.ones((n, HEAD_DIM - 2 * ROT_HALF), F32)
    zero = jnp.zeros((n, HEAD_DIM - 2 * ROT_HALF), F32)
    z8 = jnp.zeros((n, ROT_HALF), F32)
    c_tab = jnp.concatenate([cos, cos, one], axis=1)
    s1_tab = jnp.concatenate([z8, sin, zero], axis=1)
    s2_tab = jnp.concatenate([sin, z8, zero], axis=1)
    rep = LANES // HEAD_DIM
    return tuple(jnp.tile(t, (1, rep)) for t in (c_tab, s1_tab, s2_tab))


def _ada_kernel(c_ref, w_ref, b_ref, o_ref):
    c = c_ref[...]
    cond = (c * jax.nn.sigmoid(c)).astype(BF16)
    o_ref[0] = jnp.dot(cond, w_ref[0].astype(BF16), preferred_element_type=F32) + b_ref[0]


def _ada_mod(c, ada_w, ada_b):
    depth, d, n = ada_w.shape
    b = c.shape[0]
    tn = 768
    return pl.pallas_call(
        _ada_kernel,
        grid=(depth, n // tn),
        in_specs=[pl.BlockSpec((b, d), lambda l, j: (0, 0)),
                  pl.BlockSpec((1, d, tn), lambda l, j: (l, 0, j)),
                  pl.BlockSpec((1, 1, tn), lambda l, j: (l, 0, j))],
        out_specs=pl.BlockSpec((1, b, tn), lambda l, j: (l, 0, j)),
        out_shape=jax.ShapeDtypeStruct((depth, b, n), F32),
        compiler_params=_params(("parallel", "parallel"), 32),
    )(c, ada_w, ada_b.reshape(depth, 1, n))


def _head_norm_rope(t, gain, c_tab, s1_tab, s2_tab):
    lane = lax.broadcasted_iota(jnp.int32, t.shape, 1)
    lo = lane < HEAD_DIM
    sq = t * t
    s_lo = jnp.sum(jnp.where(lo, sq, 0.0), axis=-1, keepdims=True)
    s_hi = jnp.sum(jnp.where(lo, 0.0, sq), axis=-1, keepdims=True)
    ms = jnp.where(lo, s_lo, s_hi) * (1.0 / HEAD_DIM)
    y = t * lax.rsqrt(ms + EPS) * gain
    return (y * c_tab + pltpu.roll(y, ROT_HALF, 1) * s1_tab
            - pltpu.roll(y, LANES - ROT_HALF, 1) * s2_tab)


def _inproj_kernel(x_ref, mod_ref, g_ref, w_ref, c_ref, s1_ref, s2_ref, gain_a_ref, gain_b_ref,
                   gbias_ref, qa_ref, ka_ref, va_ref, qb_ref, kb_ref, vb_ref, gates_ref, h_scr):
    x = x_ref[...]
    ms = jnp.mean(x * x, axis=-1, keepdims=True)
    shift = mod_ref[0, :, 0:D_MODEL]
    scale = mod_ref[0, :, D_MODEL:2 * D_MODEL]
    h = x * lax.rsqrt(ms + EPS) * g_ref[...] * (1.0 + scale) + shift
    h_scr[...] = h.astype(BF16)

    c_tab, s1_tab, s2_tab = c_ref[...], s1_ref[...], s2_ref[...]
    cw = 256

    def proj(col):
        return jnp.dot(h_scr[...], w_ref[:, col:col + cw], preferred_element_type=F32)

    def normed(out_ref, col0, width, gain):
        for c in range(0, width, cw):
            acc = proj(col0 + c)
            for j in range(0, cw, LANES):
                out_ref[:, c + j:c + j + LANES] = _head_norm_rope(
                    acc[:, j:j + LANES], gain, c_tab, s1_tab, s2_tab).astype(BF16)

    def plain(out_ref, col0, width):
        for c in range(0, width, cw):
            out_ref[:, c:c + cw] = proj(col0 + c).astype(BF16)

    col = 0
    normed(qa_ref, col, WIDTH_A, gain_a_ref[0:1, :] * (SM_SCALE * LOG2E)); col += WIDTH_A
    normed(ka_ref, col, WIDTH_A, gain_a_ref[1:2, :]); col += WIDTH_A
    plain(va_ref, col, WIDTH_A); col += WIDTH_A
    normed(qb_ref, col, QKV_B_COLS, gain_b_ref[0:1, :] * SM_SCALE); col += QKV_B_COLS
    normed(kb_ref, col, QKV_B_COLS, gain_b_ref[1:2, :]); col += QKV_B_COLS
    plain(vb_ref, col, QKV_B_COLS); col += QKV_B_COLS
    for c in range(0, GATE_COLS, cw):
        acc = proj(col + c) + gbias_ref[:, c:c + cw]
        gates_ref[:, c:c + cw] = jax.nn.sigmoid(acc).astype(BF16)


def _in_projection(x2, mod3, g, w_bf, tabs, gain_a, gain_b, gate_bias, seq):
    n, d = x2.shape
    tm = 512
    row = lambda i: (i, 0)
    fixed = lambda i: (0, 0)
    widths = (WIDTH_A, WIDTH_A, WIDTH_A, QKV_B_COLS, QKV_B_COLS, QKV_B_COLS, GATE_COLS)
    return pl.pallas_call(
        _inproj_kernel,
        grid=(n // tm,),
        in_specs=[pl.BlockSpec((tm, d), row),
                  pl.BlockSpec((1, 1, mod3.shape[2]), lambda i: ((i * tm) // seq, 0, 0)),
                  _resident((1, d), fixed),
                  _resident((d, IN_COLS), fixed),
                  pl.BlockSpec((tm, LANES), row),
                  pl.BlockSpec((tm, LANES), row),
                  pl.BlockSpec((tm, LANES), row),
                  _resident((2, LANES), fixed),
                  _resident((2, LANES), fixed),
                  _resident((1, GATE_COLS), fixed)],
        out_specs=[pl.BlockSpec((tm, w), row) for w in widths],
        out_shape=[jax.ShapeDtypeStruct((n, w), BF16) for w in widths],
        scratch_shapes=[pltpu.VMEM((tm, d), BF16)],
        compiler_params=_params(("parallel",), 56),
    )(x2, mod3, g, w_bf, *tabs, gain_a, gain_b, gate_bias)


DA_TQ = 256
DA_TK = 512
DA_ONES_ROWS = 16


def _diff_attn_kernel(lam_ref, g_ref, q_ref, k_ref, v_ref, o_ref, sa_scr, sb_scr, ma_scr, mb_scr, vt_scr,
                      *, nq, lam_init):
    seq = k_ref.shape[1]
    tq = q_ref.shape[1]
    t = pl.program_id(0)

    @pl.when(t == 0)
    def _():
        sb_scr[...] = jnp.zeros(sb_scr.shape, F32)
        mb_scr[...] = jnp.zeros(mb_scr.shape, F32)

    @pl.when((t == 0) | ((t - 1) % nq == 0))
    def _():
        vt_scr[0:VDIM_A, :] = v_ref[0].T
        vt_scr[VDIM_A:VDIM_A + DA_ONES_ROWS, :] = jnp.ones((DA_ONES_ROWS, seq), BF16)

    def step(s_w, m_w, s_r, m_r):
        q = q_ref[0]
        lane = lax.broadcasted_iota(jnp.int32, q.shape, 1)
        zero = jnp.zeros_like(q)
        qs = jnp.concatenate([jnp.where(lane < HEAD_DIM, q, zero),
                              jnp.where(lane < HEAD_DIM, zero, q)], axis=0)
        m_prev = m_r[0:1]
        m_new = None
        acc = None
        for c in range(0, seq, DA_TK):
            st = lax.dot_general(k_ref[0, c:c + DA_TK, :], qs, NT_DIMS, preferred_element_type=F32)
            s_w[c:c + DA_TK, :] = st
            mc = jnp.max(st, axis=0, keepdims=True)
            m_new = mc if m_new is None else jnp.maximum(m_new, mc)
            pt = jnp.exp2(s_r[c:c + DA_TK, :] - m_prev).astype(BF16)
            part = jnp.dot(vt_scr[:, c:c + DA_TK], pt, preferred_element_type=F32)
            acc = part if acc is None else acc + part
        m_w[...] = jnp.broadcast_to(m_new, m_w.shape)

        o = acc[0:VDIM_A] / acc[VDIM_A:VDIM_A + 1]
        lv = lam_ref[...]
        lam = (jnp.exp(jnp.sum(lv[0:1] * lv[1:2], axis=-1, keepdims=True))
               - jnp.exp(jnp.sum(lv[2:3] * lv[3:4], axis=-1, keepdims=True)) + lam_init)
        out = o[:, :tq] - lam * o[:, tq:]
        ms = jnp.mean(out * out, axis=0, keepdims=True)
        y = out * lax.rsqrt(ms + EPS) * g_ref[...] * (1.0 - lam_init)
        o_ref[0] = y.T.astype(BF16)

    @pl.when(t % 2 == 0)
    def _():
        step(sa_scr, ma_scr, sb_scr, mb_scr)

    @pl.when(t % 2 == 1)
    def _():
        step(sb_scr, mb_scr, sa_scr, ma_scr)


def _diff_attention(qa, ka, va, lam_p, subln_g, lam_init):
    b, s, _ = qa.shape
    nq = s // DA_TQ
    n = b * HEADS_A * nq

    def blk(t):
        t = jnp.clip(t, 0, n - 1)
        return t // (HEADS_A * nq), (t // nq) % HEADS_A, t % nq

    def q_map(t):
        bi, h, i = blk(t)
        return bi, i, h

    def k_map(t):
        bi, h, _ = blk(t)
        return bi, 0, h

    def v_map(t):
        bi, h, _ = blk(t - 1)
        return bi, 0, h

    def o_map(t):
        bi, h, i = blk(t - 1)
        return bi, i, h

    score_buf = pltpu.VMEM((s, 2 * DA_TQ), F32)
    max_buf = pltpu.VMEM((8, 2 * DA_TQ), F32)
    return pl.pallas_call(
        partial(_diff_attn_kernel, nq=nq, lam_init=lam_init),
        grid=(n + 1,),
        in_specs=[pl.BlockSpec(lam_p.shape, lambda t: (0, 0)),
                  pl.BlockSpec((VDIM_A, 1), lambda t: (0, 0)),
                  pl.BlockSpec((1, DA_TQ, VDIM_A), q_map),
                  pl.BlockSpec((1, s, VDIM_A), k_map),
                  pl.BlockSpec((1, s, VDIM_A), v_map)],
        out_specs=pl.BlockSpec((1, DA_TQ, VDIM_A), o_map),
        out_shape=jax.ShapeDtypeStruct((b, s, WIDTH_A), BF16),
        scratch_shapes=[score_buf, score_buf, max_buf, max_buf,
                        pltpu.VMEM((VDIM_A + DA_ONES_ROWS, s), BF16)],
        compiler_params=_params(("arbitrary",), 40),
        name="diff_attn",
    )(lam_p, subln_g, qa, ka, va)


DIL_TQ = 128
DIL_WIN = 256


def _dil_attn_kernel(q_ref, k_ref, v_ref, o_ref, lse_ref, *, side):
    length = k_ref.shape[1]
    t0 = pl.program_id(1) * DIL_TQ
    ks = pl.multiple_of(jnp.clip(t0 - side, 0, length - DIL_WIN), side)
    q = q_ref[0]
    kw = k_ref[0, pl.ds(ks, DIL_WIN), :]
    vw = v_ref[0, pl.ds(ks, DIL_WIN), :]
    head = lax.broadcasted_iota(jnp.int32, q.shape, 1) // HEAD_DIM
    zero = jnp.zeros_like(q)
    qs = jnp.concatenate([jnp.where(head == h, q, zero) for h in range(HEADS_PER_GROUP_B)], axis=0)
    s = lax.dot_general(qs, kw, NT_DIMS, preferred_element_type=F32)
    rows = lax.broadcasted_iota(jnp.int32, s.shape, 0)
    qpos = t0 + rows % DIL_TQ
    kpos = ks + lax.broadcasted_iota(jnp.int32, s.shape, 1)
    s = jnp.where(jnp.abs(qpos - kpos) <= side, s, NEG_INF)
    m = jnp.max(s, axis=-1, keepdims=True)
    p = jnp.exp(s - m)
    l = jnp.sum(p, axis=-1, keepdims=True)
    o = jnp.dot(p.astype(BF16), vw, preferred_element_type=F32) / l
    lse = jnp.broadcast_to(m + jnp.log(l), o.shape)
    o_sel = jnp.zeros(q.shape, F32)
    lse_sel = jnp.zeros(q.shape, F32)
    for h in range(HEADS_PER_GROUP_B):
        rs = slice(h * DIL_TQ, (h + 1) * DIL_TQ)
        o_sel = jnp.where(head == h, o[rs], o_sel)
        lse_sel = jnp.where(head == h, lse[rs], lse_sel)
    o_ref[0] = o_sel
    lse_ref[0] = lse_sel


def _dilated_attention(q, k, v, side):
    n, length, w = q.shape
    seq_block = lambda i, j: (i, 0, 0)
    q_block = lambda i, j: (i, j, 0)
    return pl.pallas_call(
        partial(_dil_attn_kernel, side=side),
        grid=(n, length // DIL_TQ),
        in_specs=[pl.BlockSpec((1, DIL_TQ, w), q_block),
                  pl.BlockSpec((1, length, w), seq_block),
                  pl.BlockSpec((1, length, w), seq_block)],
        out_specs=[pl.BlockSpec((1, DIL_TQ, w), q_block)] * 2,
        out_shape=[jax.ShapeDtypeStruct((n, length, w), F32)] * 2,
        compiler_params=_params(("parallel", "arbitrary"), 32),
    )(q, k, v)


def _merge_kernel(x_ref, mod_ref, oa_ref, o0_ref, o1_ref, o2_ref, l0_ref, l1_ref, l2_ref,
                  gates_ref, wa_ref, wb_ref, wo_ref, out_ref):
    l0, l1, l2 = l0_ref[...], l1_ref[...], l2_ref[...]
    m = jnp.maximum(jnp.maximum(l0, l1), l2)
    e0, e1, e2 = jnp.exp(l0 - m), jnp.exp(l1 - m), jnp.exp(l2 - m)
    ob = (e0 * o0_ref[...] + e1 * o1_ref[...] + e2 * o2_ref[...]) / (e0 + e1 + e2)
    br_a = jnp.dot(oa_ref[...], wa_ref[...], preferred_element_type=F32)
    br_b = jnp.dot(ob.astype(BF16), wb_ref[...], preferred_element_type=F32)
    g_a = gates_ref[:, 0:D_MODEL].astype(F32)
    g_b = gates_ref[:, D_MODEL:GATE_COLS].astype(F32)
    y = jnp.dot((g_a * br_a + g_b * br_b).astype(BF16), wo_ref[...], preferred_element_type=F32)
    gate_m = mod_ref[0, :, 2 * D_MODEL:3 * D_MODEL]
    out_ref[...] = x_ref[...] + gate_m * y


def _merge_project(x2, mod3, oa, outs, lses, gates, wa, wb, wo, seq):
    n, d = x2.shape
    tm = 512
    row = lambda i: (i, 0)
    fixed = lambda i: (0, 0)
    return pl.pallas_call(
        _merge_kernel,
        grid=(n // tm,),
        in_specs=[pl.BlockSpec((tm, d), row),
                  pl.BlockSpec((1, 1, mod3.shape[2]), lambda i: ((i * tm) // seq, 0, 0)),
                  pl.BlockSpec((tm, WIDTH_A), row)]
                 + [pl.BlockSpec((tm, WIDTH_B), row)] * 6
                 + [pl.BlockSpec((tm, GATE_COLS), row),
                    _resident(wa.shape, fixed), _resident(wb.shape, fixed), _resident(wo.shape, fixed)],
        out_specs=pl.BlockSpec((tm, d), row),
        out_shape=jax.ShapeDtypeStruct((n, d), F32),
        compiler_params=_params(("parallel",), 48),
    )(x2, mod3, oa, *outs, *lses, gates, wa, wb, wo)


def _mlp_kernel(x_ref, mod_ref, g_ref, wu_ref, wd_ref, out_ref):
    x = x_ref[...]
    ms = jnp.mean(x * x, axis=-1, keepdims=True)
    shift = mod_ref[0, :, 3 * D_MODEL:4 * D_MODEL]
    scale = mod_ref[0, :, 4 * D_MODEL:5 * D_MODEL]
    gate = mod_ref[0, :, 5 * D_MODEL:6 * D_MODEL]
    h = (x * lax.rsqrt(ms + EPS) * g_ref[...] * (1.0 + scale) + shift).astype(BF16)
    u = jnp.maximum(jnp.dot(h, wu_ref[...], preferred_element_type=F32), 0.0)
    dn = jnp.dot((u * u).astype(BF16), wd_ref[...], preferred_element_type=F32)
    out_ref[...] = x + gate * dn


def _mlp(x2, mod3, g, wu, wd, seq):
    n, d = x2.shape
    tm = 512
    row = lambda i: (i, 0)
    fixed = lambda i: (0, 0)
    return pl.pallas_call(
        _mlp_kernel,
        grid=(n // tm,),
        in_specs=[pl.BlockSpec((tm, d), row),
                  pl.BlockSpec((1, 1, mod3.shape[2]), lambda i: ((i * tm) // seq, 0, 0)),
                  _resident((1, d), fixed),
                  _resident(wu.shape, fixed),
                  _resident(wd.shape, fixed)],
        out_specs=pl.BlockSpec((tm, d), row),
        out_shape=jax.ShapeDtypeStruct((n, d), F32),
        compiler_params=_params(("parallel",), 56),
    )(x2, mod3, g, wu, wd)


def _to_residues(t, dilation):
    b, s, w = t.shape
    return t.reshape(b, s // dilation, dilation, w).transpose(0, 2, 1, 3).reshape(b * dilation, s // dilation, w)


def _from_residues(t, b, dilation):
    n, length, w = t.shape
    return t.reshape(b, dilation, length, w).transpose(0, 2, 1, 3).reshape(b * length * dilation, w)


def kernel(x, c, positions, ada_w, ada_b, norm_mix_g, norm_mlp_g, w_in, qk_gain_a, lambda_a, subln_g_a,
           qk_gain_b, w_branch_a, w_branch_b, gate_bias, w_out, w_mlp_up, w_mlp_down):
    b, s, d = x.shape
    depth = ada_w.shape[0]
    tabs = _rope_tables(positions)
    mod = _ada_mod(c, ada_w, ada_b)
    rep = LANES // HEAD_DIM
    x2 = x.reshape(b * s, d)
    for l in range(depth):
        mod3 = mod[l].reshape(b, 1, 6 * d)
        qa, ka, va, qb, kb, vb, gates = _in_projection(
            x2, mod3, norm_mix_g[l].reshape(1, d), w_in[l].astype(BF16), tabs,
            jnp.tile(qk_gain_a[l], (1, rep)), jnp.tile(qk_gain_b[l], (1, rep)),
            gate_bias[l].reshape(1, GATE_COLS), s)

        lam_init = 0.8 - 0.6 * math.exp(-0.3 * l)
        oa = _diff_attention(qa.reshape(b, s, WIDTH_A), ka.reshape(b, s, WIDTH_A),
                             va.reshape(b, s, WIDTH_A), lambda_a[l], subln_g_a[l].reshape(VDIM_A, 1),
                             lam_init)

        outs, lses = [], []
        for g, (window, dilation) in enumerate(DILATED_GROUPS):
            cols = slice(g * WIDTH_B, (g + 1) * WIDTH_B)
            split = lambda t: _to_residues(t.reshape(b, s, QKV_B_COLS)[:, :, cols], dilation)
            o_g, lse_g = _dilated_attention(split(qb), split(kb), split(vb), window // (2 * dilation))
            outs.append(_from_residues(o_g, b, dilation))
            lses.append(_from_residues(lse_g, b, dilation))

        x2 = _merge_project(x2, mod3, oa.reshape(b * s, WIDTH_A), outs, lses, gates,
                            w_branch_a[l].astype(BF16), w_branch_b[l].astype(BF16),
                            w_out[l].astype(BF16), s)
        x2 = _mlp(x2, mod3, norm_mlp_g[l].reshape(1, d), w_mlp_up[l].astype(BF16),
                  w_mlp_down[l].astype(BF16), s)
    return x2.reshape(b, s, d)
```

```python
import math
from functools import partial

import jax
import jax.numpy as jnp
from jax import lax
from jax.experimental import pallas as pl
from jax.experimental.pallas import tpu as pltpu

D_MODEL = 1024
HEAD_DIM = 64
HEADS_A = 8
VDIM_A = 2 * HEAD_DIM
WIDTH_A = HEADS_A * VDIM_A
DILATED_GROUPS = ((128, 1), (512, 4), (2048, 16))
N_GROUPS_B = 3
HEADS_PER_GROUP_B = 4
WIDTH_B = HEADS_PER_GROUP_B * HEAD_DIM
QKV_B_COLS = N_GROUPS_B * WIDTH_B
D_FF = 4 * D_MODEL
ROPE_THETA = 500000.0
ROT_HALF = HEAD_DIM // 4 // 2
EPS = 1e-6
NEG_INF = -1e30
GATE_COLS = 2 * D_MODEL
IN_COLS = 3 * WIDTH_A + 3 * QKV_B_COLS + GATE_COLS
SM_SCALE = HEAD_DIM ** -0.5
LOG2E = math.log2(math.e)

LANES = 128
PROJ_CHUNK = 256
TOKEN_TILE = 512
VMEM_BYTES = 64 * 1024 * 1024
MIB = 1024 * 1024

BF16 = jnp.bfloat16
F32 = jnp.float32
NT_DIMS = (((1,), (1,)), ((), ()))


def _params(semantics, vmem_mib):
    return pltpu.CompilerParams(dimension_semantics=semantics,
                                vmem_limit_bytes=min(vmem_mib * MIB, VMEM_BYTES - 4 * MIB))


def _resident(shape, index_map):
    return pl.BlockSpec(shape, index_map, pipeline_mode=pl.Buffered(1))


def _rope_kernel(pos_ref, cos_ref, sin_ref):
    pos = pos_ref[...].astype(F32)
    for f in range(ROT_HALF):
        ang = pos * (ROPE_THETA ** (-(2.0 * f) / (2 * ROT_HALF)))
        cos_ref[f] = jnp.cos(ang)
        sin_ref[f] = jnp.sin(ang)


def _rope_tables(positions):
    n = positions.size
    pos2 = positions.reshape(n // LANES, LANES)
    cos, sin = pl.pallas_call(
        _rope_kernel,
        out_shape=[jax.ShapeDtypeStruct((ROT_HALF, n // LANES, LANES), F32)] * 2,
    )(pos2)
    cos = cos.reshape(ROT_HALF, n).T
    sin = sin.reshape(ROT_HALF, n).T
    one = jnp.ones((n, HEAD_DIM - 2 * ROT_HALF), F32)
    zero = jnp.zeros((n, HEAD_DIM - 2 * ROT_HALF), F32)
    z8 = jnp.zeros((n, ROT_HALF), F32)
    c_tab = jnp.concatenate([cos, cos, one], axis=1)
    s1_tab = jnp.concatenate([z8, sin, zero], axis=1)
    s2_tab = jnp.concatenate([sin, z8, zero], axis=1)
    rep = LANES // HEAD_DIM
    return tuple(jnp.tile(t, (1, rep)) for t in (c_tab, s1_tab, s2_tab))


def _ada_kernel(c_ref, w_ref, b_ref, o_ref):
    c = c_ref[...]
    cond = (c * jax.nn.sigmoid(c)).astype(BF16)
    o_ref[0] = jnp.dot(cond, w_ref[0].astype(BF16), preferred_element_type=F32) + b_ref[0]


def _ada_mod(c, ada_w, ada_b):
    depth, d, n = ada_w.shape
    b = c.shape[0]
    tn = 768
    return pl.pallas_call(
        _ada_kernel,
        grid=(depth, n // tn),
        in_specs=[pl.BlockSpec((b, d), lambda l, j: (0, 0)),
                  pl.BlockSpec((1, d, tn), lambda l, j: (l, 0, j)),
                  pl.BlockSpec((1, 1, tn), lambda l, j: (l, 0, j))],
        out_specs=pl.BlockSpec((1, b, tn), lambda l, j: (l, 0, j)),
        out_shape=jax.ShapeDtypeStruct((depth, b, n), F32),
        compiler_params=_params(("parallel", "parallel"), 32),
    )(c, ada_w, ada_b.reshape(depth, 1, n))


def _head_norm_rope(t, gain, c_tab, s1_tab, s2_tab):
    lane = lax.broadcasted_iota(jnp.int32, t.shape, 1)
    lo = lane < HEAD_DIM
    sq = t * t
    s_lo = jnp.sum(jnp.where(lo, sq, 0.0), axis=-1, keepdims=True)
    s_hi = jnp.sum(jnp.where(lo, 0.0, sq), axis=-1, keepdims=True)
    ms = jnp.where(lo, s_lo, s_hi) * (1.0 / HEAD_DIM)
    y = t * lax.rsqrt(ms + EPS) * gain
    return (y * c_tab + pltpu.roll(y, ROT_HALF, 1) * s1_tab
            - pltpu.roll(y, LANES - ROT_HALF, 1) * s2_tab)


def _inproj_kernel(x_ref, mod_ref, g_ref, w_ref, c_ref, s1_ref, s2_ref, gain_a_ref, gain_b_ref,
                   gbias_ref, qa_ref, ka_ref, va_ref, *rest):
    qb_refs, kb_refs, vb_refs = rest[0:3], rest[3:6], rest[6:9]
    gates_ref, h_scr, y_scr = rest[9:12]
    tm = x_ref.shape[0]
    x = x_ref[...]
    ms = jnp.mean(x * x, axis=-1, keepdims=True)
    shift = mod_ref[0, :, 0:D_MODEL]
    scale = mod_ref[0, :, D_MODEL:2 * D_MODEL]
    h = x * lax.rsqrt(ms + EPS) * g_ref[...] * (1.0 + scale) + shift
    h_scr[...] = h.astype(BF16)

    c_tab, s1_tab, s2_tab = c_ref[...], s1_ref[...], s2_ref[...]
    cw = PROJ_CHUNK

    def proj(col):
        return jnp.dot(h_scr[...], w_ref[:, col:col + cw], preferred_element_type=F32)

    def normed(out_ref, col0, width, gain):
        for c in range(0, width, cw):
            acc = proj(col0 + c)
            for j in range(0, cw, LANES):
                out_ref[:, c + j:c + j + LANES] = _head_norm_rope(
                    acc[:, j:j + LANES], gain, c_tab, s1_tab, s2_tab).astype(BF16)

    def plain(out_ref, col0, width):
        for c in range(0, width, cw):
            out_ref[:, c:c + cw] = proj(col0 + c).astype(BF16)

    def split_residues(out_ref, dilation):
        for r in range(dilation):
            for j in range(WIDTH_B // LANES):
                out_ref[0, r, :, j * LANES:(j + 1) * LANES] = (
                    y_scr[j, pl.ds(r, tm // dilation, stride=dilation), :].astype(BF16))

    def normed_groups(out_refs, col0, gain):
        for g, (_, dilation) in enumerate(DILATED_GROUPS):
            acc = proj(col0 + g * WIDTH_B)
            for j in range(0, WIDTH_B, LANES):
                y_scr[j // LANES] = _head_norm_rope(acc[:, j:j + LANES], gain, c_tab, s1_tab, s2_tab)
            split_residues(out_refs[g], dilation)

    def plain_groups(out_refs, col0):
        for g, (_, dilation) in enumerate(DILATED_GROUPS):
            acc = proj(col0 + g * WIDTH_B)
            for j in range(0, WIDTH_B, LANES):
                y_scr[j // LANES] = acc[:, j:j + LANES]
            split_residues(out_refs[g], dilation)

    col = 0
    normed(qa_ref, col, WIDTH_A, gain_a_ref[0:1, :] * (SM_SCALE * LOG2E)); col += WIDTH_A
    normed(ka_ref, col, WIDTH_A, gain_a_ref[1:2, :]); col += WIDTH_A
    plain(va_ref, col, WIDTH_A); col += WIDTH_A
    normed_groups(qb_refs, col, gain_b_ref[0:1, :] * SM_SCALE); col += QKV_B_COLS
    normed_groups(kb_refs, col, gain_b_ref[1:2, :]); col += QKV_B_COLS
    plain_groups(vb_refs, col); col += QKV_B_COLS
    for c in range(0, GATE_COLS, cw):
        acc = proj(col + c) + gbias_ref[:, c:c + cw]
        gates_ref[:, c:c + cw] = jax.nn.sigmoid(acc).astype(BF16)


def _in_projection(x2, mod3, g, w_bf, tabs, gain_a, gain_b, gate_bias, seq):
    n, d = x2.shape
    tm = TOKEN_TILE
    tiles = seq // tm
    row = lambda i: (i, 0)
    fixed = lambda i: (0, 0)
    res_block = lambda i: (i // tiles, 0, i % tiles, 0)
    wide = [pl.BlockSpec((tm, WIDTH_A), row)] * 3
    wide_shape = [jax.ShapeDtypeStruct((n, WIDTH_A), BF16)] * 3
    grp = [pl.BlockSpec((1, dil, tm // dil, WIDTH_B), res_block) for _, dil in DILATED_GROUPS] * 3
    grp_shape = [jax.ShapeDtypeStruct((n // seq, dil, seq // dil, WIDTH_B), BF16) for _, dil in DILATED_GROUPS] * 3
    outs = pl.pallas_call(
        _inproj_kernel,
        grid=(n // tm,),
        in_specs=[pl.BlockSpec((tm, d), row),
                  pl.BlockSpec((1, 1, mod3.shape[2]), lambda i: (i // tiles, 0, 0)),
                  _resident((1, d), fixed),
                  _resident((d, IN_COLS), fixed),
                  pl.BlockSpec((tm, LANES), row),
                  pl.BlockSpec((tm, LANES), row),
                  pl.BlockSpec((tm, LANES), row),
                  _resident((2, LANES), fixed),
                  _resident((2, LANES), fixed),
                  _resident((1, GATE_COLS), fixed)],
        out_specs=wide + grp + [pl.BlockSpec((tm, GATE_COLS), row)],
        out_shape=wide_shape + grp_shape + [jax.ShapeDtypeStruct((n, GATE_COLS), BF16)],
        scratch_shapes=[pltpu.VMEM((tm, d), BF16), pltpu.VMEM((WIDTH_B // LANES, tm, LANES), F32)],
        compiler_params=_params(("parallel",), 56),
        name="in_proj",
    )(x2, mod3, g, w_bf, *tabs, gain_a, gain_b, gate_bias)
    return outs[0], outs[1], outs[2], outs[3:6], outs[6:9], outs[9:12], outs[12]


DA_TQ = 512
DA_TK = 512
DA_ONES_ROWS = 16


def _diff_attn_kernel(lam_ref, g_ref, q_ref, k_ref, v_ref, o_ref, sa_scr, sb_scr, ma_scr, mb_scr, vt_scr,
                      *, nq, lam_init):
    seq = k_ref.shape[1]
    tq = q_ref.shape[1]
    t = pl.program_id(0)

    @pl.when(t == 0)
    def _():
        sb_scr[...] = jnp.zeros(sb_scr.shape, F32)
        mb_scr[...] = jnp.zeros(mb_scr.shape, F32)

    @pl.when((t == 0) | ((t - 1) % nq == 0))
    def _():
        vt_scr[0:VDIM_A, :] = v_ref[0].T
        vt_scr[VDIM_A:VDIM_A + DA_ONES_ROWS, :] = jnp.ones((DA_ONES_ROWS, seq), BF16)

    def step(s_w, m_w, s_r, m_r):
        q = q_ref[0]
        lane = lax.broadcasted_iota(jnp.int32, q.shape, 1)
        zero = jnp.zeros_like(q)
        qs = jnp.concatenate([jnp.where(lane < HEAD_DIM, q, zero),
                              jnp.where(lane < HEAD_DIM, zero, q)], axis=0)
        m_prev = m_r[0:1]
        m_new = None
        acc = None
        for c in range(0, seq, DA_TK):
            st = lax.dot_general(k_ref[0, c:c + DA_TK, :], qs, NT_DIMS, preferred_element_type=F32)
            s_w[c:c + DA_TK, :] = st
            mc = jnp.max(st, axis=0, keepdims=True)
            m_new = mc if m_new is None else jnp.maximum(m_new, mc)
            pt = jnp.exp2(s_r[c:c + DA_TK, :] - m_prev).astype(BF16)
            part = jnp.dot(vt_scr[:, c:c + DA_TK], pt, preferred_element_type=F32)
            acc = part if acc is None else acc + part
        m_w[...] = jnp.broadcast_to(m_new, m_w.shape)

        o = acc[0:VDIM_A] / acc[VDIM_A:VDIM_A + 1]
        lv = lam_ref[...]
        lam = (jnp.exp(jnp.sum(lv[0:1] * lv[1:2], axis=-1, keepdims=True))
               - jnp.exp(jnp.sum(lv[2:3] * lv[3:4], axis=-1, keepdims=True)) + lam_init)
        out = o[:, :tq] - lam * o[:, tq:]
        ms = jnp.mean(out * out, axis=0, keepdims=True)
        y = out * lax.rsqrt(ms + EPS) * g_ref[...] * (1.0 - lam_init)
        o_ref[0] = y.T.astype(BF16)

    @pl.when(t % 2 == 0)
    def _():
        step(sa_scr, ma_scr, sb_scr, mb_scr)

    @pl.when(t % 2 == 1)
    def _():
        step(sb_scr, mb_scr, sa_scr, ma_scr)


def _diff_attention(qa, ka, va, lam_p, subln_g, lam_init):
    b, s, _ = qa.shape
    nq = s // DA_TQ
    n = b * HEADS_A * nq

    def blk(t):
        t = jnp.clip(t, 0, n - 1)
        return t // (HEADS_A * nq), (t // nq) % HEADS_A, t % nq

    def q_map(t):
        bi, h, i = blk(t)
        return bi, i, h

    def k_map(t):
        bi, h, _ = blk(t)
        return bi, 0, h

    def v_map(t):
        bi, h, _ = blk(t - 1)
        return bi, 0, h

    def o_map(t):
        bi, h, i = blk(t - 1)
        return bi, i, h

    score_buf = pltpu.VMEM((s, 2 * DA_TQ), F32)
    max_buf = pltpu.VMEM((8, 2 * DA_TQ), F32)
    return pl.pallas_call(
        partial(_diff_attn_kernel, nq=nq, lam_init=lam_init),
        grid=(n + 1,),
        in_specs=[pl.BlockSpec(lam_p.shape, lambda t: (0, 0)),
                  pl.BlockSpec((VDIM_A, 1), lambda t: (0, 0)),
                  pl.BlockSpec((1, DA_TQ, VDIM_A), q_map),
                  pl.BlockSpec((1, s, VDIM_A), k_map),
                  pl.BlockSpec((1, s, VDIM_A), v_map)],
        out_specs=pl.BlockSpec((1, DA_TQ, VDIM_A), o_map),
        out_shape=jax.ShapeDtypeStruct((b, s, WIDTH_A), BF16),
        scratch_shapes=[score_buf, score_buf, max_buf, max_buf,
                        pltpu.VMEM((VDIM_A + DA_ONES_ROWS, s), BF16)],
        compiler_params=_params(("arbitrary",), 52),
        name="diff_attn",
    )(lam_p, subln_g, qa, ka, va)


DIL_TQ = 128
DIL_WIN = 256


def _band_block(q, kw, vw, qpos0, kpos0, side):
    nq = q.shape[0]
    head = lax.broadcasted_iota(jnp.int32, q.shape, 1) // HEAD_DIM
    zero = jnp.zeros_like(q)
    qs = jnp.concatenate([jnp.where(head == h, q, zero) for h in range(HEADS_PER_GROUP_B)], axis=0)
    s = lax.dot_general(qs, kw, NT_DIMS, preferred_element_type=F32)
    rows = lax.broadcasted_iota(jnp.int32, s.shape, 0)
    cols = lax.broadcasted_iota(jnp.int32, s.shape, 1)
    dist = (rows & (nq - 1)) - cols + (qpos0 - kpos0)
    s = jnp.where(jnp.abs(dist) <= side, s, NEG_INF)
    m = jnp.max(s, axis=-1, keepdims=True)
    p = jnp.exp(s - m)
    l = jnp.sum(p, axis=-1, keepdims=True)
    o = jnp.dot(p.astype(BF16), vw, preferred_element_type=F32) / l
    lse = jnp.broadcast_to(m + jnp.log(l), o.shape)
    o_sel = jnp.zeros(q.shape, F32)
    lse_sel = jnp.zeros(q.shape, F32)
    for h in range(HEADS_PER_GROUP_B):
        rs = slice(h * nq, (h + 1) * nq)
        o_sel = jnp.where(head == h, o[rs], o_sel)
        lse_sel = jnp.where(head == h, lse[rs], lse_sel)
    return o_sel, lse_sel


def _dilated_kernel(*refs):
    q_refs, k_refs, v_refs = refs[0:3], refs[3:6], refs[6:9]
    ob_ref, o_scr, lse_scr = refs[9:12]
    tile = ob_ref.shape[0]
    i = pl.program_id(1)
    for g, (window, dil) in enumerate(DILATED_GROUPS):
        side = window // (2 * dil)
        length = k_refs[g].shape[2]
        per_res = tile // dil
        nq = min(per_res, DIL_TQ)
        for r in range(dil):
            for u in range(per_res // nq):
                t0 = i * per_res + u * nq
                if length == DIL_WIN:
                    ks = 0
                else:
                    ks = pl.multiple_of(jnp.clip(t0 - side, 0, length - DIL_WIN), side)
                q = q_refs[g][0, r, u * nq:(u + 1) * nq, :]
                kw = k_refs[g][0, r, pl.ds(ks, DIL_WIN), :]
                vw = v_refs[g][0, r, pl.ds(ks, DIL_WIN), :]
                o, lse = _band_block(q, kw, vw, t0, ks, side)
                rows = pl.ds(u * nq * dil + r, nq, stride=dil) if dil > 1 else pl.ds(u * nq, nq)
                for j in range(WIDTH_B // LANES):
                    o_scr[g, j, rows, :] = o[:, j * LANES:(j + 1) * LANES]
                    lse_scr[g, j, rows, :] = lse[:, j * LANES:(j + 1) * LANES]
    for j in range(WIDTH_B // LANES):
        l0, l1, l2 = lse_scr[0, j], lse_scr[1, j], lse_scr[2, j]
        m = jnp.maximum(jnp.maximum(l0, l1), l2)
        e0, e1, e2 = jnp.exp(l0 - m), jnp.exp(l1 - m), jnp.exp(l2 - m)
        ob = (e0 * o_scr[0, j] + e1 * o_scr[1, j] + e2 * o_scr[2, j]) / (e0 + e1 + e2)
        ob_ref[:, j * LANES:(j + 1) * LANES] = ob.astype(BF16)


def _dilated_attention(qs, ks, vs):
    b = qs[0].shape[0]
    seq = qs[0].shape[1] * qs[0].shape[2]
    tile = TOKEN_TILE
    tiles = seq // tile
    q_specs = [pl.BlockSpec((1, dil, tile // dil, WIDTH_B), lambda bi, i: (bi, 0, i, 0))
               for _, dil in DILATED_GROUPS]
    kv_specs = [pl.BlockSpec((1, dil, seq // dil, WIDTH_B), lambda bi, i: (bi, 0, 0, 0),
                             pipeline_mode=pl.Buffered(1))
                for _, dil in DILATED_GROUPS]
    return pl.pallas_call(
        _dilated_kernel,
        grid=(b, tiles),
        in_specs=q_specs + kv_specs + kv_specs,
        out_specs=pl.BlockSpec((tile, WIDTH_B), lambda bi, i: (bi * tiles + i, 0)),
        out_shape=jax.ShapeDtypeStruct((b * seq, WIDTH_B), BF16),
        scratch_shapes=[pltpu.VMEM((N_GROUPS_B, WIDTH_B // LANES, tile, LANES), F32)] * 2,
        compiler_params=_params(("parallel", "arbitrary"), 48),
        name="dilated_attn",
    )(*qs, *ks, *vs)


def _post_kernel(x_ref, mod_ref, oa_ref, ob_ref, gates_ref, wa_ref, wb_ref, wo_ref, g_ref, wu_ref, wd_ref,
                 out_ref):
    br_a = jnp.dot(oa_ref[...], wa_ref[...], preferred_element_type=F32)
    br_b = jnp.dot(ob_ref[...], wb_ref[...], preferred_element_type=F32)
    g_a = gates_ref[:, 0:D_MODEL].astype(F32)
    g_b = gates_ref[:, D_MODEL:GATE_COLS].astype(F32)
    y = jnp.dot((g_a * br_a + g_b * br_b).astype(BF16), wo_ref[...], preferred_element_type=F32)
    mod = lambda k: mod_ref[0, :, k * D_MODEL:(k + 1) * D_MODEL]
    x = x_ref[...] + mod(2) * y

    ms = jnp.mean(x * x, axis=-1, keepdims=True)
    h = (x * lax.rsqrt(ms + EPS) * g_ref[...] * (1.0 + mod(4)) + mod(3)).astype(BF16)
    u = jnp.maximum(jnp.dot(h, wu_ref[...], preferred_element_type=F32), 0.0)
    dn = jnp.dot((u * u).astype(BF16), wd_ref[...], preferred_element_type=F32)
    out_ref[...] = x + mod(5) * dn


def _post_attention(x2, mod3, oa, ob, gates, wa, wb, wo, g, wu, wd, seq):
    n, d = x2.shape
    tm = TOKEN_TILE
    row = lambda i: (i, 0)
    fixed = lambda i: (0, 0)
    return pl.pallas_call(
        _post_kernel,
        grid=(n // tm,),
        in_specs=[pl.BlockSpec((tm, d), row),
                  pl.BlockSpec((1, 1, mod3.shape[2]), lambda i: ((i * tm) // seq, 0, 0)),
                  pl.BlockSpec((tm, WIDTH_A), row),
                  pl.BlockSpec((tm, WIDTH_B), row),
                  pl.BlockSpec((tm, GATE_COLS), row),
                  _resident(wa.shape, fixed), _resident(wb.shape, fixed), _resident(wo.shape, fixed),
                  _resident((1, d), fixed), _resident(wu.shape, fixed), _resident(wd.shape, fixed)],
        out_specs=pl.BlockSpec((tm, d), row),
        out_shape=jax.ShapeDtypeStruct((n, d), F32),
        compiler_params=_params(("parallel",), 60),
        name="post_attn",
    )(x2, mod3, oa, ob, gates, wa, wb, wo, g, wu, wd)


def kernel(x, c, positions, ada_w, ada_b, norm_mix_g, norm_mlp_g, w_in, qk_gain_a, lambda_a, subln_g_a,
           qk_gain_b, w_branch_a, w_branch_b, gate_bias, w_out, w_mlp_up, w_mlp_down):
    b, s, d = x.shape
    depth = ada_w.shape[0]
    tabs = _rope_tables(positions)
    mod = _ada_mod(c, ada_w, ada_b)
    rep = LANES // HEAD_DIM
    x2 = x.reshape(b * s, d)
    for l in range(depth):
        mod3 = mod[l].reshape(b, 1, 6 * d)
        qa, ka, va, qbs, kbs, vbs, gates = _in_projection(
            x2, mod3, norm_mix_g[l].reshape(1, d), w_in[l].astype(BF16), tabs,
            jnp.tile(qk_gain_a[l], (1, rep)), jnp.tile(qk_gain_b[l], (1, rep)),
            gate_bias[l].reshape(1, GATE_COLS), s)

        lam_init = 0.8 - 0.6 * math.exp(-0.3 * l)
        oa = _diff_attention(qa.reshape(b, s, WIDTH_A), ka.reshape(b, s, WIDTH_A),
                             va.reshape(b, s, WIDTH_A), lambda_a[l], subln_g_a[l].reshape(VDIM_A, 1),
                             lam_init)
        ob = _dilated_attention(qbs, kbs, vbs)

        x2 = _post_attention(x2, mod3, oa.reshape(b * s, WIDTH_A), ob, gates,
                             w_branch_a[l].astype(BF16), w_branch_b[l].astype(BF16), w_out[l].astype(BF16),
                             norm_mlp_g[l].reshape(1, d), w_mlp_up[l].astype(BF16),
                             w_mlp_down[l].astype(BF16), s)
    return x2.reshape(b, s, d)
```

```python
import math
from functools import partial

import jax
import jax.numpy as jnp
from jax import lax
from jax.experimental import pallas as pl
from jax.experimental.pallas import tpu as pltpu

D_MODEL = 1024
HEAD_DIM = 64
HEADS_A = 8
VDIM_A = 2 * HEAD_DIM
WIDTH_A = HEADS_A * VDIM_A
DILATED_GROUPS = ((128, 1), (512, 4), (2048, 16))
N_GROUPS_B = 3
HEADS_PER_GROUP_B = 4
WIDTH_B = HEADS_PER_GROUP_B * HEAD_DIM
QKV_B_COLS = N_GROUPS_B * WIDTH_B
D_FF = 4 * D_MODEL
ROPE_THETA = 500000.0
ROT_HALF = HEAD_DIM // 4 // 2
EPS = 1e-6
NEG_INF = -1e30
GATE_COLS = 2 * D_MODEL
IN_COLS = 3 * WIDTH_A + 3 * QKV_B_COLS + GATE_COLS
SM_SCALE = HEAD_DIM ** -0.5
LOG2E = math.log2(math.e)

LANES = 128
PROJ_CHUNK = 256
TOKEN_TILE = 512
VMEM_BYTES = 64 * 1024 * 1024
MIB = 1024 * 1024

BF16 = jnp.bfloat16
F32 = jnp.float32
NT_DIMS = (((1,), (1,)), ((), ()))


def _params(semantics, vmem_mib):
    return pltpu.CompilerParams(dimension_semantics=semantics,
                                vmem_limit_bytes=min(vmem_mib * MIB, VMEM_BYTES - 4 * MIB))


def _resident(shape, index_map):
    return pl.BlockSpec(shape, index_map, pipeline_mode=pl.Buffered(1))


def _rope_kernel(pos_ref, cos_ref, sin_ref):
    pos = pos_ref[...].astype(F32)
    for f in range(ROT_HALF):
        ang = pos * (ROPE_THETA ** (-(2.0 * f) / (2 * ROT_HALF)))
        cos_ref[f] = jnp.cos(ang)
        sin_ref[f] = jnp.sin(ang)


def _rope_tables(positions):
    n = positions.size
    pos2 = positions.reshape(n // LANES, LANES)
    cos, sin = pl.pallas_call(
        _rope_kernel,
        out_shape=[jax.ShapeDtypeStruct((ROT_HALF, n // LANES, LANES), F32)] * 2,
    )(pos2)
    cos = cos.reshape(ROT_HALF, n).T
    sin = sin.reshape(ROT_HALF, n).T
    one = jnp.ones((n, HEAD_DIM - 2 * ROT_HALF), F32)
    zero = jnp.zeros((n, HEAD_DIM - 2 * ROT_HALF), F32)
    z8 = jnp.zeros((n, ROT_HALF), F32)
    c_tab = jnp.concatenate([cos, cos, one], axis=1)
    s1_tab = jnp.concatenate([z8, sin, zero], axis=1)
    s2_tab = jnp.concatenate([sin, z8, zero], axis=1)
    rep = LANES // HEAD_DIM
    return tuple(jnp.tile(t, (1, rep)) for t in (c_tab, s1_tab, s2_tab))


def _ada_kernel(c_ref, w_ref, b_ref, o_ref):
    c = c_ref[...]
    cond = (c * jax.nn.sigmoid(c)).astype(BF16)
    o_ref[0] = jnp.dot(cond, w_ref[0].astype(BF16), preferred_element_type=F32) + b_ref[0]


def _ada_mod(c, ada_w, ada_b):
    depth, d, n = ada_w.shape
    b = c.shape[0]
    tn = 768
    return pl.pallas_call(
        _ada_kernel,
        grid=(depth, n // tn),
        in_specs=[pl.BlockSpec((b, d), lambda l, j: (0, 0)),
                  pl.BlockSpec((1, d, tn), lambda l, j: (l, 0, j)),
                  pl.BlockSpec((1, 1, tn), lambda l, j: (l, 0, j))],
        out_specs=pl.BlockSpec((1, b, tn), lambda l, j: (l, 0, j)),
        out_shape=jax.ShapeDtypeStruct((depth, b, n), F32),
        compiler_params=_params(("parallel", "parallel"), 32),
    )(c, ada_w, ada_b.reshape(depth, 1, n))


def _head_norm_rope(t, gain, c_tab, s1_tab, s2_tab):
    lane = lax.broadcasted_iota(jnp.int32, t.shape, 1)
    lo = lane < HEAD_DIM
    sq = t * t
    s_lo = jnp.sum(jnp.where(lo, sq, 0.0), axis=-1, keepdims=True)
    s_hi = jnp.sum(jnp.where(lo, 0.0, sq), axis=-1, keepdims=True)
    ms = jnp.where(lo, s_lo, s_hi) * (1.0 / HEAD_DIM)
    y = t * lax.rsqrt(ms + EPS) * gain
    return (y * c_tab + pltpu.roll(y, ROT_HALF, 1) * s1_tab
            - pltpu.roll(y, LANES - ROT_HALF, 1) * s2_tab)


def _inproj_kernel(x_ref, mod_ref, g_ref, w_ref, c_ref, s1_ref, s2_ref, gain_a_ref, gain_b_ref,
                   gbias_ref, qa_ref, ka_ref, va_ref, *rest):
    qb_refs, kb_refs, vb_refs = rest[0:3], rest[3:6], rest[6:9]
    gates_ref, h_scr, y_scr = rest[9:12]
    tm = x_ref.shape[0]
    x = x_ref[...]
    ms = jnp.mean(x * x, axis=-1, keepdims=True)
    shift = mod_ref[0, :, 0:D_MODEL]
    scale = mod_ref[0, :, D_MODEL:2 * D_MODEL]
    h = x * lax.rsqrt(ms + EPS) * g_ref[...] * (1.0 + scale) + shift
    h_scr[...] = h.astype(BF16)

    c_tab, s1_tab, s2_tab = c_ref[...], s1_ref[...], s2_ref[...]
    cw = PROJ_CHUNK

    def proj(col):
        return jnp.dot(h_scr[...], w_ref[:, col:col + cw], preferred_element_type=F32)

    def normed(out_ref, col0, width, gain):
        for c in range(0, width, cw):
            acc = proj(col0 + c)
            for j in range(0, cw, LANES):
                out_ref[:, c + j:c + j + LANES] = _head_norm_rope(
                    acc[:, j:j + LANES], gain, c_tab, s1_tab, s2_tab).astype(BF16)

    def plain(out_ref, col0, width):
        for c in range(0, width, cw):
            out_ref[:, c:c + cw] = proj(col0 + c).astype(BF16)

    def split_residues(out_ref, dilation):
        for r in range(dilation):
            for j in range(WIDTH_B // LANES):
                out_ref[0, r, :, j * LANES:(j + 1) * LANES] = (
                    y_scr[j, pl.ds(r, tm // dilation, stride=dilation), :].astype(BF16))

    def normed_groups(out_refs, col0, gain):
        for g, (_, dilation) in enumerate(DILATED_GROUPS):
            acc = proj(col0 + g * WIDTH_B)
            for j in range(0, WIDTH_B, LANES):
                y_scr[j // LANES] = _head_norm_rope(acc[:, j:j + LANES], gain, c_tab, s1_tab, s2_tab)
            split_residues(out_refs[g], dilation)

    def plain_groups(out_refs, col0):
        for g, (_, dilation) in enumerate(DILATED_GROUPS):
            acc = proj(col0 + g * WIDTH_B)
            for j in range(0, WIDTH_B, LANES):
                y_scr[j // LANES] = acc[:, j:j + LANES]
            split_residues(out_refs[g], dilation)

    col = 0
    normed(qa_ref, col, WIDTH_A, gain_a_ref[0:1, :] * (SM_SCALE * LOG2E)); col += WIDTH_A
    normed(ka_ref, col, WIDTH_A, gain_a_ref[1:2, :]); col += WIDTH_A
    plain(va_ref, col, WIDTH_A); col += WIDTH_A
    normed_groups(qb_refs, col, gain_b_ref[0:1, :] * SM_SCALE); col += QKV_B_COLS
    normed_groups(kb_refs, col, gain_b_ref[1:2, :]); col += QKV_B_COLS
    plain_groups(vb_refs, col); col += QKV_B_COLS
    for c in range(0, GATE_COLS, cw):
        acc = proj(col + c) + gbias_ref[:, c:c + cw]
        gates_ref[:, c:c + cw] = jax.nn.sigmoid(acc).astype(BF16)


def _in_projection(x2, mod3, g, w_bf, tabs, gain_a, gain_b, gate_bias, seq):
    n, d = x2.shape
    tm = TOKEN_TILE
    tiles = seq // tm
    row = lambda i: (i, 0)
    fixed = lambda i: (0, 0)
    res_block = lambda i: (i // tiles, 0, i % tiles, 0)
    wide = [pl.BlockSpec((tm, WIDTH_A), row)] * 3
    wide_shape = [jax.ShapeDtypeStruct((n, WIDTH_A), BF16)] * 3
    grp = [pl.BlockSpec((1, dil, tm // dil, WIDTH_B), res_block) for _, dil in DILATED_GROUPS] * 3
    grp_shape = [jax.ShapeDtypeStruct((n // seq, dil, seq // dil, WIDTH_B), BF16) for _, dil in DILATED_GROUPS] * 3
    outs = pl.pallas_call(
        _inproj_kernel,
        grid=(n // tm,),
        in_specs=[pl.BlockSpec((tm, d), row),
                  pl.BlockSpec((1, 1, mod3.shape[2]), lambda i: (i // tiles, 0, 0)),
                  _resident((1, d), fixed),
                  _resident((d, IN_COLS), fixed),
                  pl.BlockSpec((tm, LANES), row),
                  pl.BlockSpec((tm, LANES), row),
                  pl.BlockSpec((tm, LANES), row),
                  _resident((2, LANES), fixed),
                  _resident((2, LANES), fixed),
                  _resident((1, GATE_COLS), fixed)],
        out_specs=wide + grp + [pl.BlockSpec((tm, GATE_COLS), row)],
        out_shape=wide_shape + grp_shape + [jax.ShapeDtypeStruct((n, GATE_COLS), BF16)],
        scratch_shapes=[pltpu.VMEM((tm, d), BF16), pltpu.VMEM((WIDTH_B // LANES, tm, LANES), F32)],
        compiler_params=_params(("parallel",), 56),
        name="in_proj",
    )(x2, mod3, g, w_bf, *tabs, gain_a, gain_b, gate_bias)
    return outs[0], outs[1], outs[2], outs[3:6], outs[6:9], outs[9:12], outs[12]


DA_TQ = 256
DA_TK = 512
DA_FINISH_AFTER = 1024
DA_ONES_ROWS = 16


def _diff_attn_kernel(lam_ref, g_ref, q_ref, k_ref, v_ref, o_ref, sa_scr, sb_scr, ma_scr, mb_scr,
                      acca_scr, accb_scr, vt_scr, *, nq, lam_init):
    seq = k_ref.shape[1]
    tq = q_ref.shape[1]
    t = pl.program_id(0)

    @pl.when(t == 0)
    def _():
        sb_scr[...] = jnp.zeros(sb_scr.shape, F32)
        mb_scr[...] = jnp.zeros(mb_scr.shape, F32)
        accb_scr[...] = jnp.ones(accb_scr.shape, F32)

    @pl.when((t == 0) | ((t - 1) % nq == 0))
    def _():
        vt_scr[0:VDIM_A, :] = v_ref[0].T
        vt_scr[VDIM_A:VDIM_A + DA_ONES_ROWS, :] = jnp.ones((DA_ONES_ROWS, seq), BF16)

    def finish(acc_r):
        o = acc_r[0:VDIM_A, :] / acc_r[VDIM_A:VDIM_A + 1, :]
        lv = lam_ref[...]
        lam = (jnp.exp(jnp.sum(lv[0:1] * lv[1:2], axis=-1, keepdims=True))
               - jnp.exp(jnp.sum(lv[2:3] * lv[3:4], axis=-1, keepdims=True)) + lam_init)
        out = o[:, :tq] - lam * o[:, tq:]
        ms = jnp.mean(out * out, axis=0, keepdims=True)
        y = out * lax.rsqrt(ms + EPS) * g_ref[...] * (1.0 - lam_init)
        o_ref[0] = y.T.astype(BF16)

    def step(s_w, m_w, acc_w, s_r, m_r, acc_r):
        q = q_ref[0]
        lane = lax.broadcasted_iota(jnp.int32, q.shape, 1)
        zero = jnp.zeros_like(q)
        qs = jnp.concatenate([jnp.where(lane < HEAD_DIM, q, zero),
                              jnp.where(lane < HEAD_DIM, zero, q)], axis=0)
        m_prev = m_r[0:1]
        m_new = None
        acc = None
        for c in range(0, seq, DA_TK):
            st = lax.dot_general(k_ref[0, c:c + DA_TK, :], qs, NT_DIMS, preferred_element_type=F32)
            s_w[c:c + DA_TK, :] = st
            mc = jnp.max(st, axis=0, keepdims=True)
            m_new = mc if m_new is None else jnp.maximum(m_new, mc)
            pt = jnp.exp2(s_r[c:c + DA_TK, :] - m_prev).astype(BF16)
            part = jnp.dot(vt_scr[:, c:c + DA_TK], pt, preferred_element_type=F32)
            acc = part if acc is None else acc + part
            if c == DA_FINISH_AFTER:
                finish(acc_r)
        m_w[...] = jnp.broadcast_to(m_new, m_w.shape)
        acc_w[...] = acc

    @pl.when(t % 2 == 0)
    def _():
        step(sa_scr, ma_scr, acca_scr, sb_scr, mb_scr, accb_scr)

    @pl.when(t % 2 == 1)
    def _():
        step(sb_scr, mb_scr, accb_scr, sa_scr, ma_scr, acca_scr)


def _diff_attention(qa, ka, va, lam_p, subln_g, lam_init):
    b, s, _ = qa.shape
    nq = s // DA_TQ
    n = b * HEADS_A * nq

    def blk(t):
        t = jnp.clip(t, 0, n - 1)
        return t // (HEADS_A * nq), (t // nq) % HEADS_A, t % nq

    def q_map(t):
        bi, h, i = blk(t)
        return bi, i, h

    def k_map(t):
        bi, h, _ = blk(t)
        return bi, 0, h

    def v_map(t):
        bi, h, _ = blk(t - 1)
        return bi, 0, h

    def o_map(t):
        bi, h, i = blk(t - 2)
        return bi, i, h

    score_buf = pltpu.VMEM((s, 2 * DA_TQ), F32)
    max_buf = pltpu.VMEM((8, 2 * DA_TQ), F32)
    acc_buf = pltpu.VMEM((VDIM_A + DA_ONES_ROWS, 2 * DA_TQ), F32)
    return pl.pallas_call(
        partial(_diff_attn_kernel, nq=nq, lam_init=lam_init),
        grid=(n + 2,),
        in_specs=[pl.BlockSpec(lam_p.shape, lambda t: (0, 0)),
                  pl.BlockSpec((VDIM_A, 1), lambda t: (0, 0)),
                  pl.BlockSpec((1, DA_TQ, VDIM_A), q_map),
                  pl.BlockSpec((1, s, VDIM_A), k_map),
                  pl.BlockSpec((1, s, VDIM_A), v_map)],
        out_specs=pl.BlockSpec((1, DA_TQ, VDIM_A), o_map),
        out_shape=jax.ShapeDtypeStruct((b, s, WIDTH_A), BF16),
        scratch_shapes=[score_buf, score_buf, max_buf, max_buf, acc_buf, acc_buf,
                        pltpu.VMEM((VDIM_A + DA_ONES_ROWS, s), BF16)],
        compiler_params=_params(("arbitrary",), 40),
        name="diff_attn",
    )(lam_p, subln_g, qa, ka, va)


DIL_TQ = 128
DIL_SKEW = 1
DIL_WIN = 256


def _band_scores(q, kw):
    head = lax.broadcasted_iota(jnp.int32, q.shape, 1) // HEAD_DIM
    zero = jnp.zeros_like(q)
    qs = jnp.concatenate([jnp.where(head == h, q, zero) for h in range(HEADS_PER_GROUP_B)], axis=0)
    return lax.dot_general(qs, kw, NT_DIMS, preferred_element_type=F32)


def _band_softmax(s, rel, offset, side):
    s = jnp.where(jnp.abs(rel + offset) <= side, s, NEG_INF)
    m = jnp.max(s, axis=-1, keepdims=True)
    p = jnp.exp(s - m)
    l = jnp.sum(p, axis=-1, keepdims=True)
    return p.astype(BF16), m, l


def _band_output(p, m, l, vw, nq):
    o = jnp.dot(p, vw, preferred_element_type=F32) / l
    lse = jnp.broadcast_to(m + jnp.log(l), o.shape)
    head = lax.broadcasted_iota(jnp.int32, (nq, WIDTH_B), 1) // HEAD_DIM
    o_sel = jnp.zeros((nq, WIDTH_B), F32)
    lse_sel = jnp.zeros((nq, WIDTH_B), F32)
    for h in range(HEADS_PER_GROUP_B):
        rs = slice(h * nq, (h + 1) * nq)
        o_sel = jnp.where(head == h, o[rs], o_sel)
        lse_sel = jnp.where(head == h, lse[rs], lse_sel)
    return o_sel, lse_sel


def _dilated_kernel(*refs):
    q_refs, k_refs, v_refs = refs[0:3], refs[3:6], refs[6:9]
    ob_ref, o_scr, lse_scr = refs[9:12]
    tile = ob_ref.shape[0]
    i = pl.program_id(1)

    blocks = []
    for g, (window, dil) in enumerate(DILATED_GROUPS):
        side = window // (2 * dil)
        length = k_refs[g].shape[2]
        per_res = tile // dil
        nq = min(per_res, DIL_TQ)
        rows = lax.broadcasted_iota(jnp.int32, (HEADS_PER_GROUP_B * nq, DIL_WIN), 0)
        cols = lax.broadcasted_iota(jnp.int32, (HEADS_PER_GROUP_B * nq, DIL_WIN), 1)
        rel = (rows & (nq - 1)) - cols
        for r in range(dil):
            for u in range(per_res // nq):
                t0 = i * per_res + u * nq
                if length == DIL_WIN:
                    ks = 0
                else:
                    ks = pl.multiple_of(jnp.clip(t0 - side, 0, length - DIL_WIN), side)
                blocks.append(dict(g=g, dil=dil, r=r, u=u, nq=nq, t0=t0, ks=ks, side=side, rel=rel))

    def scores(bk):
        q = q_refs[bk["g"]][0, bk["r"], bk["u"] * bk["nq"]:(bk["u"] + 1) * bk["nq"], :]
        kw = k_refs[bk["g"]][0, bk["r"], pl.ds(bk["ks"], DIL_WIN), :]
        bk["s"] = _band_scores(q, kw)

    def softmax(bk):
        bk["pml"] = _band_softmax(bk.pop("s"), bk["rel"], bk["t0"] - bk["ks"], bk["side"])

    def output(bk):
        g, dil, r, u, nq = bk["g"], bk["dil"], bk["r"], bk["u"], bk["nq"]
        vw = v_refs[g][0, r, pl.ds(bk["ks"], DIL_WIN), :]
        o, lse = _band_output(*bk.pop("pml"), vw, nq)
        rows = pl.ds(u * nq * dil + r, nq, stride=dil) if dil > 1 else pl.ds(u * nq, nq)
        for j in range(WIDTH_B // LANES):
            o_scr[g, j, rows, :] = o[:, j * LANES:(j + 1) * LANES]
            lse_scr[g, j, rows, :] = lse[:, j * LANES:(j + 1) * LANES]

    for n in range(len(blocks) + 2 * DIL_SKEW):
        if n < len(blocks):
            scores(blocks[n])
        if 0 <= n - DIL_SKEW < len(blocks):
            softmax(blocks[n - DIL_SKEW])
        if 0 <= n - 2 * DIL_SKEW < len(blocks):
            output(blocks[n - 2 * DIL_SKEW])

    for j in range(WIDTH_B // LANES):
        l0, l1, l2 = lse_scr[0, j], lse_scr[1, j], lse_scr[2, j]
        m = jnp.maximum(jnp.maximum(l0, l1), l2)
        e0, e1, e2 = jnp.exp(l0 - m), jnp.exp(l1 - m), jnp.exp(l2 - m)
        ob = (e0 * o_scr[0, j] + e1 * o_scr[1, j] + e2 * o_scr[2, j]) / (e0 + e1 + e2)
        ob_ref[:, j * LANES:(j + 1) * LANES] = ob.astype(BF16)


def _dilated_attention(qs, ks, vs):
    b = qs[0].shape[0]
    seq = qs[0].shape[1] * qs[0].shape[2]
    tile = TOKEN_TILE
    tiles = seq // tile
    q_specs = [pl.BlockSpec((1, dil, tile // dil, WIDTH_B), lambda bi, i: (bi, 0, i, 0))
               for _, dil in DILATED_GROUPS]
    kv_specs = [pl.BlockSpec((1, dil, seq // dil, WIDTH_B), lambda bi, i: (bi, 0, 0, 0),
                             pipeline_mode=pl.Buffered(1))
                for _, dil in DILATED_GROUPS]
    return pl.pallas_call(
        _dilated_kernel,
        grid=(b, tiles),
        in_specs=q_specs + kv_specs + kv_specs,
        out_specs=pl.BlockSpec((tile, WIDTH_B), lambda bi, i: (bi * tiles + i, 0)),
        out_shape=jax.ShapeDtypeStruct((b * seq, WIDTH_B), BF16),
        scratch_shapes=[pltpu.VMEM((N_GROUPS_B, WIDTH_B // LANES, tile, LANES), F32)] * 2,
        compiler_params=_params(("parallel", "arbitrary"), 48),
        name="dilated_attn",
    )(*qs, *ks, *vs)


def _post_kernel(x_ref, mod_ref, oa_ref, ob_ref, gates_ref, wa_ref, wb_ref, wo_ref, g_ref, wu_ref, wd_ref,
                 out_ref):
    br_a = jnp.dot(oa_ref[...], wa_ref[...], preferred_element_type=F32)
    br_b = jnp.dot(ob_ref[...], wb_ref[...], preferred_element_type=F32)
    g_a = gates_ref[:, 0:D_MODEL].astype(F32)
    g_b = gates_ref[:, D_MODEL:GATE_COLS].astype(F32)
    y = jnp.dot((g_a * br_a + g_b * br_b).astype(BF16), wo_ref[...], preferred_element_type=F32)
    mod = lambda k: mod_ref[0, :, k * D_MODEL:(k + 1) * D_MODEL]
    x = x_ref[...] + mod(2) * y

    ms = jnp.mean(x * x, axis=-1, keepdims=True)
    h = (x * lax.rsqrt(ms + EPS) * g_ref[...] * (1.0 + mod(4)) + mod(3)).astype(BF16)
    u = jnp.maximum(jnp.dot(h, wu_ref[...], preferred_element_type=F32), 0.0)
    dn = jnp.dot((u * u).astype(BF16), wd_ref[...], preferred_element_type=F32)
    out_ref[...] = x + mod(5) * dn


def _post_attention(x2, mod3, oa, ob, gates, wa, wb, wo, g, wu, wd, seq):
    n, d = x2.shape
    tm = TOKEN_TILE
    row = lambda i: (i, 0)
    fixed = lambda i: (0, 0)
    return pl.pallas_call(
        _post_kernel,
        grid=(n // tm,),
        in_specs=[pl.BlockSpec((tm, d), row),
                  pl.BlockSpec((1, 1, mod3.shape[2]), lambda i: ((i * tm) // seq, 0, 0)),
                  pl.BlockSpec((tm, WIDTH_A), row),
                  pl.BlockSpec((tm, WIDTH_B), row),
                  pl.BlockSpec((tm, GATE_COLS), row),
                  _resident(wa.shape, fixed), _resident(wb.shape, fixed), _resident(wo.shape, fixed),
                  _resident((1, d), fixed), _resident(wu.shape, fixed), _resident(wd.shape, fixed)],
        out_specs=pl.BlockSpec((tm, d), row),
        out_shape=jax.ShapeDtypeStruct((n, d), F32),
        compiler_params=_params(("parallel",), 60),
        name="post_attn",
    )(x2, mod3, oa, ob, gates, wa, wb, wo, g, wu, wd)


def kernel(x, c, positions, ada_w, ada_b, norm_mix_g, norm_mlp_g, w_in, qk_gain_a, lambda_a, subln_g_a,
           qk_gain_b, w_branch_a, w_branch_b, gate_bias, w_out, w_mlp_up, w_mlp_down):
    b, s, d = x.shape
    depth = ada_w.shape[0]
    tabs = _rope_tables(positions)
    mod = _ada_mod(c, ada_w, ada_b)
    rep = LANES // HEAD_DIM
    x2 = x.reshape(b * s, d)
    for l in range(depth):
        mod3 = mod[l].reshape(b, 1, 6 * d)
        qa, ka, va, qbs, kbs, vbs, gates = _in_projection(
            x2, mod3, norm_mix_g[l].reshape(1, d), w_in[l].astype(BF16), tabs,
            jnp.tile(qk_gain_a[l], (1, rep)), jnp.tile(qk_gain_b[l], (1, rep)),
            gate_bias[l].reshape(1, GATE_COLS), s)

        lam_init = 0.8 - 0.6 * math.exp(-0.3 * l)
        oa = _diff_attention(qa.reshape(b, s, WIDTH_A), ka.reshape(b, s, WIDTH_A),
                             va.reshape(b, s, WIDTH_A), lambda_a[l], subln_g_a[l].reshape(VDIM_A, 1),
                             lam_init)
        ob = _dilated_attention(qbs, kbs, vbs)

        x2 = _post_attention(x2, mod3, oa.reshape(b * s, WIDTH_A), ob, gates,
                             w_branch_a[l].astype(BF16), w_branch_b[l].astype(BF16), w_out[l].astype(BF16),
                             norm_mlp_g[l].reshape(1, d), w_mlp_up[l].astype(BF16),
                             w_mlp_down[l].astype(BF16), s)
    return x2.reshape(b, s, d)
```

```python
import math
from functools import partial

import jax
import jax.numpy as jnp
from jax import lax
from jax.experimental import pallas as pl
from jax.experimental.pallas import tpu as pltpu

D_MODEL = 1024
HEAD_DIM = 64
HEADS_A = 8
VDIM_A = 2 * HEAD_DIM
WIDTH_A = HEADS_A * VDIM_A
DILATED_GROUPS = ((128, 1), (512, 4), (2048, 16))
N_GROUPS_B = 3
HEADS_PER_GROUP_B = 4
WIDTH_B = HEADS_PER_GROUP_B * HEAD_DIM
QKV_B_COLS = N_GROUPS_B * WIDTH_B
D_FF = 4 * D_MODEL
ROPE_THETA = 500000.0
ROT_HALF = HEAD_DIM // 4 // 2
EPS = 1e-6
NEG_INF = -1e30
GATE_COLS = 2 * D_MODEL
IN_COLS = 3 * WIDTH_A + 3 * QKV_B_COLS + GATE_COLS
SM_SCALE = HEAD_DIM ** -0.5
LOG2E = math.log2(math.e)

LANES = 128
PROJ_CHUNK = 256
TOKEN_TILE = 512
VMEM_BYTES = 64 * 1024 * 1024
MIB = 1024 * 1024

BF16 = jnp.bfloat16
F32 = jnp.float32
NT_DIMS = (((1,), (1,)), ((), ()))


def _params(semantics, vmem_mib):
    return pltpu.CompilerParams(dimension_semantics=semantics,
                                vmem_limit_bytes=min(vmem_mib * MIB, VMEM_BYTES - 4 * MIB))


def _resident(shape, index_map):
    return pl.BlockSpec(shape, index_map, pipeline_mode=pl.Buffered(1))


def _rope_kernel(pos_ref, cos_ref, sin_ref):
    pos = pos_ref[...].astype(F32)
    for f in range(ROT_HALF):
        ang = pos * (ROPE_THETA ** (-(2.0 * f) / (2 * ROT_HALF)))
        cos_ref[f] = jnp.cos(ang)
        sin_ref[f] = jnp.sin(ang)


def _rope_tables(positions):
    n = positions.size
    pos2 = positions.reshape(n // LANES, LANES)
    cos, sin = pl.pallas_call(
        _rope_kernel,
        out_shape=[jax.ShapeDtypeStruct((ROT_HALF, n // LANES, LANES), F32)] * 2,
    )(pos2)
    cos = cos.reshape(ROT_HALF, n).T
    sin = sin.reshape(ROT_HALF, n).T
    one = jnp.ones((n, HEAD_DIM - 2 * ROT_HALF), F32)
    zero = jnp.zeros((n, HEAD_DIM - 2 * ROT_HALF), F32)
    z8 = jnp.zeros((n, ROT_HALF), F32)
    c_tab = jnp.concatenate([cos, cos, one], axis=1)
    s1_tab = jnp.concatenate([z8, sin, zero], axis=1)
    s2_tab = jnp.concatenate([sin, z8, zero], axis=1)
    rep = LANES // HEAD_DIM
    return tuple(jnp.tile(t, (1, rep)) for t in (c_tab, s1_tab, s2_tab))


def _ada_kernel(c_ref, w_ref, b_ref, o_ref):
    c = c_ref[...]
    cond = (c * jax.nn.sigmoid(c)).astype(BF16)
    o_ref[0] = jnp.dot(cond, w_ref[0].astype(BF16), preferred_element_type=F32) + b_ref[0]


def _ada_mod(c, ada_w, ada_b):
    depth, d, n = ada_w.shape
    b = c.shape[0]
    tn = 768
    return pl.pallas_call(
        _ada_kernel,
        grid=(depth, n // tn),
        in_specs=[pl.BlockSpec((b, d), lambda l, j: (0, 0)),
                  pl.BlockSpec((1, d, tn), lambda l, j: (l, 0, j)),
                  pl.BlockSpec((1, 1, tn), lambda l, j: (l, 0, j))],
        out_specs=pl.BlockSpec((1, b, tn), lambda l, j: (l, 0, j)),
        out_shape=jax.ShapeDtypeStruct((depth, b, n), F32),
        compiler_params=_params(("parallel", "parallel"), 32),
    )(c, ada_w, ada_b.reshape(depth, 1, n))


def _head_norm_rope(t, gain, c_tab, s1_tab, s2_tab):
    lane = lax.broadcasted_iota(jnp.int32, t.shape, 1)
    lo = lane < HEAD_DIM
    sq = t * t
    s_lo = jnp.sum(jnp.where(lo, sq, 0.0), axis=-1, keepdims=True)
    s_hi = jnp.sum(jnp.where(lo, 0.0, sq), axis=-1, keepdims=True)
    ms = jnp.where(lo, s_lo, s_hi) * (1.0 / HEAD_DIM)
    y = t * lax.rsqrt(ms + EPS) * gain
    return (y * c_tab + pltpu.roll(y, ROT_HALF, 1) * s1_tab
            - pltpu.roll(y, LANES - ROT_HALF, 1) * s2_tab)


def _inproj_kernel(x_ref, mod_ref, g_ref, w_ref, c_ref, s1_ref, s2_ref, gain_a_ref, gain_b_ref,
                   gbias_ref, qa_ref, ka_ref, va_ref, *rest):
    qb_refs, kb_refs, vb_refs = rest[0:3], rest[3:6], rest[6:9]
    gates_ref, h_scr, y_scr = rest[9:12]
    tm = x_ref.shape[0]
    x = x_ref[...]
    ms = jnp.mean(x * x, axis=-1, keepdims=True)
    shift = mod_ref[0, :, 0:D_MODEL]
    scale = mod_ref[0, :, D_MODEL:2 * D_MODEL]
    h = x * lax.rsqrt(ms + EPS) * g_ref[...] * (1.0 + scale) + shift
    h_scr[...] = h.astype(BF16)

    c_tab, s1_tab, s2_tab = c_ref[...], s1_ref[...], s2_ref[...]
    cw = PROJ_CHUNK

    def proj(col):
        return jnp.dot(h_scr[...], w_ref[:, col:col + cw], preferred_element_type=F32)

    def normed(out_ref, col0, width, gain):
        for c in range(0, width, cw):
            acc = proj(col0 + c)
            for j in range(0, cw, LANES):
                out_ref[:, c + j:c + j + LANES] = _head_norm_rope(
                    acc[:, j:j + LANES], gain, c_tab, s1_tab, s2_tab).astype(BF16)

    def plain(out_ref, col0, width):
        for c in range(0, width, cw):
            out_ref[:, c:c + cw] = proj(col0 + c).astype(BF16)

    def split_residues(out_ref, dilation):
        for r in range(dilation):
            for j in range(WIDTH_B // LANES):
                out_ref[0, r, :, j * LANES:(j + 1) * LANES] = (
                    y_scr[j, pl.ds(r, tm // dilation, stride=dilation), :].astype(BF16))

    def normed_groups(out_refs, col0, gain):
        for g, (_, dilation) in enumerate(DILATED_GROUPS):
            acc = proj(col0 + g * WIDTH_B)
            for j in range(0, WIDTH_B, LANES):
                y_scr[j // LANES] = _head_norm_rope(acc[:, j:j + LANES], gain, c_tab, s1_tab, s2_tab)
            split_residues(out_refs[g], dilation)

    def plain_groups(out_refs, col0):
        for g, (_, dilation) in enumerate(DILATED_GROUPS):
            acc = proj(col0 + g * WIDTH_B)
            for j in range(0, WIDTH_B, LANES):
                y_scr[j // LANES] = acc[:, j:j + LANES]
            split_residues(out_refs[g], dilation)

    col = 0
    normed(qa_ref, col, WIDTH_A, gain_a_ref[0:1, :] * (SM_SCALE * LOG2E)); col += WIDTH_A
    normed(ka_ref, col, WIDTH_A, gain_a_ref[1:2, :]); col += WIDTH_A
    plain(va_ref, col, WIDTH_A); col += WIDTH_A
    normed_groups(qb_refs, col, gain_b_ref[0:1, :] * SM_SCALE); col += QKV_B_COLS
    normed_groups(kb_refs, col, gain_b_ref[1:2, :]); col += QKV_B_COLS
    plain_groups(vb_refs, col); col += QKV_B_COLS
    for c in range(0, GATE_COLS, cw):
        acc = proj(col + c) + gbias_ref[:, c:c + cw]
        gates_ref[:, c:c + cw] = jax.nn.sigmoid(acc).astype(BF16)


def _in_projection(x2, mod3, g, w_bf, tabs, gain_a, gain_b, gate_bias, seq):
    n, d = x2.shape
    tm = TOKEN_TILE
    tiles = seq // tm
    row = lambda i: (i, 0)
    fixed = lambda i: (0, 0)
    res_block = lambda i: (i // tiles, 0, i % tiles, 0)
    wide = [pl.BlockSpec((tm, WIDTH_A), row)] * 3
    wide_shape = [jax.ShapeDtypeStruct((n, WIDTH_A), BF16)] * 3
    grp = [pl.BlockSpec((1, dil, tm // dil, WIDTH_B), res_block) for _, dil in DILATED_GROUPS] * 3
    grp_shape = [jax.ShapeDtypeStruct((n // seq, dil, seq // dil, WIDTH_B), BF16) for _, dil in DILATED_GROUPS] * 3
    outs = pl.pallas_call(
        _inproj_kernel,
        grid=(n // tm,),
        in_specs=[pl.BlockSpec((tm, d), row),
                  pl.BlockSpec((1, 1, mod3.shape[2]), lambda i: (i // tiles, 0, 0)),
                  _resident((1, d), fixed),
                  _resident((d, IN_COLS), fixed),
                  pl.BlockSpec((tm, LANES), row),
                  pl.BlockSpec((tm, LANES), row),
                  pl.BlockSpec((tm, LANES), row),
                  _resident((2, LANES), fixed),
                  _resident((2, LANES), fixed),
                  _resident((1, GATE_COLS), fixed)],
        out_specs=wide + grp + [pl.BlockSpec((tm, GATE_COLS), row)],
        out_shape=wide_shape + grp_shape + [jax.ShapeDtypeStruct((n, GATE_COLS), BF16)],
        scratch_shapes=[pltpu.VMEM((tm, d), BF16), pltpu.VMEM((WIDTH_B // LANES, tm, LANES), F32)],
        compiler_params=_params(("parallel",), 56),
        name="in_proj",
    )(x2, mod3, g, w_bf, *tabs, gain_a, gain_b, gate_bias)
    return outs[0], outs[1], outs[2], outs[3:6], outs[6:9], outs[9:12], outs[12]


DA_TQ = 256
DA_TK = 512
DA_FINISH_AFTER = 1024
DA_ONES_ROWS = 16


def _diff_attn_kernel(lam_ref, g_ref, q_ref, k_ref, v_ref, o_ref, sa_scr, sb_scr, ma_scr, mb_scr,
                      acca_scr, accb_scr, vt_scr, *, nq, lam_init):
    seq = k_ref.shape[1]
    tq = q_ref.shape[1]
    t = pl.program_id(0)

    @pl.when(t == 0)
    def _():
        sb_scr[...] = jnp.zeros(sb_scr.shape, F32)
        mb_scr[...] = jnp.zeros(mb_scr.shape, F32)
        accb_scr[...] = jnp.ones(accb_scr.shape, F32)

    @pl.when((t == 0) | ((t - 1) % nq == 0))
    def _():
        vt_scr[0:VDIM_A, :] = v_ref[0].T
        vt_scr[VDIM_A:VDIM_A + DA_ONES_ROWS, :] = jnp.ones((DA_ONES_ROWS, seq), BF16)

    def finish(acc_r):
        o = acc_r[0:VDIM_A, :] / acc_r[VDIM_A:VDIM_A + 1, :]
        lv = lam_ref[...]
        lam = (jnp.exp(jnp.sum(lv[0:1] * lv[1:2], axis=-1, keepdims=True))
               - jnp.exp(jnp.sum(lv[2:3] * lv[3:4], axis=-1, keepdims=True)) + lam_init)
        out = o[:, :tq] - lam * o[:, tq:]
        ms = jnp.mean(out * out, axis=0, keepdims=True)
        y = out * lax.rsqrt(ms + EPS) * g_ref[...] * (1.0 - lam_init)
        o_ref[0] = y.T.astype(BF16)

    def step(s_w, m_w, acc_w, s_r, m_r, acc_r):
        q = q_ref[0]
        lane = lax.broadcasted_iota(jnp.int32, q.shape, 1)
        zero = jnp.zeros_like(q)
        qs = jnp.concatenate([jnp.where(lane < HEAD_DIM, q, zero),
                              jnp.where(lane < HEAD_DIM, zero, q)], axis=0)
        m_prev = m_r[0:1]
        m_new = None
        acc = None
        for c in range(0, seq, DA_TK):
            st = lax.dot_general(k_ref[0, c:c + DA_TK, :], qs, NT_DIMS, preferred_element_type=F32)
            s_w[c:c + DA_TK, :] = st
            mc = jnp.max(st, axis=0, keepdims=True)
            m_new = mc if m_new is None else jnp.maximum(m_new, mc)
            pt = jnp.exp2((s_r[c:c + DA_TK, :] - m_prev).astype(BF16))
            part = jnp.dot(vt_scr[:, c:c + DA_TK], pt, preferred_element_type=F32)
            acc = part if acc is None else acc + part
            if c == DA_FINISH_AFTER:
                finish(acc_r)
        m_w[...] = jnp.broadcast_to(m_new, m_w.shape)
        acc_w[...] = acc

    @pl.when(t % 2 == 0)
    def _():
        step(sa_scr, ma_scr, acca_scr, sb_scr, mb_scr, accb_scr)

    @pl.when(t % 2 == 1)
    def _():
        step(sb_scr, mb_scr, accb_scr, sa_scr, ma_scr, acca_scr)


def _diff_attention(qa, ka, va, lam_p, subln_g, lam_init):
    b, s, _ = qa.shape
    nq = s // DA_TQ
    n = b * HEADS_A * nq

    def blk(t):
        t = jnp.clip(t, 0, n - 1)
        return t // (HEADS_A * nq), (t // nq) % HEADS_A, t % nq

    def q_map(t):
        bi, h, i = blk(t)
        return bi, i, h

    def k_map(t):
        bi, h, _ = blk(t)
        return bi, 0, h

    def v_map(t):
        bi, h, _ = blk(t - 1)
        return bi, 0, h

    def o_map(t):
        bi, h, i = blk(t - 2)
        return bi, i, h

    score_buf = pltpu.VMEM((s, 2 * DA_TQ), F32)
    max_buf = pltpu.VMEM((8, 2 * DA_TQ), F32)
    acc_buf = pltpu.VMEM((VDIM_A + DA_ONES_ROWS, 2 * DA_TQ), F32)
    return pl.pallas_call(
        partial(_diff_attn_kernel, nq=nq, lam_init=lam_init),
        grid=(n + 2,),
        in_specs=[pl.BlockSpec(lam_p.shape, lambda t: (0, 0)),
                  pl.BlockSpec((VDIM_A, 1), lambda t: (0, 0)),
                  pl.BlockSpec((1, DA_TQ, VDIM_A), q_map),
                  pl.BlockSpec((1, s, VDIM_A), k_map),
                  pl.BlockSpec((1, s, VDIM_A), v_map)],
        out_specs=pl.BlockSpec((1, DA_TQ, VDIM_A), o_map),
        out_shape=jax.ShapeDtypeStruct((b, s, WIDTH_A), BF16),
        scratch_shapes=[score_buf, score_buf, max_buf, max_buf, acc_buf, acc_buf,
                        pltpu.VMEM((VDIM_A + DA_ONES_ROWS, s), BF16)],
        compiler_params=_params(("arbitrary",), 40),
        name="diff_attn",
    )(lam_p, subln_g, qa, ka, va)


DIL_TQ = 128
DIL_SKEW = 1
DIL_WIN = 256


def _band_scores(q, kw):
    head = lax.broadcasted_iota(jnp.int32, q.shape, 1) // HEAD_DIM
    zero = jnp.zeros_like(q)
    qs = jnp.concatenate([jnp.where(head == h, q, zero) for h in range(HEADS_PER_GROUP_B)], axis=0)
    return lax.dot_general(qs, kw, NT_DIMS, preferred_element_type=F32)


def _band_softmax(s, rel, offset, side):
    s = jnp.where(jnp.abs(rel + offset) <= side, s, NEG_INF)
    m = jnp.max(s, axis=-1, keepdims=True)
    p = jnp.exp(s - m)
    l = jnp.sum(p, axis=-1, keepdims=True)
    return p.astype(BF16), m, l


def _band_output(p, m, l, vw, nq):
    o = jnp.dot(p, vw, preferred_element_type=F32) / l
    lse = jnp.broadcast_to(m + jnp.log(l), o.shape)
    head = lax.broadcasted_iota(jnp.int32, (nq, WIDTH_B), 1) // HEAD_DIM
    o_sel = jnp.zeros((nq, WIDTH_B), F32)
    lse_sel = jnp.zeros((nq, WIDTH_B), F32)
    for h in range(HEADS_PER_GROUP_B):
        rs = slice(h * nq, (h + 1) * nq)
        o_sel = jnp.where(head == h, o[rs], o_sel)
        lse_sel = jnp.where(head == h, lse[rs], lse_sel)
    return o_sel, lse_sel


def _dilated_kernel(*refs):
    q_refs, k_refs, v_refs = refs[0:3], refs[3:6], refs[6:9]
    ob_ref, o_scr, lse_scr = refs[9:12]
    tile = ob_ref.shape[0]
    i = pl.program_id(1)

    blocks = []
    for g, (window, dil) in enumerate(DILATED_GROUPS):
        side = window // (2 * dil)
        length = k_refs[g].shape[2]
        per_res = tile // dil
        nq = min(per_res, DIL_TQ)
        rows = lax.broadcasted_iota(jnp.int32, (HEADS_PER_GROUP_B * nq, DIL_WIN), 0)
        cols = lax.broadcasted_iota(jnp.int32, (HEADS_PER_GROUP_B * nq, DIL_WIN), 1)
        rel = (rows & (nq - 1)) - cols
        for r in range(dil):
            for u in range(per_res // nq):
                t0 = i * per_res + u * nq
                if length == DIL_WIN:
                    ks = 0
                else:
                    ks = pl.multiple_of(jnp.clip(t0 - side, 0, length - DIL_WIN), side)
                blocks.append(dict(g=g, dil=dil, r=r, u=u, nq=nq, t0=t0, ks=ks, side=side, rel=rel))

    def scores(bk):
        q = q_refs[bk["g"]][0, bk["r"], bk["u"] * bk["nq"]:(bk["u"] + 1) * bk["nq"], :]
        kw = k_refs[bk["g"]][0, bk["r"], pl.ds(bk["ks"], DIL_WIN), :]
        bk["s"] = _band_scores(q, kw)

    def softmax(bk):
        bk["pml"] = _band_softmax(bk.pop("s"), bk["rel"], bk["t0"] - bk["ks"], bk["side"])

    def output(bk):
        g, dil, r, u, nq = bk["g"], bk["dil"], bk["r"], bk["u"], bk["nq"]
        vw = v_refs[g][0, r, pl.ds(bk["ks"], DIL_WIN), :]
        o, lse = _band_output(*bk.pop("pml"), vw, nq)
        rows = pl.ds(u * nq * dil + r, nq, stride=dil) if dil > 1 else pl.ds(u * nq, nq)
        for j in range(WIDTH_B // LANES):
            o_scr[g, j, rows, :] = o[:, j * LANES:(j + 1) * LANES]
            lse_scr[g, j, rows, :] = lse[:, j * LANES:(j + 1) * LANES]

    for n in range(len(blocks) + 2 * DIL_SKEW):
        if n < len(blocks):
            scores(blocks[n])
        if 0 <= n - DIL_SKEW < len(blocks):
            softmax(blocks[n - DIL_SKEW])
        if 0 <= n - 2 * DIL_SKEW < len(blocks):
            output(blocks[n - 2 * DIL_SKEW])

    for j in range(WIDTH_B // LANES):
        l0, l1, l2 = lse_scr[0, j], lse_scr[1, j], lse_scr[2, j]
        m = jnp.maximum(jnp.maximum(l0, l1), l2)
        e0, e1, e2 = jnp.exp(l0 - m), jnp.exp(l1 - m), jnp.exp(l2 - m)
        ob = (e0 * o_scr[0, j] + e1 * o_scr[1, j] + e2 * o_scr[2, j]) / (e0 + e1 + e2)
        ob_ref[:, j * LANES:(j + 1) * LANES] = ob.astype(BF16)


def _dilated_attention(qs, ks, vs):
    b = qs[0].shape[0]
    seq = qs[0].shape[1] * qs[0].shape[2]
    tile = TOKEN_TILE
    tiles = seq // tile
    q_specs = [pl.BlockSpec((1, dil, tile // dil, WIDTH_B), lambda bi, i: (bi, 0, i, 0))
               for _, dil in DILATED_GROUPS]
    kv_specs = [pl.BlockSpec((1, dil, seq // dil, WIDTH_B), lambda bi, i: (bi, 0, 0, 0),
                             pipeline_mode=pl.Buffered(1))
                for _, dil in DILATED_GROUPS]
    return pl.pallas_call(
        _dilated_kernel,
        grid=(b, tiles),
        in_specs=q_specs + kv_specs + kv_specs,
        out_specs=pl.BlockSpec((tile, WIDTH_B), lambda bi, i: (bi * tiles + i, 0)),
        out_shape=jax.ShapeDtypeStruct((b * seq, WIDTH_B), BF16),
        scratch_shapes=[pltpu.VMEM((N_GROUPS_B, WIDTH_B // LANES, tile, LANES), F32)] * 2,
        compiler_params=_params(("parallel", "arbitrary"), 48),
        name="dilated_attn",
    )(*qs, *ks, *vs)


def _post_kernel(x_ref, mod_ref, oa_ref, ob_ref, gates_ref, wa_ref, wb_ref, wo_ref, g_ref, wu_ref, wd_ref,
                 out_ref):
    br_a = jnp.dot(oa_ref[...], wa_ref[...], preferred_element_type=F32)
    br_b = jnp.dot(ob_ref[...], wb_ref[...], preferred_element_type=F32)
    g_a = gates_ref[:, 0:D_MODEL].astype(F32)
    g_b = gates_ref[:, D_MODEL:GATE_COLS].astype(F32)
    y = jnp.dot((g_a * br_a + g_b * br_b).astype(BF16), wo_ref[...], preferred_element_type=F32)
    mod = lambda k: mod_ref[0, :, k * D_MODEL:(k + 1) * D_MODEL]
    x = x_ref[...] + mod(2) * y

    ms = jnp.mean(x * x, axis=-1, keepdims=True)
    h = (x * lax.rsqrt(ms + EPS) * g_ref[...] * (1.0 + mod(4)) + mod(3)).astype(BF16)
    u = jnp.maximum(jnp.dot(h, wu_ref[...], preferred_element_type=F32), 0.0)
    dn = jnp.dot((u * u).astype(BF16), wd_ref[...], preferred_element_type=F32)
    out_ref[...] = x + mod(5) * dn


def _post_attention(x2, mod3, oa, ob, gates, wa, wb, wo, g, wu, wd, seq):
    n, d = x2.shape
    tm = TOKEN_TILE
    row = lambda i: (i, 0)
    fixed = lambda i: (0, 0)
    return pl.pallas_call(
        _post_kernel,
        grid=(n // tm,),
        in_specs=[pl.BlockSpec((tm, d), row),
                  pl.BlockSpec((1, 1, mod3.shape[2]), lambda i: ((i * tm) // seq, 0, 0)),
                  pl.BlockSpec((tm, WIDTH_A), row),
                  pl.BlockSpec((tm, WIDTH_B), row),
                  pl.BlockSpec((tm, GATE_COLS), row),
                  _resident(wa.shape, fixed), _resident(wb.shape, fixed), _resident(wo.shape, fixed),
                  _resident((1, d), fixed), _resident(wu.shape, fixed), _resident(wd.shape, fixed)],
        out_specs=pl.BlockSpec((tm, d), row),
        out_shape=jax.ShapeDtypeStruct((n, d), F32),
        compiler_params=_params(("parallel",), 60),
        name="post_attn",
    )(x2, mod3, oa, ob, gates, wa, wb, wo, g, wu, wd)


def kernel(x, c, positions, ada_w, ada_b, norm_mix_g, norm_mlp_g, w_in, qk_gain_a, lambda_a, subln_g_a,
           qk_gain_b, w_branch_a, w_branch_b, gate_bias, w_out, w_mlp_up, w_mlp_down):
    b, s, d = x.shape
    depth = ada_w.shape[0]
    tabs = _rope_tables(positions)
    mod = _ada_mod(c, ada_w, ada_b)
    rep = LANES // HEAD_DIM
    x2 = x.reshape(b * s, d)
    for l in range(depth):
        mod3 = mod[l].reshape(b, 1, 6 * d)
        qa, ka, va, qbs, kbs, vbs, gates = _in_projection(
            x2, mod3, norm_mix_g[l].reshape(1, d), w_in[l].astype(BF16), tabs,
            jnp.tile(qk_gain_a[l], (1, rep)), jnp.tile(qk_gain_b[l], (1, rep)),
            gate_bias[l].reshape(1, GATE_COLS), s)

        lam_init = 0.8 - 0.6 * math.exp(-0.3 * l)
        oa = _diff_attention(qa.reshape(b, s, WIDTH_A), ka.reshape(b, s, WIDTH_A),
                             va.reshape(b, s, WIDTH_A), lambda_a[l], subln_g_a[l].reshape(VDIM_A, 1),
                             lam_init)
        ob = _dilated_attention(qbs, kbs, vbs)

        x2 = _post_attention(x2, mod3, oa.reshape(b * s, WIDTH_A), ob, gates,
                             w_branch_a[l].astype(BF16), w_branch_b[l].astype(BF16), w_out[l].astype(BF16),
                             norm_mlp_g[l].reshape(1, d), w_mlp_up[l].astype(BF16),
                             w_mlp_down[l].astype(BF16), s)
    return x2.reshape(b, s, d)
```

```python
import math
from functools import partial

import jax
import jax.numpy as jnp
from jax import lax
from jax.experimental import pallas as pl
from jax.experimental.pallas import tpu as pltpu

D_MODEL = 1024
HEAD_DIM = 64
HEADS_A = 8
VDIM_A = 2 * HEAD_DIM
WIDTH_A = HEADS_A * VDIM_A
DILATED_GROUPS = ((128, 1), (512, 4), (2048, 16))
N_GROUPS_B = 3
HEADS_PER_GROUP_B = 4
WIDTH_B = HEADS_PER_GROUP_B * HEAD_DIM
QKV_B_COLS = N_GROUPS_B * WIDTH_B
D_FF = 4 * D_MODEL
ROPE_THETA = 500000.0
ROT_HALF = HEAD_DIM // 4 // 2
EPS = 1e-6
NEG_INF = -1e30
GATE_COLS = 2 * D_MODEL
IN_COLS = 3 * WIDTH_A + 3 * QKV_B_COLS + GATE_COLS
SM_SCALE = HEAD_DIM ** -0.5
LOG2E = math.log2(math.e)

LANES = 128
PROJ_CHUNK = 256
TOKEN_TILE = 512
VMEM_BYTES = 64 * 1024 * 1024
MIB = 1024 * 1024

BF16 = jnp.bfloat16
F32 = jnp.float32
NT_DIMS = (((1,), (1,)), ((), ()))


def _params(semantics, vmem_mib):
    return pltpu.CompilerParams(dimension_semantics=semantics,
                                vmem_limit_bytes=min(vmem_mib * MIB, VMEM_BYTES - 4 * MIB))


def _resident(shape, index_map):
    return pl.BlockSpec(shape, index_map, pipeline_mode=pl.Buffered(1))


def _rope_kernel(pos_ref, cos_ref, sin_ref):
    pos = pos_ref[...].astype(F32)
    for f in range(ROT_HALF):
        ang = pos * (ROPE_THETA ** (-(2.0 * f) / (2 * ROT_HALF)))
        cos_ref[f] = jnp.cos(ang)
        sin_ref[f] = jnp.sin(ang)


def _rope_tables(positions):
    n = positions.size
    pos2 = positions.reshape(n // LANES, LANES)
    cos, sin = pl.pallas_call(
        _rope_kernel,
        out_shape=[jax.ShapeDtypeStruct((ROT_HALF, n // LANES, LANES), F32)] * 2,
    )(pos2)
    cos = cos.reshape(ROT_HALF, n).T
    sin = sin.reshape(ROT_HALF, n).T
    one = jnp.ones((n, HEAD_DIM - 2 * ROT_HALF), F32)
    zero = jnp.zeros((n, HEAD_DIM - 2 * ROT_HALF), F32)
    z8 = jnp.zeros((n, ROT_HALF), F32)
    c_tab = jnp.concatenate([cos, cos, one], axis=1)
    s1_tab = jnp.concatenate([z8, sin, zero], axis=1)
    s2_tab = jnp.concatenate([sin, z8, zero], axis=1)
    rep = LANES // HEAD_DIM
    return tuple(jnp.tile(t, (1, rep)) for t in (c_tab, s1_tab, s2_tab))


def _ada_kernel(c_ref, w_ref, b_ref, o_ref):
    c = c_ref[...]
    cond = (c * jax.nn.sigmoid(c)).astype(BF16)
    o_ref[0] = jnp.dot(cond, w_ref[0].astype(BF16), preferred_element_type=F32) + b_ref[0]


def _ada_mod(c, ada_w, ada_b):
    depth, d, n = ada_w.shape
    b = c.shape[0]
    tn = 768
    return pl.pallas_call(
        _ada_kernel,
        grid=(depth, n // tn),
        in_specs=[pl.BlockSpec((b, d), lambda l, j: (0, 0)),
                  pl.BlockSpec((1, d, tn), lambda l, j: (l, 0, j)),
                  pl.BlockSpec((1, 1, tn), lambda l, j: (l, 0, j))],
        out_specs=pl.BlockSpec((1, b, tn), lambda l, j: (l, 0, j)),
        out_shape=jax.ShapeDtypeStruct((depth, b, n), F32),
        compiler_params=_params(("parallel", "parallel"), 32),
    )(c, ada_w, ada_b.reshape(depth, 1, n))


def _head_norm_rope(t, gain, c_tab, s1_tab, s2_tab):
    lane = lax.broadcasted_iota(jnp.int32, t.shape, 1)
    lo = lane < HEAD_DIM
    sq = t * t
    s_lo = jnp.sum(jnp.where(lo, sq, 0.0), axis=-1, keepdims=True)
    s_hi = jnp.sum(jnp.where(lo, 0.0, sq), axis=-1, keepdims=True)
    ms = jnp.where(lo, s_lo, s_hi) * (1.0 / HEAD_DIM)
    y = t * lax.rsqrt(ms + EPS) * gain
    return (y * c_tab + pltpu.roll(y, ROT_HALF, 1) * s1_tab
            - pltpu.roll(y, LANES - ROT_HALF, 1) * s2_tab)


def _inproj_kernel(x_ref, mod_ref, g_ref, w_ref, c_ref, s1_ref, s2_ref, gain_a_ref, gain_b_ref,
                   gbias_ref, qa_ref, ka_ref, va_ref, *rest):
    qb_refs, kb_refs, vb_refs = rest[0:3], rest[3:6], rest[6:9]
    gates_ref, h_scr, y_scr = rest[9:12]
    tm = x_ref.shape[0]
    x = x_ref[...]
    ms = jnp.mean(x * x, axis=-1, keepdims=True)
    shift = mod_ref[0, :, 0:D_MODEL]
    scale = mod_ref[0, :, D_MODEL:2 * D_MODEL]
    h = x * lax.rsqrt(ms + EPS) * g_ref[...] * (1.0 + scale) + shift
    h_scr[...] = h.astype(BF16)

    c_tab, s1_tab, s2_tab = c_ref[...], s1_ref[...], s2_ref[...]
    cw = PROJ_CHUNK

    def proj(col):
        return jnp.dot(h_scr[...], w_ref[:, col:col + cw], preferred_element_type=F32)

    def normed(out_ref, col0, width, gain):
        for c in range(0, width, cw):
            acc = proj(col0 + c)
            for j in range(0, cw, LANES):
                out_ref[:, c + j:c + j + LANES] = _head_norm_rope(
                    acc[:, j:j + LANES], gain, c_tab, s1_tab, s2_tab).astype(BF16)

    def plain(out_ref, col0, width):
        for c in range(0, width, cw):
            out_ref[:, c:c + cw] = proj(col0 + c).astype(BF16)

    def split_residues(out_ref, dilation):
        for r in range(dilation):
            for j in range(WIDTH_B // LANES):
                out_ref[0, r, :, j * LANES:(j + 1) * LANES] = (
                    y_scr[j, pl.ds(r, tm // dilation, stride=dilation), :].astype(BF16))

    def normed_groups(out_refs, col0, gain):
        for g, (_, dilation) in enumerate(DILATED_GROUPS):
            acc = proj(col0 + g * WIDTH_B)
            for j in range(0, WIDTH_B, LANES):
                y_scr[j // LANES] = _head_norm_rope(acc[:, j:j + LANES], gain, c_tab, s1_tab, s2_tab)
            split_residues(out_refs[g], dilation)

    def plain_groups(out_refs, col0):
        for g, (_, dilation) in enumerate(DILATED_GROUPS):
            acc = proj(col0 + g * WIDTH_B)
            for j in range(0, WIDTH_B, LANES):
                y_scr[j // LANES] = acc[:, j:j + LANES]
            split_residues(out_refs[g], dilation)

    col = 0
    normed(qa_ref, col, WIDTH_A, gain_a_ref[0:1, :] * (SM_SCALE * LOG2E)); col += WIDTH_A
    normed(ka_ref, col, WIDTH_A, gain_a_ref[1:2, :]); col += WIDTH_A
    plain(va_ref, col, WIDTH_A); col += WIDTH_A
    normed_groups(qb_refs, col, gain_b_ref[0:1, :] * SM_SCALE); col += QKV_B_COLS
    normed_groups(kb_refs, col, gain_b_ref[1:2, :]); col += QKV_B_COLS
    plain_groups(vb_refs, col); col += QKV_B_COLS
    for c in range(0, GATE_COLS, cw):
        acc = proj(col + c) + gbias_ref[:, c:c + cw]
        gates_ref[:, c:c + cw] = jax.nn.sigmoid(acc).astype(BF16)


def _in_projection(x2, mod3, g, w_bf, tabs, gain_a, gain_b, gate_bias, seq):
    n, d = x2.shape
    tm = TOKEN_TILE
    tiles = seq // tm
    row = lambda i: (i, 0)
    fixed = lambda i: (0, 0)
    res_block = lambda i: (i // tiles, 0, i % tiles, 0)
    wide = [pl.BlockSpec((tm, WIDTH_A), row)] * 3
    wide_shape = [jax.ShapeDtypeStruct((n, WIDTH_A), BF16)] * 3
    grp = [pl.BlockSpec((1, dil, tm // dil, WIDTH_B), res_block) for _, dil in DILATED_GROUPS] * 3
    grp_shape = [jax.ShapeDtypeStruct((n // seq, dil, seq // dil, WIDTH_B), BF16) for _, dil in DILATED_GROUPS] * 3
    outs = pl.pallas_call(
        _inproj_kernel,
        grid=(n // tm,),
        in_specs=[pl.BlockSpec((tm, d), row),
                  pl.BlockSpec((1, 1, mod3.shape[2]), lambda i: (i // tiles, 0, 0)),
                  _resident((1, d), fixed),
                  _resident((d, IN_COLS), fixed),
                  pl.BlockSpec((tm, LANES), row),
                  pl.BlockSpec((tm, LANES), row),
                  pl.BlockSpec((tm, LANES), row),
                  _resident((2, LANES), fixed),
                  _resident((2, LANES), fixed),
                  _resident((1, GATE_COLS), fixed)],
        out_specs=wide + grp + [pl.BlockSpec((tm, GATE_COLS), row)],
        out_shape=wide_shape + grp_shape + [jax.ShapeDtypeStruct((n, GATE_COLS), BF16)],
        scratch_shapes=[pltpu.VMEM((tm, d), BF16), pltpu.VMEM((WIDTH_B // LANES, tm, LANES), F32)],
        compiler_params=_params(("parallel",), 56),
        name="in_proj",
    )(x2, mod3, g, w_bf, *tabs, gain_a, gain_b, gate_bias)
    return outs[0], outs[1], outs[2], outs[3:6], outs[6:9], outs[9:12], outs[12]


DA_TQ = 512
DA_TK = 512
DA_FINISH_AFTER = 1024
DA_ONES_ROWS = 16


def _diff_attn_kernel(lam_ref, g_ref, q_ref, k_ref, v_ref, o_ref, sa_scr, sb_scr, ma_scr, mb_scr,
                      acca_scr, accb_scr, vt_scr, *, nq, lam_init):
    seq = k_ref.shape[1]
    tq = q_ref.shape[1]
    t = pl.program_id(0)

    @pl.when(t == 0)
    def _():
        sb_scr[...] = jnp.zeros(sb_scr.shape, F32)
        mb_scr[...] = jnp.zeros(mb_scr.shape, F32)
        accb_scr[...] = jnp.ones(accb_scr.shape, F32)

    @pl.when((t == 0) | ((t - 1) % nq == 0))
    def _():
        vt_scr[0:VDIM_A, :] = v_ref[0].T
        vt_scr[VDIM_A:VDIM_A + DA_ONES_ROWS, :] = jnp.ones((DA_ONES_ROWS, seq), BF16)

    def finish(acc_r):
        o = acc_r[0:VDIM_A, :] / acc_r[VDIM_A:VDIM_A + 1, :]
        lv = lam_ref[...]
        lam = (jnp.exp(jnp.sum(lv[0:1] * lv[1:2], axis=-1, keepdims=True))
               - jnp.exp(jnp.sum(lv[2:3] * lv[3:4], axis=-1, keepdims=True)) + lam_init)
        out = o[:, :tq] - lam * o[:, tq:]
        ms = jnp.mean(out * out, axis=0, keepdims=True)
        y = out * lax.rsqrt(ms + EPS) * g_ref[...] * (1.0 - lam_init)
        o_ref[0] = y.T.astype(BF16)

    def step(s_w, m_w, acc_w, s_r, m_r, acc_r):
        q = q_ref[0]
        lane = lax.broadcasted_iota(jnp.int32, q.shape, 1)
        zero = jnp.zeros_like(q)
        qs = jnp.concatenate([jnp.where(lane < HEAD_DIM, q, zero),
                              jnp.where(lane < HEAD_DIM, zero, q)], axis=0)
        m_prev = m_r[0:1]
        m_new = None
        acc = None
        for c in range(0, seq, DA_TK):
            st = lax.dot_general(k_ref[0, c:c + DA_TK, :], qs, NT_DIMS, preferred_element_type=F32)
            s_w[c:c + DA_TK, :] = st
            mc = jnp.max(st, axis=0, keepdims=True)
            m_new = mc if m_new is None else jnp.maximum(m_new, mc)
            pt = jnp.exp2((s_r[c:c + DA_TK, :] - m_prev).astype(BF16))
            part = jnp.dot(vt_scr[:, c:c + DA_TK], pt, preferred_element_type=F32)
            acc = part if acc is None else acc + part
            if c == DA_FINISH_AFTER:
                finish(acc_r)
        m_w[...] = jnp.broadcast_to(m_new, m_w.shape)
        acc_w[...] = acc

    @pl.when(t % 2 == 0)
    def _():
        step(sa_scr, ma_scr, acca_scr, sb_scr, mb_scr, accb_scr)

    @pl.when(t % 2 == 1)
    def _():
        step(sb_scr, mb_scr, accb_scr, sa_scr, ma_scr, acca_scr)


def _diff_attention(qa, ka, va, lam_p, subln_g, lam_init):
    b, s, _ = qa.shape
    nq = s // DA_TQ
    n = b * HEADS_A * nq

    def blk(t):
        t = jnp.clip(t, 0, n - 1)
        return t // (HEADS_A * nq), (t // nq) % HEADS_A, t % nq

    def q_map(t):
        bi, h, i = blk(t)
        return bi, i, h

    def k_map(t):
        bi, h, _ = blk(t)
        return bi, 0, h

    def v_map(t):
        bi, h, _ = blk(t - 1)
        return bi, 0, h

    def o_map(t):
        bi, h, i = blk(t - 2)
        return bi, i, h

    score_buf = pltpu.VMEM((s, 2 * DA_TQ), F32)
    max_buf = pltpu.VMEM((8, 2 * DA_TQ), F32)
    acc_buf = pltpu.VMEM((VDIM_A + DA_ONES_ROWS, 2 * DA_TQ), F32)
    return pl.pallas_call(
        partial(_diff_attn_kernel, nq=nq, lam_init=lam_init),
        grid=(n + 2,),
        in_specs=[pl.BlockSpec(lam_p.shape, lambda t: (0, 0)),
                  pl.BlockSpec((VDIM_A, 1), lambda t: (0, 0)),
                  pl.BlockSpec((1, DA_TQ, VDIM_A), q_map),
                  pl.BlockSpec((1, s, VDIM_A), k_map),
                  pl.BlockSpec((1, s, VDIM_A), v_map)],
        out_specs=pl.BlockSpec((1, DA_TQ, VDIM_A), o_map),
        out_shape=jax.ShapeDtypeStruct((b, s, WIDTH_A), BF16),
        scratch_shapes=[score_buf, score_buf, max_buf, max_buf, acc_buf, acc_buf,
                        pltpu.VMEM((VDIM_A + DA_ONES_ROWS, s), BF16)],
        compiler_params=_params(("arbitrary",), 52),
        name="diff_attn",
    )(lam_p, subln_g, qa, ka, va)


DIL_TQ = 128
DIL_SKEW = 1
DIL_WIN = 256


def _band_scores(q, kw):
    head = lax.broadcasted_iota(jnp.int32, q.shape, 1) // HEAD_DIM
    zero = jnp.zeros_like(q)
    qs = jnp.concatenate([jnp.where(head == h, q, zero) for h in range(HEADS_PER_GROUP_B)], axis=0)
    return lax.dot_general(qs, kw, NT_DIMS, preferred_element_type=F32)


def _band_softmax(s, rel, offset, side):
    s = jnp.where(jnp.abs(rel + offset) <= side, s, NEG_INF)
    m = jnp.max(s, axis=-1, keepdims=True)
    p = jnp.exp(s - m)
    l = jnp.sum(p, axis=-1, keepdims=True)
    return p.astype(BF16), m, l


def _band_output(p, m, l, vw, nq):
    o = jnp.dot(p, vw, preferred_element_type=F32) / l
    lse = jnp.broadcast_to(m + jnp.log(l), o.shape)
    head = lax.broadcasted_iota(jnp.int32, (nq, WIDTH_B), 1) // HEAD_DIM
    o_sel = jnp.zeros((nq, WIDTH_B), F32)
    lse_sel = jnp.zeros((nq, WIDTH_B), F32)
    for h in range(HEADS_PER_GROUP_B):
        rs = slice(h * nq, (h + 1) * nq)
        o_sel = jnp.where(head == h, o[rs], o_sel)
        lse_sel = jnp.where(head == h, lse[rs], lse_sel)
    return o_sel, lse_sel


def _dilated_kernel(*refs):
    q_refs, k_refs, v_refs = refs[0:3], refs[3:6], refs[6:9]
    ob_ref, o_scr, lse_scr = refs[9:12]
    tile = ob_ref.shape[0]
    i = pl.program_id(1)

    blocks = []
    for g, (window, dil) in enumerate(DILATED_GROUPS):
        side = window // (2 * dil)
        length = k_refs[g].shape[2]
        per_res = tile // dil
        nq = min(per_res, DIL_TQ)
        rows = lax.broadcasted_iota(jnp.int32, (HEADS_PER_GROUP_B * nq, DIL_WIN), 0)
        cols = lax.broadcasted_iota(jnp.int32, (HEADS_PER_GROUP_B * nq, DIL_WIN), 1)
        rel = (rows & (nq - 1)) - cols
        for r in range(dil):
            for u in range(per_res // nq):
                t0 = i * per_res + u * nq
                if length == DIL_WIN:
                    ks = 0
                else:
                    ks = pl.multiple_of(jnp.clip(t0 - side, 0, length - DIL_WIN), side)
                blocks.append(dict(g=g, dil=dil, r=r, u=u, nq=nq, t0=t0, ks=ks, side=side, rel=rel))

    def scores(bk):
        q = q_refs[bk["g"]][0, bk["r"], bk["u"] * bk["nq"]:(bk["u"] + 1) * bk["nq"], :]
        kw = k_refs[bk["g"]][0, bk["r"], pl.ds(bk["ks"], DIL_WIN), :]
        bk["s"] = _band_scores(q, kw)

    def softmax(bk):
        bk["pml"] = _band_softmax(bk.pop("s"), bk["rel"], bk["t0"] - bk["ks"], bk["side"])

    def output(bk):
        g, dil, r, u, nq = bk["g"], bk["dil"], bk["r"], bk["u"], bk["nq"]
        vw = v_refs[g][0, r, pl.ds(bk["ks"], DIL_WIN), :]
        o, lse = _band_output(*bk.pop("pml"), vw, nq)
        rows = pl.ds(u * nq * dil + r, nq, stride=dil) if dil > 1 else pl.ds(u * nq, nq)
        for j in range(WIDTH_B // LANES):
            o_scr[g, j, rows, :] = o[:, j * LANES:(j + 1) * LANES]
            lse_scr[g, j, rows, :] = lse[:, j * LANES:(j + 1) * LANES]

    for n in range(len(blocks) + 2 * DIL_SKEW):
        if n < len(blocks):
            scores(blocks[n])
        if 0 <= n - DIL_SKEW < len(blocks):
            softmax(blocks[n - DIL_SKEW])
        if 0 <= n - 2 * DIL_SKEW < len(blocks):
            output(blocks[n - 2 * DIL_SKEW])

    for j in range(WIDTH_B // LANES):
        l0, l1, l2 = lse_scr[0, j], lse_scr[1, j], lse_scr[2, j]
        m = jnp.maximum(jnp.maximum(l0, l1), l2)
        e0, e1, e2 = jnp.exp(l0 - m), jnp.exp(l1 - m), jnp.exp(l2 - m)
        ob = (e0 * o_scr[0, j] + e1 * o_scr[1, j] + e2 * o_scr[2, j]) / (e0 + e1 + e2)
        ob_ref[:, j * LANES:(j + 1) * LANES] = ob.astype(BF16)


def _dilated_attention(qs, ks, vs):
    b = qs[0].shape[0]
    seq = qs[0].shape[1] * qs[0].shape[2]
    tile = TOKEN_TILE
    tiles = seq // tile
    q_specs = [pl.BlockSpec((1, dil, tile // dil, WIDTH_B), lambda bi, i: (bi, 0, i, 0))
               for _, dil in DILATED_GROUPS]
    kv_specs = [pl.BlockSpec((1, dil, seq // dil, WIDTH_B), lambda bi, i: (bi, 0, 0, 0),
                             pipeline_mode=pl.Buffered(1))
                for _, dil in DILATED_GROUPS]
    return pl.pallas_call(
        _dilated_kernel,
        grid=(b, tiles),
        in_specs=q_specs + kv_specs + kv_specs,
        out_specs=pl.BlockSpec((tile, WIDTH_B), lambda bi, i: (bi * tiles + i, 0)),
        out_shape=jax.ShapeDtypeStruct((b * seq, WIDTH_B), BF16),
        scratch_shapes=[pltpu.VMEM((N_GROUPS_B, WIDTH_B // LANES, tile, LANES), F32)] * 2,
        compiler_params=_params(("parallel", "arbitrary"), 48),
        name="dilated_attn",
    )(*qs, *ks, *vs)


def _post_kernel(x_ref, mod_ref, oa_ref, ob_ref, gates_ref, wa_ref, wb_ref, wo_ref, g_ref, wu_ref, wd_ref,
                 out_ref):
    br_a = jnp.dot(oa_ref[...], wa_ref[...], preferred_element_type=F32)
    br_b = jnp.dot(ob_ref[...], wb_ref[...], preferred_element_type=F32)
    g_a = gates_ref[:, 0:D_MODEL].astype(F32)
    g_b = gates_ref[:, D_MODEL:GATE_COLS].astype(F32)
    y = jnp.dot((g_a * br_a + g_b * br_b).astype(BF16), wo_ref[...], preferred_element_type=F32)
    mod = lambda k: mod_ref[0, :, k * D_MODEL:(k + 1) * D_MODEL]
    x = x_ref[...] + mod(2) * y

    ms = jnp.mean(x * x, axis=-1, keepdims=True)
    h = (x * lax.rsqrt(ms + EPS) * g_ref[...] * (1.0 + mod(4)) + mod(3)).astype(BF16)
    u = jnp.maximum(jnp.dot(h, wu_ref[...], preferred_element_type=F32), 0.0)
    dn = jnp.dot((u * u).astype(BF16), wd_ref[...], preferred_element_type=F32)
    out_ref[...] = x + mod(5) * dn


def _post_attention(x2, mod3, oa, ob, gates, wa, wb, wo, g, wu, wd, seq):
    n, d = x2.shape
    tm = TOKEN_TILE
    row = lambda i: (i, 0)
    fixed = lambda i: (0, 0)
    return pl.pallas_call(
        _post_kernel,
        grid=(n // tm,),
        in_specs=[pl.BlockSpec((tm, d), row),
                  pl.BlockSpec((1, 1, mod3.shape[2]), lambda i: ((i * tm) // seq, 0, 0)),
                  pl.BlockSpec((tm, WIDTH_A), row),
                  pl.BlockSpec((tm, WIDTH_B), row),
                  pl.BlockSpec((tm, GATE_COLS), row),
                  _resident(wa.shape, fixed), _resident(wb.shape, fixed), _resident(wo.shape, fixed),
                  _resident((1, d), fixed), _resident(wu.shape, fixed), _resident(wd.shape, fixed)],
        out_specs=pl.BlockSpec((tm, d), row),
        out_shape=jax.ShapeDtypeStruct((n, d), F32),
        compiler_params=_params(("parallel",), 60),
        name="post_attn",
    )(x2, mod3, oa, ob, gates, wa, wb, wo, g, wu, wd)


def kernel(x, c, positions, ada_w, ada_b, norm_mix_g, norm_mlp_g, w_in, qk_gain_a, lambda_a, subln_g_a,
           qk_gain_b, w_branch_a, w_branch_b, gate_bias, w_out, w_mlp_up, w_mlp_down):
    b, s, d = x.shape
    depth = ada_w.shape[0]
    tabs = _rope_tables(positions)
    mod = _ada_mod(c, ada_w, ada_b)
    rep = LANES // HEAD_DIM
    x2 = x.reshape(b * s, d)
    for l in range(depth):
        mod3 = mod[l].reshape(b, 1, 6 * d)
        qa, ka, va, qbs, kbs, vbs, gates = _in_projection(
            x2, mod3, norm_mix_g[l].reshape(1, d), w_in[l].astype(BF16), tabs,
            jnp.tile(qk_gain_a[l], (1, rep)), jnp.tile(qk_gain_b[l], (1, rep)),
            gate_bias[l].reshape(1, GATE_COLS), s)

        lam_init = 0.8 - 0.6 * math.exp(-0.3 * l)
        oa = _diff_attention(qa.reshape(b, s, WIDTH_A), ka.reshape(b, s, WIDTH_A),
                             va.reshape(b, s, WIDTH_A), lambda_a[l], subln_g_a[l].reshape(VDIM_A, 1),
                             lam_init)
        ob = _dilated_attention(qbs, kbs, vbs)

        x2 = _post_attention(x2, mod3, oa.reshape(b * s, WIDTH_A), ob, gates,
                             w_branch_a[l].astype(BF16), w_branch_b[l].astype(BF16), w_out[l].astype(BF16),
                             norm_mlp_g[l].reshape(1, d), w_mlp_up[l].astype(BF16),
                             w_mlp_down[l].astype(BF16), s)
    return x2.reshape(b, s, d)
```

```python
import math
from functools import partial

import jax
import jax.numpy as jnp
from jax import lax
from jax.experimental import pallas as pl
from jax.experimental.pallas import tpu as pltpu

D_MODEL = 1024
HEAD_DIM = 64
HEADS_A = 8
VDIM_A = 2 * HEAD_DIM
WIDTH_A = HEADS_A * VDIM_A
DILATED_GROUPS = ((128, 1), (512, 4), (2048, 16))
N_GROUPS_B = 3
HEADS_PER_GROUP_B = 4
WIDTH_B = HEADS_PER_GROUP_B * HEAD_DIM
QKV_B_COLS = N_GROUPS_B * WIDTH_B
D_FF = 4 * D_MODEL
ROPE_THETA = 500000.0
ROT_HALF = HEAD_DIM // 4 // 2
EPS = 1e-6
NEG_INF = -1e30
GATE_COLS = 2 * D_MODEL
IN_COLS = 3 * WIDTH_A + 3 * QKV_B_COLS + GATE_COLS
SM_SCALE = HEAD_DIM ** -0.5
LOG2E = math.log2(math.e)

LANES = 128
PROJ_CHUNK = 256
TOKEN_TILE = 512
VMEM_BYTES = 64 * 1024 * 1024
MIB = 1024 * 1024

BF16 = jnp.bfloat16
F32 = jnp.float32
NT_DIMS = (((1,), (1,)), ((), ()))


def _params(semantics, vmem_mib):
    return pltpu.CompilerParams(dimension_semantics=semantics,
                                vmem_limit_bytes=min(vmem_mib * MIB, VMEM_BYTES - 4 * MIB))


def _resident(shape, index_map):
    return pl.BlockSpec(shape, index_map, pipeline_mode=pl.Buffered(1))


def _rope_kernel(pos_ref, cos_ref, sin_ref):
    pos = pos_ref[...].astype(F32)
    for f in range(ROT_HALF):
        ang = pos * (ROPE_THETA ** (-(2.0 * f) / (2 * ROT_HALF)))
        cos_ref[f] = jnp.cos(ang)
        sin_ref[f] = jnp.sin(ang)


def _rope_tables(positions):
    n = positions.size
    pos2 = positions.reshape(n // LANES, LANES)
    cos, sin = pl.pallas_call(
        _rope_kernel,
        out_shape=[jax.ShapeDtypeStruct((ROT_HALF, n // LANES, LANES), F32)] * 2,
    )(pos2)
    cos = cos.reshape(ROT_HALF, n).T
    sin = sin.reshape(ROT_HALF, n).T
    one = jnp.ones((n, HEAD_DIM - 2 * ROT_HALF), F32)
    zero = jnp.zeros((n, HEAD_DIM - 2 * ROT_HALF), F32)
    z8 = jnp.zeros((n, ROT_HALF), F32)
    c_tab = jnp.concatenate([cos, cos, one], axis=1)
    s1_tab = jnp.concatenate([z8, sin, zero], axis=1)
    s2_tab = jnp.concatenate([sin, z8, zero], axis=1)
    rep = LANES // HEAD_DIM
    return tuple(jnp.tile(t, (1, rep)) for t in (c_tab, s1_tab, s2_tab))


def _ada_kernel(c_ref, w_ref, b_ref, o_ref):
    c = c_ref[...]
    cond = (c * jax.nn.sigmoid(c)).astype(BF16)
    o_ref[0] = jnp.dot(cond, w_ref[0].astype(BF16), preferred_element_type=F32) + b_ref[0]


def _ada_mod(c, ada_w, ada_b):
    depth, d, n = ada_w.shape
    b = c.shape[0]
    tn = 768
    return pl.pallas_call(
        _ada_kernel,
        grid=(depth, n // tn),
        in_specs=[pl.BlockSpec((b, d), lambda l, j: (0, 0)),
                  pl.BlockSpec((1, d, tn), lambda l, j: (l, 0, j)),
                  pl.BlockSpec((1, 1, tn), lambda l, j: (l, 0, j))],
        out_specs=pl.BlockSpec((1, b, tn), lambda l, j: (l, 0, j)),
        out_shape=jax.ShapeDtypeStruct((depth, b, n), F32),
        compiler_params=_params(("parallel", "parallel"), 32),
    )(c, ada_w, ada_b.reshape(depth, 1, n))


def _head_norm_rope(t, gain, c_tab, s1_tab, s2_tab):
    lane = lax.broadcasted_iota(jnp.int32, t.shape, 1)
    lo = lane < HEAD_DIM
    sq = t * t
    s_lo = jnp.sum(jnp.where(lo, sq, 0.0), axis=-1, keepdims=True)
    s_hi = jnp.sum(jnp.where(lo, 0.0, sq), axis=-1, keepdims=True)
    ms = jnp.where(lo, s_lo, s_hi) * (1.0 / HEAD_DIM)
    y = t * lax.rsqrt(ms + EPS) * gain
    return (y * c_tab + pltpu.roll(y, ROT_HALF, 1) * s1_tab
            - pltpu.roll(y, LANES - ROT_HALF, 1) * s2_tab)


def _inproj_kernel(x_ref, mod_ref, g_ref, w_ref, c_ref, s1_ref, s2_ref, gain_a_ref, gain_b_ref,
                   gbias_ref, qa_ref, ka_ref, va_ref, *rest):
    qb_refs, kb_refs, vb_refs = rest[0:3], rest[3:6], rest[6:9]
    gates_ref, h_scr, y_scr = rest[9:12]
    tm = x_ref.shape[0]
    x = x_ref[...]
    ms = jnp.mean(x * x, axis=-1, keepdims=True)
    shift = mod_ref[0, :, 0:D_MODEL]
    scale = mod_ref[0, :, D_MODEL:2 * D_MODEL]
    h = x * lax.rsqrt(ms + EPS) * g_ref[...] * (1.0 + scale) + shift
    h_scr[...] = h.astype(BF16)

    c_tab, s1_tab, s2_tab = c_ref[...], s1_ref[...], s2_ref[...]
    cw = PROJ_CHUNK

    def proj(col):
        return jnp.dot(h_scr[...], w_ref[:, col:col + cw], preferred_element_type=F32)

    def normed(out_ref, col0, width, gain):
        for c in range(0, width, cw):
            acc = proj(col0 + c)
            for j in range(0, cw, LANES):
                out_ref[:, c + j:c + j + LANES] = _head_norm_rope(
                    acc[:, j:j + LANES], gain, c_tab, s1_tab, s2_tab).astype(BF16)

    def plain(out_ref, col0, width):
        for c in range(0, width, cw):
            out_ref[:, c:c + cw] = proj(col0 + c).astype(BF16)

    def split_residues(out_ref, dilation):
        for r in range(dilation):
            for j in range(WIDTH_B // LANES):
                out_ref[0, r, :, j * LANES:(j + 1) * LANES] = (
                    y_scr[j, pl.ds(r, tm // dilation, stride=dilation), :].astype(BF16))

    def normed_groups(out_refs, col0, gain):
        for g, (_, dilation) in enumerate(DILATED_GROUPS):
            acc = proj(col0 + g * WIDTH_B)
            for j in range(0, WIDTH_B, LANES):
                y_scr[j // LANES] = _head_norm_rope(acc[:, j:j + LANES], gain, c_tab, s1_tab, s2_tab)
            split_residues(out_refs[g], dilation)

    def plain_groups(out_refs, col0):
        for g, (_, dilation) in enumerate(DILATED_GROUPS):
            acc = proj(col0 + g * WIDTH_B)
            for j in range(0, WIDTH_B, LANES):
                y_scr[j // LANES] = acc[:, j:j + LANES]
            split_residues(out_refs[g], dilation)

    col = 0
    normed(qa_ref, col, WIDTH_A, gain_a_ref[0:1, :] * (SM_SCALE * LOG2E)); col += WIDTH_A
    normed(ka_ref, col, WIDTH_A, gain_a_ref[1:2, :]); col += WIDTH_A
    plain(va_ref, col, WIDTH_A); col += WIDTH_A
    normed_groups(qb_refs, col, gain_b_ref[0:1, :] * SM_SCALE); col += QKV_B_COLS
    normed_groups(kb_refs, col, gain_b_ref[1:2, :]); col += QKV_B_COLS
    plain_groups(vb_refs, col); col += QKV_B_COLS
    for c in range(0, GATE_COLS, cw):
        acc = proj(col + c) + gbias_ref[:, c:c + cw]
        gates_ref[:, c:c + cw] = jax.nn.sigmoid(acc).astype(BF16)


def _in_projection(x2, mod3, g, w_bf, tabs, gain_a, gain_b, gate_bias, seq):
    n, d = x2.shape
    tm = TOKEN_TILE
    tiles = seq // tm
    row = lambda i: (i, 0)
    fixed = lambda i: (0, 0)
    res_block = lambda i: (i // tiles, 0, i % tiles, 0)
    wide = [pl.BlockSpec((tm, WIDTH_A), row)] * 3
    wide_shape = [jax.ShapeDtypeStruct((n, WIDTH_A), BF16)] * 3
    grp = [pl.BlockSpec((1, dil, tm // dil, WIDTH_B), res_block) for _, dil in DILATED_GROUPS] * 3
    grp_shape = [jax.ShapeDtypeStruct((n // seq, dil, seq // dil, WIDTH_B), BF16) for _, dil in DILATED_GROUPS] * 3
    outs = pl.pallas_call(
        _inproj_kernel,
        grid=(n // tm,),
        in_specs=[pl.BlockSpec((tm, d), row),
                  pl.BlockSpec((1, 1, mod3.shape[2]), lambda i: (i // tiles, 0, 0)),
                  _resident((1, d), fixed),
                  _resident((d, IN_COLS), fixed),
                  pl.BlockSpec((tm, LANES), row),
                  pl.BlockSpec((tm, LANES), row),
                  pl.BlockSpec((tm, LANES), row),
                  _resident((2, LANES), fixed),
                  _resident((2, LANES), fixed),
                  _resident((1, GATE_COLS), fixed)],
        out_specs=wide + grp + [pl.BlockSpec((tm, GATE_COLS), row)],
        out_shape=wide_shape + grp_shape + [jax.ShapeDtypeStruct((n, GATE_COLS), BF16)],
        scratch_shapes=[pltpu.VMEM((tm, d), BF16), pltpu.VMEM((WIDTH_B // LANES, tm, LANES), F32)],
        compiler_params=_params(("parallel",), 56),
        name="in_proj",
    )(x2, mod3, g, w_bf, *tabs, gain_a, gain_b, gate_bias)
    return outs[0], outs[1], outs[2], outs[3:6], outs[6:9], outs[9:12], outs[12]


DA_TQ = 512
DA_TK = 256
DA_FINISH_AFTER = 1024
DA_ONES_ROWS = 16


def _diff_attn_kernel(lam_ref, g_ref, q_ref, k_ref, v_ref, o_ref, sa_scr, sb_scr, ma_scr, mb_scr,
                      acca_scr, accb_scr, vt_scr, *, nq, lam_init):
    seq = k_ref.shape[1]
    tq = q_ref.shape[1]
    t = pl.program_id(0)

    @pl.when(t == 0)
    def _():
        sb_scr[...] = jnp.zeros(sb_scr.shape, F32)
        mb_scr[...] = jnp.zeros(mb_scr.shape, F32)
        accb_scr[...] = jnp.ones(accb_scr.shape, F32)

    @pl.when((t == 0) | ((t - 1) % nq == 0))
    def _():
        vt_scr[0:VDIM_A, :] = v_ref[0].T
        vt_scr[VDIM_A:VDIM_A + DA_ONES_ROWS, :] = jnp.ones((DA_ONES_ROWS, seq), BF16)

    def finish(acc_r):
        o = acc_r[0:VDIM_A, :] / acc_r[VDIM_A:VDIM_A + 1, :]
        lv = lam_ref[...]
        lam = (jnp.exp(jnp.sum(lv[0:1] * lv[1:2], axis=-1, keepdims=True))
               - jnp.exp(jnp.sum(lv[2:3] * lv[3:4], axis=-1, keepdims=True)) + lam_init)
        out = o[:, :tq] - lam * o[:, tq:]
        ms = jnp.mean(out * out, axis=0, keepdims=True)
        y = out * lax.rsqrt(ms + EPS) * g_ref[...] * (1.0 - lam_init)
        o_ref[0] = y.T.astype(BF16)

    def step(s_w, m_w, acc_w, s_r, m_r, acc_r):
        q = q_ref[0]
        lane = lax.broadcasted_iota(jnp.int32, q.shape, 1)
        zero = jnp.zeros_like(q)
        qs = jnp.concatenate([jnp.where(lane < HEAD_DIM, q, zero),
                              jnp.where(lane < HEAD_DIM, zero, q)], axis=0)
        m_prev = m_r[0:1]
        m_new = None
        acc = None
        for c in range(0, seq, DA_TK):
            st = lax.dot_general(k_ref[0, c:c + DA_TK, :], qs, NT_DIMS, preferred_element_type=F32)
            s_w[c:c + DA_TK, :] = st
            mc = jnp.max(st, axis=0, keepdims=True)
            m_new = mc if m_new is None else jnp.maximum(m_new, mc)
            pt = jnp.exp2((s_r[c:c + DA_TK, :] - m_prev).astype(BF16))
            part = jnp.dot(vt_scr[:, c:c + DA_TK], pt, preferred_element_type=F32)
            acc = part if acc is None else acc + part
            if c == DA_FINISH_AFTER:
                finish(acc_r)
        m_w[...] = jnp.broadcast_to(m_new, m_w.shape)
        acc_w[...] = acc

    @pl.when(t % 2 == 0)
    def _():
        step(sa_scr, ma_scr, acca_scr, sb_scr, mb_scr, accb_scr)

    @pl.when(t % 2 == 1)
    def _():
        step(sb_scr, mb_scr, accb_scr, sa_scr, ma_scr, acca_scr)


def _diff_attention(qa, ka, va, lam_p, subln_g, lam_init):
    b, s, _ = qa.shape
    nq = s // DA_TQ
    n = b * HEADS_A * nq

    def blk(t):
        t = jnp.clip(t, 0, n - 1)
        return t // (HEADS_A * nq), (t // nq) % HEADS_A, t % nq

    def q_map(t):
        bi, h, i = blk(t)
        return bi, i, h

    def k_map(t):
        bi, h, _ = blk(t)
        return bi, 0, h

    def v_map(t):
        bi, h, _ = blk(t - 1)
        return bi, 0, h

    def o_map(t):
        bi, h, i = blk(t - 2)
        return bi, i, h

    score_buf = pltpu.VMEM((s, 2 * DA_TQ), F32)
    max_buf = pltpu.VMEM((8, 2 * DA_TQ), F32)
    acc_buf = pltpu.VMEM((VDIM_A + DA_ONES_ROWS, 2 * DA_TQ), F32)
    return pl.pallas_call(
        partial(_diff_attn_kernel, nq=nq, lam_init=lam_init),
        grid=(n + 2,),
        in_specs=[pl.BlockSpec(lam_p.shape, lambda t: (0, 0)),
                  pl.BlockSpec((VDIM_A, 1), lambda t: (0, 0)),
                  pl.BlockSpec((1, DA_TQ, VDIM_A), q_map),
                  pl.BlockSpec((1, s, VDIM_A), k_map),
                  pl.BlockSpec((1, s, VDIM_A), v_map)],
        out_specs=pl.BlockSpec((1, DA_TQ, VDIM_A), o_map),
        out_shape=jax.ShapeDtypeStruct((b, s, WIDTH_A), BF16),
        scratch_shapes=[score_buf, score_buf, max_buf, max_buf, acc_buf, acc_buf,
                        pltpu.VMEM((VDIM_A + DA_ONES_ROWS, s), BF16)],
        compiler_params=_params(("arbitrary",), 52),
        name="diff_attn",
    )(lam_p, subln_g, qa, ka, va)


DIL_TQ = 128
DIL_SKEW = 1
DIL_WIN = 256


def _band_scores(q, kw):
    head = lax.broadcasted_iota(jnp.int32, q.shape, 1) // HEAD_DIM
    zero = jnp.zeros_like(q)
    qs = jnp.concatenate([jnp.where(head == h, q, zero) for h in range(HEADS_PER_GROUP_B)], axis=0)
    return lax.dot_general(qs, kw, NT_DIMS, preferred_element_type=F32)


def _band_softmax(s, rel, offset, side):
    s = jnp.where(jnp.abs(rel + offset) <= side, s, NEG_INF)
    m = jnp.max(s, axis=-1, keepdims=True)
    p = jnp.exp(s - m)
    l = jnp.sum(p, axis=-1, keepdims=True)
    return p.astype(BF16), m, l


def _band_output(p, m, l, vw, nq):
    o = jnp.dot(p, vw, preferred_element_type=F32) / l
    lse = jnp.broadcast_to(m + jnp.log(l), o.shape)
    head = lax.broadcasted_iota(jnp.int32, (nq, WIDTH_B), 1) // HEAD_DIM
    o_sel = jnp.zeros((nq, WIDTH_B), F32)
    lse_sel = jnp.zeros((nq, WIDTH_B), F32)
    for h in range(HEADS_PER_GROUP_B):
        rs = slice(h * nq, (h + 1) * nq)
        o_sel = jnp.where(head == h, o[rs], o_sel)
        lse_sel = jnp.where(head == h, lse[rs], lse_sel)
    return o_sel, lse_sel


def _dilated_kernel(*refs):
    q_refs, k_refs, v_refs = refs[0:3], refs[3:6], refs[6:9]
    ob_ref, o_scr, lse_scr = refs[9:12]
    tile = ob_ref.shape[0]
    i = pl.program_id(1)

    blocks = []
    for g, (window, dil) in enumerate(DILATED_GROUPS):
        side = window // (2 * dil)
        length = k_refs[g].shape[2]
        per_res = tile // dil
        nq = min(per_res, DIL_TQ)
        rows = lax.broadcasted_iota(jnp.int32, (HEADS_PER_GROUP_B * nq, DIL_WIN), 0)
        cols = lax.broadcasted_iota(jnp.int32, (HEADS_PER_GROUP_B * nq, DIL_WIN), 1)
        rel = (rows & (nq - 1)) - cols
        for r in range(dil):
            for u in range(per_res // nq):
                t0 = i * per_res + u * nq
                if length == DIL_WIN:
                    ks = 0
                else:
                    ks = pl.multiple_of(jnp.clip(t0 - side, 0, length - DIL_WIN), side)
                blocks.append(dict(g=g, dil=dil, r=r, u=u, nq=nq, t0=t0, ks=ks, side=side, rel=rel))

    def scores(bk):
        q = q_refs[bk["g"]][0, bk["r"], bk["u"] * bk["nq"]:(bk["u"] + 1) * bk["nq"], :]
        kw = k_refs[bk["g"]][0, bk["r"], pl.ds(bk["ks"], DIL_WIN), :]
        bk["s"] = _band_scores(q, kw)

    def softmax(bk):
        bk["pml"] = _band_softmax(bk.pop("s"), bk["rel"], bk["t0"] - bk["ks"], bk["side"])

    def output(bk):
        g, dil, r, u, nq = bk["g"], bk["dil"], bk["r"], bk["u"], bk["nq"]
        vw = v_refs[g][0, r, pl.ds(bk["ks"], DIL_WIN), :]
        o, lse = _band_output(*bk.pop("pml"), vw, nq)
        rows = pl.ds(u * nq * dil + r, nq, stride=dil) if dil > 1 else pl.ds(u * nq, nq)
        for j in range(WIDTH_B // LANES):
            o_scr[g, j, rows, :] = o[:, j * LANES:(j + 1) * LANES]
            lse_scr[g, j, rows, :] = lse[:, j * LANES:(j + 1) * LANES]

    for n in range(len(blocks) + 2 * DIL_SKEW):
        if n < len(blocks):
            scores(blocks[n])
        if 0 <= n - DIL_SKEW < len(blocks):
            softmax(blocks[n - DIL_SKEW])
        if 0 <= n - 2 * DIL_SKEW < len(blocks):
            output(blocks[n - 2 * DIL_SKEW])

    for j in range(WIDTH_B // LANES):
        l0, l1, l2 = lse_scr[0, j], lse_scr[1, j], lse_scr[2, j]
        m = jnp.maximum(jnp.maximum(l0, l1), l2)
        e0, e1, e2 = jnp.exp(l0 - m), jnp.exp(l1 - m), jnp.exp(l2 - m)
        ob = (e0 * o_scr[0, j] + e1 * o_scr[1, j] + e2 * o_scr[2, j]) / (e0 + e1 + e2)
        ob_ref[:, j * LANES:(j + 1) * LANES] = ob.astype(BF16)


def _dilated_attention(qs, ks, vs):
    b = qs[0].shape[0]
    seq = qs[0].shape[1] * qs[0].shape[2]
    tile = TOKEN_TILE
    tiles = seq // tile
    q_specs = [pl.BlockSpec((1, dil, tile // dil, WIDTH_B), lambda bi, i: (bi, 0, i, 0))
               for _, dil in DILATED_GROUPS]
    kv_specs = [pl.BlockSpec((1, dil, seq // dil, WIDTH_B), lambda bi, i: (bi, 0, 0, 0),
                             pipeline_mode=pl.Buffered(1))
                for _, dil in DILATED_GROUPS]
    return pl.pallas_call(
        _dilated_kernel,
        grid=(b, tiles),
        in_specs=q_specs + kv_specs + kv_specs,
        out_specs=pl.BlockSpec((tile, WIDTH_B), lambda bi, i: (bi * tiles + i, 0)),
        out_shape=jax.ShapeDtypeStruct((b * seq, WIDTH_B), BF16),
        scratch_shapes=[pltpu.VMEM((N_GROUPS_B, WIDTH_B // LANES, tile, LANES), F32)] * 2,
        compiler_params=_params(("parallel", "arbitrary"), 48),
        name="dilated_attn",
    )(*qs, *ks, *vs)


def _post_kernel(x_ref, mod_ref, oa_ref, ob_ref, gates_ref, wa_ref, wb_ref, wo_ref, g_ref, wu_ref, wd_ref,
                 out_ref):
    br_a = jnp.dot(oa_ref[...], wa_ref[...], preferred_element_type=F32)
    br_b = jnp.dot(ob_ref[...], wb_ref[...], preferred_element_type=F32)
    g_a = gates_ref[:, 0:D_MODEL].astype(F32)
    g_b = gates_ref[:, D_MODEL:GATE_COLS].astype(F32)
    y = jnp.dot((g_a * br_a + g_b * br_b).astype(BF16), wo_ref[...], preferred_element_type=F32)
    mod = lambda k: mod_ref[0, :, k * D_MODEL:(k + 1) * D_MODEL]
    x = x_ref[...] + mod(2) * y

    ms = jnp.mean(x * x, axis=-1, keepdims=True)
    h = (x * lax.rsqrt(ms + EPS) * g_ref[...] * (1.0 + mod(4)) + mod(3)).astype(BF16)
    u = jnp.maximum(jnp.dot(h, wu_ref[...], preferred_element_type=F32), 0.0)
    dn = jnp.dot((u * u).astype(BF16), wd_ref[...], preferred_element_type=F32)
    out_ref[...] = x + mod(5) * dn


def _post_attention(x2, mod3, oa, ob, gates, wa, wb, wo, g, wu, wd, seq):
    n, d = x2.shape
    tm = TOKEN_TILE
    row = lambda i: (i, 0)
    fixed = lambda i: (0, 0)
    return pl.pallas_call(
        _post_kernel,
        grid=(n // tm,),
        in_specs=[pl.BlockSpec((tm, d), row),
                  pl.BlockSpec((1, 1, mod3.shape[2]), lambda i: ((i * tm) // seq, 0, 0)),
                  pl.BlockSpec((tm, WIDTH_A), row),
                  pl.BlockSpec((tm, WIDTH_B), row),
                  pl.BlockSpec((tm, GATE_COLS), row),
                  _resident(wa.shape, fixed), _resident(wb.shape, fixed), _resident(wo.shape, fixed),
                  _resident((1, d), fixed), _resident(wu.shape, fixed), _resident(wd.shape, fixed)],
        out_specs=pl.BlockSpec((tm, d), row),
        out_shape=jax.ShapeDtypeStruct((n, d), F32),
        compiler_params=_params(("parallel",), 60),
        name="post_attn",
    )(x2, mod3, oa, ob, gates, wa, wb, wo, g, wu, wd)


def kernel(x, c, positions, ada_w, ada_b, norm_mix_g, norm_mlp_g, w_in, qk_gain_a, lambda_a, subln_g_a,
           qk_gain_b, w_branch_a, w_branch_b, gate_bias, w_out, w_mlp_up, w_mlp_down):
    b, s, d = x.shape
    depth = ada_w.shape[0]
    tabs = _rope_tables(positions)
    mod = _ada_mod(c, ada_w, ada_b)
    rep = LANES // HEAD_DIM
    x2 = x.reshape(b * s, d)
    for l in range(depth):
        mod3 = mod[l].reshape(b, 1, 6 * d)
        qa, ka, va, qbs, kbs, vbs, gates = _in_projection(
            x2, mod3, norm_mix_g[l].reshape(1, d), w_in[l].astype(BF16), tabs,
            jnp.tile(qk_gain_a[l], (1, rep)), jnp.tile(qk_gain_b[l], (1, rep)),
            gate_bias[l].reshape(1, GATE_COLS), s)

        lam_init = 0.8 - 0.6 * math.exp(-0.3 * l)
        oa = _diff_attention(qa.reshape(b, s, WIDTH_A), ka.reshape(b, s, WIDTH_A),
                             va.reshape(b, s, WIDTH_A), lambda_a[l], subln_g_a[l].reshape(VDIM_A, 1),
                             lam_init)
        ob = _dilated_attention(qbs, kbs, vbs)

        x2 = _post_attention(x2, mod3, oa.reshape(b * s, WIDTH_A), ob, gates,
                             w_branch_a[l].astype(BF16), w_branch_b[l].astype(BF16), w_out[l].astype(BF16),
                             norm_mlp_g[l].reshape(1, d), w_mlp_up[l].astype(BF16),
                             w_mlp_down[l].astype(BF16), s)
    return x2.reshape(b, s, d)
```

```python
import math
from functools import partial

import jax
import jax.numpy as jnp
from jax import lax
from jax.experimental import pallas as pl
from jax.experimental.pallas import tpu as pltpu

D_MODEL = 1024
HEAD_DIM = 64
HEADS_A = 8
VDIM_A = 2 * HEAD_DIM
WIDTH_A = HEADS_A * VDIM_A
DILATED_GROUPS = ((128, 1), (512, 4), (2048, 16))
N_GROUPS_B = 3
HEADS_PER_GROUP_B = 4
WIDTH_B = HEADS_PER_GROUP_B * HEAD_DIM
QKV_B_COLS = N_GROUPS_B * WIDTH_B
D_FF = 4 * D_MODEL
ROPE_THETA = 500000.0
ROT_HALF = HEAD_DIM // 4 // 2
EPS = 1e-6
NEG_INF = -1e30
GATE_COLS = 2 * D_MODEL
IN_COLS = 3 * WIDTH_A + 3 * QKV_B_COLS + GATE_COLS
SM_SCALE = HEAD_DIM ** -0.5
LOG2E = math.log2(math.e)

LANES = 128
PROJ_CHUNK = 256
TOKEN_TILE = 512
VMEM_BYTES = 64 * 1024 * 1024
MIB = 1024 * 1024

BF16 = jnp.bfloat16
F32 = jnp.float32
NT_DIMS = (((1,), (1,)), ((), ()))


def _params(semantics, vmem_mib):
    return pltpu.CompilerParams(dimension_semantics=semantics,
                                vmem_limit_bytes=min(vmem_mib * MIB, VMEM_BYTES - 4 * MIB))


def _resident(shape, index_map):
    return pl.BlockSpec(shape, index_map, pipeline_mode=pl.Buffered(1))


def _rope_kernel(pos_ref, cos_ref, sin_ref):
    pos = pos_ref[...].astype(F32)
    for f in range(ROT_HALF):
        ang = pos * (ROPE_THETA ** (-(2.0 * f) / (2 * ROT_HALF)))
        cos_ref[f] = jnp.cos(ang)
        sin_ref[f] = jnp.sin(ang)


def _rope_tables(positions):
    n = positions.size
    pos2 = positions.reshape(n // LANES, LANES)
    cos, sin = pl.pallas_call(
        _rope_kernel,
        out_shape=[jax.ShapeDtypeStruct((ROT_HALF, n // LANES, LANES), F32)] * 2,
    )(pos2)
    cos = cos.reshape(ROT_HALF, n).T
    sin = sin.reshape(ROT_HALF, n).T
    one = jnp.ones((n, HEAD_DIM - 2 * ROT_HALF), F32)
    zero = jnp.zeros((n, HEAD_DIM - 2 * ROT_HALF), F32)
    z8 = jnp.zeros((n, ROT_HALF), F32)
    c_tab = jnp.concatenate([cos, cos, one], axis=1)
    s1_tab = jnp.concatenate([z8, sin, zero], axis=1)
    s2_tab = jnp.concatenate([sin, z8, zero], axis=1)
    rep = LANES // HEAD_DIM
    return tuple(jnp.tile(t, (1, rep)) for t in (c_tab, s1_tab, s2_tab))


def _ada_kernel(c_ref, w_ref, b_ref, o_ref):
    c = c_ref[...]
    cond = (c * jax.nn.sigmoid(c)).astype(BF16)
    o_ref[0] = jnp.dot(cond, w_ref[0].astype(BF16), preferred_element_type=F32) + b_ref[0]


def _ada_mod(c, ada_w, ada_b):
    depth, d, n = ada_w.shape
    b = c.shape[0]
    tn = 768
    return pl.pallas_call(
        _ada_kernel,
        grid=(depth, n // tn),
        in_specs=[pl.BlockSpec((b, d), lambda l, j: (0, 0)),
                  pl.BlockSpec((1, d, tn), lambda l, j: (l, 0, j)),
                  pl.BlockSpec((1, 1, tn), lambda l, j: (l, 0, j))],
        out_specs=pl.BlockSpec((1, b, tn), lambda l, j: (l, 0, j)),
        out_shape=jax.ShapeDtypeStruct((depth, b, n), F32),
        compiler_params=_params(("parallel", "parallel"), 32),
    )(c, ada_w, ada_b.reshape(depth, 1, n))


def _head_norm_rope(t, gain, c_tab, s1_tab, s2_tab):
    lane = lax.broadcasted_iota(jnp.int32, t.shape, 1)
    lo = lane < HEAD_DIM
    sq = t * t
    s_lo = jnp.sum(jnp.where(lo, sq, 0.0), axis=-1, keepdims=True)
    s_hi = jnp.sum(jnp.where(lo, 0.0, sq), axis=-1, keepdims=True)
    ms = jnp.where(lo, s_lo, s_hi) * (1.0 / HEAD_DIM)
    y = t * lax.rsqrt(ms + EPS) * gain
    return (y * c_tab + pltpu.roll(y, ROT_HALF, 1) * s1_tab
            - pltpu.roll(y, LANES - ROT_HALF, 1) * s2_tab)


def _inproj_kernel(x_ref, mod_ref, g_ref, w_ref, c_ref, s1_ref, s2_ref, gain_a_ref, gain_b_ref,
                   gbias_ref, qa_ref, ka_ref, va_ref, *rest):
    qb_refs, kb_refs, vb_refs = rest[0:3], rest[3:6], rest[6:9]
    gates_ref, h_scr, y_scr = rest[9:12]
    tm = x_ref.shape[0]
    x = x_ref[...]
    ms = jnp.mean(x * x, axis=-1, keepdims=True)
    shift = mod_ref[0, :, 0:D_MODEL]
    scale = mod_ref[0, :, D_MODEL:2 * D_MODEL]
    h = x * lax.rsqrt(ms + EPS) * g_ref[...] * (1.0 + scale) + shift
    h_scr[...] = h.astype(BF16)

    c_tab, s1_tab, s2_tab = c_ref[...], s1_ref[...], s2_ref[...]
    cw = PROJ_CHUNK

    def proj(col):
        return jnp.dot(h_scr[...], w_ref[:, col:col + cw], preferred_element_type=F32)

    def normed(out_ref, col0, width, gain):
        for c in range(0, width, cw):
            acc = proj(col0 + c)
            for j in range(0, cw, LANES):
                out_ref[:, c + j:c + j + LANES] = _head_norm_rope(
                    acc[:, j:j + LANES], gain, c_tab, s1_tab, s2_tab).astype(BF16)

    def plain(out_ref, col0, width):
        for c in range(0, width, cw):
            out_ref[:, c:c + cw] = proj(col0 + c).astype(BF16)

    def split_residues(out_ref, dilation):
        for r in range(dilation):
            for j in range(WIDTH_B // LANES):
                out_ref[0, r, :, j * LANES:(j + 1) * LANES] = (
                    y_scr[j, pl.ds(r, tm // dilation, stride=dilation), :].astype(BF16))

    def normed_groups(out_refs, col0, gain):
        for g, (_, dilation) in enumerate(DILATED_GROUPS):
            acc = proj(col0 + g * WIDTH_B)
            for j in range(0, WIDTH_B, LANES):
                y_scr[j // LANES] = _head_norm_rope(acc[:, j:j + LANES], gain, c_tab, s1_tab, s2_tab)
            split_residues(out_refs[g], dilation)

    def plain_groups(out_refs, col0):
        for g, (_, dilation) in enumerate(DILATED_GROUPS):
            acc = proj(col0 + g * WIDTH_B)
            for j in range(0, WIDTH_B, LANES):
                y_scr[j // LANES] = acc[:, j:j + LANES]
            split_residues(out_refs[g], dilation)

    col = 0
    normed(qa_ref, col, WIDTH_A, gain_a_ref[0:1, :] * (SM_SCALE * LOG2E)); col += WIDTH_A
    normed(ka_ref, col, WIDTH_A, gain_a_ref[1:2, :]); col += WIDTH_A
    plain(va_ref, col, WIDTH_A); col += WIDTH_A
    normed_groups(qb_refs, col, gain_b_ref[0:1, :] * SM_SCALE); col += QKV_B_COLS
    normed_groups(kb_refs, col, gain_b_ref[1:2, :]); col += QKV_B_COLS
    plain_groups(vb_refs, col); col += QKV_B_COLS
    for c in range(0, GATE_COLS, cw):
        acc = proj(col + c) + gbias_ref[:, c:c + cw]
        gates_ref[:, c:c + cw] = jax.nn.sigmoid(acc).astype(BF16)


def _in_projection(x2, mod3, g, w_bf, tabs, gain_a, gain_b, gate_bias, seq):
    n, d = x2.shape
    tm = TOKEN_TILE
    tiles = seq // tm
    row = lambda i: (i, 0)
    fixed = lambda i: (0, 0)
    res_block = lambda i: (i // tiles, 0, i % tiles, 0)
    wide = [pl.BlockSpec((tm, WIDTH_A), row)] * 3
    wide_shape = [jax.ShapeDtypeStruct((n, WIDTH_A), BF16)] * 3
    grp = [pl.BlockSpec((1, dil, tm // dil, WIDTH_B), res_block) for _, dil in DILATED_GROUPS] * 3
    grp_shape = [jax.ShapeDtypeStruct((n // seq, dil, seq // dil, WIDTH_B), BF16) for _, dil in DILATED_GROUPS] * 3
    outs = pl.pallas_call(
        _inproj_kernel,
        grid=(n // tm,),
        in_specs=[pl.BlockSpec((tm, d), row),
                  pl.BlockSpec((1, 1, mod3.shape[2]), lambda i: (i // tiles, 0, 0)),
                  _resident((1, d), fixed),
                  _resident((d, IN_COLS), fixed),
                  pl.BlockSpec((tm, LANES), row),
                  pl.BlockSpec((tm, LANES), row),
                  pl.BlockSpec((tm, LANES), row),
                  _resident((2, LANES), fixed),
                  _resident((2, LANES), fixed),
                  _resident((1, GATE_COLS), fixed)],
        out_specs=wide + grp + [pl.BlockSpec((tm, GATE_COLS), row)],
        out_shape=wide_shape + grp_shape + [jax.ShapeDtypeStruct((n, GATE_COLS), BF16)],
        scratch_shapes=[pltpu.VMEM((tm, d), BF16), pltpu.VMEM((WIDTH_B // LANES, tm, LANES), F32)],
        compiler_params=_params(("parallel",), 56),
        name="in_proj",
    )(x2, mod3, g, w_bf, *tabs, gain_a, gain_b, gate_bias)
    return outs[0], outs[1], outs[2], outs[3:6], outs[6:9], outs[9:12], outs[12]


DA_TQ = 256
DA_TK = 1024
DA_FINISH_AFTER = 1024
DA_ONES_ROWS = 16


def _diff_attn_kernel(lam_ref, g_ref, q_ref, k_ref, v_ref, o_ref, sa_scr, sb_scr, ma_scr, mb_scr,
                      acca_scr, accb_scr, vt_scr, *, nq, lam_init):
    seq = k_ref.shape[1]
    tq = q_ref.shape[1]
    t = pl.program_id(0)

    @pl.when(t == 0)
    def _():
        sb_scr[...] = jnp.zeros(sb_scr.shape, F32)
        mb_scr[...] = jnp.zeros(mb_scr.shape, F32)
        accb_scr[...] = jnp.ones(accb_scr.shape, F32)

    @pl.when((t == 0) | ((t - 1) % nq == 0))
    def _():
        vt_scr[0:VDIM_A, :] = v_ref[0].T
        vt_scr[VDIM_A:VDIM_A + DA_ONES_ROWS, :] = jnp.ones((DA_ONES_ROWS, seq), BF16)

    def finish(acc_r):
        o = acc_r[0:VDIM_A, :] / acc_r[VDIM_A:VDIM_A + 1, :]
        lv = lam_ref[...]
        lam = (jnp.exp(jnp.sum(lv[0:1] * lv[1:2], axis=-1, keepdims=True))
               - jnp.exp(jnp.sum(lv[2:3] * lv[3:4], axis=-1, keepdims=True)) + lam_init)
        out = o[:, :tq] - lam * o[:, tq:]
        ms = jnp.mean(out * out, axis=0, keepdims=True)
        y = out * lax.rsqrt(ms + EPS) * g_ref[...] * (1.0 - lam_init)
        o_ref[0] = y.T.astype(BF16)

    def step(s_w, m_w, acc_w, s_r, m_r, acc_r):
        q = q_ref[0]
        lane = lax.broadcasted_iota(jnp.int32, q.shape, 1)
        zero = jnp.zeros_like(q)
        qs = jnp.concatenate([jnp.where(lane < HEAD_DIM, q, zero),
                              jnp.where(lane < HEAD_DIM, zero, q)], axis=0)
        m_prev = m_r[0:1]
        m_new = None
        acc = None
        for c in range(0, seq, DA_TK):
            st = lax.dot_general(k_ref[0, c:c + DA_TK, :], qs, NT_DIMS, preferred_element_type=F32)
            s_w[c:c + DA_TK, :] = st
            mc = jnp.max(st, axis=0, keepdims=True)
            m_new = mc if m_new is None else jnp.maximum(m_new, mc)
            pt = jnp.exp2((s_r[c:c + DA_TK, :] - m_prev).astype(BF16))
            part = jnp.dot(vt_scr[:, c:c + DA_TK], pt, preferred_element_type=F32)
            acc = part if acc is None else acc + part
            if c == DA_FINISH_AFTER:
                finish(acc_r)
        m_w[...] = jnp.broadcast_to(m_new, m_w.shape)
        acc_w[...] = acc

    @pl.when(t % 2 == 0)
    def _():
        step(sa_scr, ma_scr, acca_scr, sb_scr, mb_scr, accb_scr)

    @pl.when(t % 2 == 1)
    def _():
        step(sb_scr, mb_scr, accb_scr, sa_scr, ma_scr, acca_scr)


def _diff_attention(qa, ka, va, lam_p, subln_g, lam_init):
    b, s, _ = qa.shape
    nq = s // DA_TQ
    n = b * HEADS_A * nq

    def blk(t):
        t = jnp.clip(t, 0, n - 1)
        return t // (HEADS_A * nq), (t // nq) % HEADS_A, t % nq

    def q_map(t):
        bi, h, i = blk(t)
        return bi, i, h

    def k_map(t):
        bi, h, _ = blk(t)
        return bi, 0, h

    def v_map(t):
        bi, h, _ = blk(t - 1)
        return bi, 0, h

    def o_map(t):
        bi, h, i = blk(t - 2)
        return bi, i, h

    score_buf = pltpu.VMEM((s, 2 * DA_TQ), F32)
    max_buf = pltpu.VMEM((8, 2 * DA_TQ), F32)
    acc_buf = pltpu.VMEM((VDIM_A + DA_ONES_ROWS, 2 * DA_TQ), F32)
    return pl.pallas_call(
        partial(_diff_attn_kernel, nq=nq, lam_init=lam_init),
        grid=(n + 2,),
        in_specs=[pl.BlockSpec(lam_p.shape, lambda t: (0, 0)),
                  pl.BlockSpec((VDIM_A, 1), lambda t: (0, 0)),
                  pl.BlockSpec((1, DA_TQ, VDIM_A), q_map),
                  pl.BlockSpec((1, s, VDIM_A), k_map),
                  pl.BlockSpec((1, s, VDIM_A), v_map)],
        out_specs=pl.BlockSpec((1, DA_TQ, VDIM_A), o_map),
        out_shape=jax.ShapeDtypeStruct((b, s, WIDTH_A), BF16),
        scratch_shapes=[score_buf, score_buf, max_buf, max_buf, acc_buf, acc_buf,
                        pltpu.VMEM((VDIM_A + DA_ONES_ROWS, s), BF16)],
        compiler_params=_params(("arbitrary",), 52),
        name="diff_attn",
    )(lam_p, subln_g, qa, ka, va)


DIL_TQ = 128
DIL_SKEW = 1
DIL_WIN = 256


def _band_scores(q, kw):
    head = lax.broadcasted_iota(jnp.int32, q.shape, 1) // HEAD_DIM
    zero = jnp.zeros_like(q)
    qs = jnp.concatenate([jnp.where(head == h, q, zero) for h in range(HEADS_PER_GROUP_B)], axis=0)
    return lax.dot_general(qs, kw, NT_DIMS, preferred_element_type=F32)


def _band_softmax(s, rel, offset, side):
    s = jnp.where(jnp.abs(rel + offset) <= side, s, NEG_INF)
    m = jnp.max(s, axis=-1, keepdims=True)
    p = jnp.exp(s - m)
    l = jnp.sum(p, axis=-1, keepdims=True)
    return p.astype(BF16), m, l


def _band_output(p, m, l, vw, nq):
    o = jnp.dot(p, vw, preferred_element_type=F32) / l
    lse = jnp.broadcast_to(m + jnp.log(l), o.shape)
    head = lax.broadcasted_iota(jnp.int32, (nq, WIDTH_B), 1) // HEAD_DIM
    o_sel = jnp.zeros((nq, WIDTH_B), F32)
    lse_sel = jnp.zeros((nq, WIDTH_B), F32)
    for h in range(HEADS_PER_GROUP_B):
        rs = slice(h * nq, (h + 1) * nq)
        o_sel = jnp.where(head == h, o[rs], o_sel)
        lse_sel = jnp.where(head == h, lse[rs], lse_sel)
    return o_sel, lse_sel


def _dilated_kernel(*refs):
    q_refs, k_refs, v_refs = refs[0:3], refs[3:6], refs[6:9]
    ob_ref, o_scr, lse_scr = refs[9:12]
    tile = ob_ref.shape[0]
    i = pl.program_id(1)

    blocks = []
    for g, (window, dil) in enumerate(DILATED_GROUPS):
        side = window // (2 * dil)
        length = k_refs[g].shape[2]
        per_res = tile // dil
        nq = min(per_res, DIL_TQ)
        rows = lax.broadcasted_iota(jnp.int32, (HEADS_PER_GROUP_B * nq, DIL_WIN), 0)
        cols = lax.broadcasted_iota(jnp.int32, (HEADS_PER_GROUP_B * nq, DIL_WIN), 1)
        rel = (rows & (nq - 1)) - cols
        for r in range(dil):
            for u in range(per_res // nq):
                t0 = i * per_res + u * nq
                if length == DIL_WIN:
                    ks = 0
                else:
                    ks = pl.multiple_of(jnp.clip(t0 - side, 0, length - DIL_WIN), side)
                blocks.append(dict(g=g, dil=dil, r=r, u=u, nq=nq, t0=t0, ks=ks, side=side, rel=rel))

    def scores(bk):
        q = q_refs[bk["g"]][0, bk["r"], bk["u"] * bk["nq"]:(bk["u"] + 1) * bk["nq"], :]
        kw = k_refs[bk["g"]][0, bk["r"], pl.ds(bk["ks"], DIL_WIN), :]
        bk["s"] = _band_scores(q, kw)

    def softmax(bk):
        bk["pml"] = _band_softmax(bk.pop("s"), bk["rel"], bk["t0"] - bk["ks"], bk["side"])

    def output(bk):
        g, dil, r, u, nq = bk["g"], bk["dil"], bk["r"], bk["u"], bk["nq"]
        vw = v_refs[g][0, r, pl.ds(bk["ks"], DIL_WIN), :]
        o, lse = _band_output(*bk.pop("pml"), vw, nq)
        rows = pl.ds(u * nq * dil + r, nq, stride=dil) if dil > 1 else pl.ds(u * nq, nq)
        for j in range(WIDTH_B // LANES):
            o_scr[g, j, rows, :] = o[:, j * LANES:(j + 1) * LANES]
            lse_scr[g, j, rows, :] = lse[:, j * LANES:(j + 1) * LANES]

    for n in range(len(blocks) + 2 * DIL_SKEW):
        if n < len(blocks):
            scores(blocks[n])
        if 0 <= n - DIL_SKEW < len(blocks):
            softmax(blocks[n - DIL_SKEW])
        if 0 <= n - 2 * DIL_SKEW < len(blocks):
            output(blocks[n - 2 * DIL_SKEW])

    for j in range(WIDTH_B // LANES):
        l0, l1, l2 = lse_scr[0, j], lse_scr[1, j], lse_scr[2, j]
        m = jnp.maximum(jnp.maximum(l0, l1), l2)
        e0, e1, e2 = jnp.exp(l0 - m), jnp.exp(l1 - m), jnp.exp(l2 - m)
        ob = (e0 * o_scr[0, j] + e1 * o_scr[1, j] + e2 * o_scr[2, j]) / (e0 + e1 + e2)
        ob_ref[:, j * LANES:(j + 1) * LANES] = ob.astype(BF16)


def _dilated_attention(qs, ks, vs):
    b = qs[0].shape[0]
    seq = qs[0].shape[1] * qs[0].shape[2]
    tile = TOKEN_TILE
    tiles = seq // tile
    q_specs = [pl.BlockSpec((1, dil, tile // dil, WIDTH_B), lambda bi, i: (bi, 0, i, 0))
               for _, dil in DILATED_GROUPS]
    kv_specs = [pl.BlockSpec((1, dil, seq // dil, WIDTH_B), lambda bi, i: (bi, 0, 0, 0),
                             pipeline_mode=pl.Buffered(1))
                for _, dil in DILATED_GROUPS]
    return pl.pallas_call(
        _dilated_kernel,
        grid=(b, tiles),
        in_specs=q_specs + kv_specs + kv_specs,
        out_specs=pl.BlockSpec((tile, WIDTH_B), lambda bi, i: (bi * tiles + i, 0)),
        out_shape=jax.ShapeDtypeStruct((b * seq, WIDTH_B), BF16),
        scratch_shapes=[pltpu.VMEM((N_GROUPS_B, WIDTH_B // LANES, tile, LANES), F32)] * 2,
        compiler_params=_params(("parallel", "arbitrary"), 48),
        name="dilated_attn",
    )(*qs, *ks, *vs)


def _post_kernel(x_ref, mod_ref, oa_ref, ob_ref, gates_ref, wa_ref, wb_ref, wo_ref, g_ref, wu_ref, wd_ref,
                 out_ref):
    br_a = jnp.dot(oa_ref[...], wa_ref[...], preferred_element_type=F32)
    br_b = jnp.dot(ob_ref[...], wb_ref[...], preferred_element_type=F32)
    g_a = gates_ref[:, 0:D_MODEL].astype(F32)
    g_b = gates_ref[:, D_MODEL:GATE_COLS].astype(F32)
    y = jnp.dot((g_a * br_a + g_b * br_b).astype(BF16), wo_ref[...], preferred_element_type=F32)
    mod = lambda k: mod_ref[0, :, k * D_MODEL:(k + 1) * D_MODEL]
    x = x_ref[...] + mod(2) * y

    ms = jnp.mean(x * x, axis=-1, keepdims=True)
    h = (x * lax.rsqrt(ms + EPS) * g_ref[...] * (1.0 + mod(4)) + mod(3)).astype(BF16)
    u = jnp.maximum(jnp.dot(h, wu_ref[...], preferred_element_type=F32), 0.0)
    dn = jnp.dot((u * u).astype(BF16), wd_ref[...], preferred_element_type=F32)
    out_ref[...] = x + mod(5) * dn


def _post_attention(x2, mod3, oa, ob, gates, wa, wb, wo, g, wu, wd, seq):
    n, d = x2.shape
    tm = TOKEN_TILE
    row = lambda i: (i, 0)
    fixed = lambda i: (0, 0)
    return pl.pallas_call(
        _post_kernel,
        grid=(n // tm,),
        in_specs=[pl.BlockSpec((tm, d), row),
                  pl.BlockSpec((1, 1, mod3.shape[2]), lambda i: ((i * tm) // seq, 0, 0)),
                  pl.BlockSpec((tm, WIDTH_A), row),
                  pl.BlockSpec((tm, WIDTH_B), row),
                  pl.BlockSpec((tm, GATE_COLS), row),
                  _resident(wa.shape, fixed), _resident(wb.shape, fixed), _resident(wo.shape, fixed),
                  _resident((1, d), fixed), _resident(wu.shape, fixed), _resident(wd.shape, fixed)],
        out_specs=pl.BlockSpec((tm, d), row),
        out_shape=jax.ShapeDtypeStruct((n, d), F32),
        compiler_params=_params(("parallel",), 60),
        name="post_attn",
    )(x2, mod3, oa, ob, gates, wa, wb, wo, g, wu, wd)


def kernel(x, c, positions, ada_w, ada_b, norm_mix_g, norm_mlp_g, w_in, qk_gain_a, lambda_a, subln_g_a,
           qk_gain_b, w_branch_a, w_branch_b, gate_bias, w_out, w_mlp_up, w_mlp_down):
    b, s, d = x.shape
    depth = ada_w.shape[0]
    tabs = _rope_tables(positions)
    mod = _ada_mod(c, ada_w, ada_b)
    rep = LANES // HEAD_DIM
    x2 = x.reshape(b * s, d)
    for l in range(depth):
        mod3 = mod[l].reshape(b, 1, 6 * d)
        qa, ka, va, qbs, kbs, vbs, gates = _in_projection(
            x2, mod3, norm_mix_g[l].reshape(1, d), w_in[l].astype(BF16), tabs,
            jnp.tile(qk_gain_a[l], (1, rep)), jnp.tile(qk_gain_b[l], (1, rep)),
            gate_bias[l].reshape(1, GATE_COLS), s)

        lam_init = 0.8 - 0.6 * math.exp(-0.3 * l)
        oa = _diff_attention(qa.reshape(b, s, WIDTH_A), ka.reshape(b, s, WIDTH_A),
                             va.reshape(b, s, WIDTH_A), lambda_a[l], subln_g_a[l].reshape(VDIM_A, 1),
                             lam_init)
        ob = _dilated_attention(qbs, kbs, vbs)

        x2 = _post_attention(x2, mod3, oa.reshape(b * s, WIDTH_A), ob, gates,
                             w_branch_a[l].astype(BF16), w_branch_b[l].astype(BF16), w_out[l].astype(BF16),
                             norm_mlp_g[l].reshape(1, d), w_mlp_up[l].astype(BF16),
                             w_mlp_down[l].astype(BF16), s)
    return x2.reshape(b, s, d)
```

```python
import math
from functools import partial

import jax
import jax.numpy as jnp
from jax import lax
from jax.experimental import pallas as pl
from jax.experimental.pallas import tpu as pltpu

D_MODEL = 1024
HEAD_DIM = 64
HEADS_A = 8
VDIM_A = 2 * HEAD_DIM
WIDTH_A = HEADS_A * VDIM_A
DILATED_GROUPS = ((128, 1), (512, 4), (2048, 16))
N_GROUPS_B = 3
HEADS_PER_GROUP_B = 4
WIDTH_B = HEADS_PER_GROUP_B * HEAD_DIM
QKV_B_COLS = N_GROUPS_B * WIDTH_B
D_FF = 4 * D_MODEL
ROPE_THETA = 500000.0
ROT_HALF = HEAD_DIM // 4 // 2
EPS = 1e-6
NEG_INF = -1e30
GATE_COLS = 2 * D_MODEL
IN_COLS = 3 * WIDTH_A + 3 * QKV_B_COLS + GATE_COLS
SM_SCALE = HEAD_DIM ** -0.5
LOG2E = math.log2(math.e)

LANES = 128
PROJ_CHUNK = 256
TOKEN_TILE = 512
VMEM_BYTES = 64 * 1024 * 1024
MIB = 1024 * 1024

BF16 = jnp.bfloat16
F32 = jnp.float32
NT_DIMS = (((1,), (1,)), ((), ()))


def _params(semantics, vmem_mib):
    return pltpu.CompilerParams(dimension_semantics=semantics,
                                vmem_limit_bytes=min(vmem_mib * MIB, VMEM_BYTES - 4 * MIB))


def _resident(shape, index_map):
    return pl.BlockSpec(shape, index_map, pipeline_mode=pl.Buffered(1))


def _rope_kernel(pos_ref, cos_ref, sin_ref):
    pos = pos_ref[...].astype(F32)
    for f in range(ROT_HALF):
        ang = pos * (ROPE_THETA ** (-(2.0 * f) / (2 * ROT_HALF)))
        cos_ref[f] = jnp.cos(ang)
        sin_ref[f] = jnp.sin(ang)


def _rope_tables(positions):
    n = positions.size
    pos2 = positions.reshape(n // LANES, LANES)
    cos, sin = pl.pallas_call(
        _rope_kernel,
        out_shape=[jax.ShapeDtypeStruct((ROT_HALF, n // LANES, LANES), F32)] * 2,
    )(pos2)
    cos = cos.reshape(ROT_HALF, n).T
    sin = sin.reshape(ROT_HALF, n).T
    one = jnp.ones((n, HEAD_DIM - 2 * ROT_HALF), F32)
    zero = jnp.zeros((n, HEAD_DIM - 2 * ROT_HALF), F32)
    z8 = jnp.zeros((n, ROT_HALF), F32)
    c_tab = jnp.concatenate([cos, cos, one], axis=1)
    s1_tab = jnp.concatenate([z8, sin, zero], axis=1)
    s2_tab = jnp.concatenate([sin, z8, zero], axis=1)
    rep = LANES // HEAD_DIM
    return tuple(jnp.tile(t, (1, rep)) for t in (c_tab, s1_tab, s2_tab))


def _ada_kernel(c_ref, w_ref, b_ref, o_ref):
    c = c_ref[...]
    cond = (c * jax.nn.sigmoid(c)).astype(BF16)
    o_ref[0] = jnp.dot(cond, w_ref[0].astype(BF16), preferred_element_type=F32) + b_ref[0]


def _ada_mod(c, ada_w, ada_b):
    depth, d, n = ada_w.shape
    b = c.shape[0]
    tn = 768
    return pl.pallas_call(
        _ada_kernel,
        grid=(depth, n // tn),
        in_specs=[pl.BlockSpec((b, d), lambda l, j: (0, 0)),
                  pl.BlockSpec((1, d, tn), lambda l, j: (l, 0, j)),
                  pl.BlockSpec((1, 1, tn), lambda l, j: (l, 0, j))],
        out_specs=pl.BlockSpec((1, b, tn), lambda l, j: (l, 0, j)),
        out_shape=jax.ShapeDtypeStruct((depth, b, n), F32),
        compiler_params=_params(("parallel", "parallel"), 32),
    )(c, ada_w, ada_b.reshape(depth, 1, n))


def _head_norm_rope(acc, ssum, gain, c_tab, s1_tab, s2_tab):
    y = acc * lax.rsqrt(ssum * (1.0 / HEAD_DIM) + EPS)
    halves = []
    for j in range(0, acc.shape[1], LANES):
        yj = y[:, j:j + LANES] * gain
        halves.append(yj * c_tab + pltpu.roll(yj, ROT_HALF, 1) * s1_tab
                      - pltpu.roll(yj, LANES - ROT_HALF, 1) * s2_tab)
    return halves


def _inproj_kernel(x_ref, mod_ref, g_ref, w_ref, c_ref, s1_ref, s2_ref, gain_a_ref, gain_b_ref,
                   gbias_ref, ones_ref, qa_ref, ka_ref, va_ref, *rest):
    qb_refs, kb_refs, vb_refs = rest[0:3], rest[3:6], rest[6:9]
    gates_ref, h_scr, y_scr = rest[9:12]
    tm = x_ref.shape[0]
    x = x_ref[...]
    ms = jnp.mean(x * x, axis=-1, keepdims=True)
    shift = mod_ref[0, :, 0:D_MODEL]
    scale = mod_ref[0, :, D_MODEL:2 * D_MODEL]
    h = x * lax.rsqrt(ms + EPS) * g_ref[...] * (1.0 + scale) + shift
    h_scr[...] = h.astype(BF16)

    c_tab, s1_tab, s2_tab = c_ref[...], s1_ref[...], s2_ref[...]
    cw = PROJ_CHUNK

    def split_residues(out_ref, dilation):
        for r in range(dilation):
            for j in range(WIDTH_B // LANES):
                out_ref[0, r, :, j * LANES:(j + 1) * LANES] = (
                    y_scr[j, pl.ds(r, tm // dilation, stride=dilation), :].astype(BF16))

    tasks = []

    def normed(out_ref, width, gain):
        def epilogue(acc, ssum, c):
            for j, half in enumerate(_head_norm_rope(acc, ssum, gain, c_tab, s1_tab, s2_tab)):
                out_ref[:, c + j * LANES:c + (j + 1) * LANES] = half.astype(BF16)
        tasks.extend((True, partial(epilogue, c=c)) for c in range(0, width, cw))

    def plain(out_ref, width):
        def epilogue(acc, ssum, c):
            out_ref[:, c:c + cw] = acc.astype(BF16)
        tasks.extend((False, partial(epilogue, c=c)) for c in range(0, width, cw))

    def normed_groups(out_refs, gain):
        def epilogue(acc, ssum, g):
            for j, half in enumerate(_head_norm_rope(acc, ssum, gain, c_tab, s1_tab, s2_tab)):
                y_scr[j] = half
            split_residues(out_refs[g], DILATED_GROUPS[g][1])
        tasks.extend((True, partial(epilogue, g=g)) for g in range(N_GROUPS_B))

    def plain_groups(out_refs):
        def epilogue(acc, ssum, g):
            for j in range(0, WIDTH_B, LANES):
                y_scr[j // LANES] = acc[:, j:j + LANES]
            split_residues(out_refs[g], DILATED_GROUPS[g][1])
        tasks.extend((False, partial(epilogue, g=g)) for g in range(N_GROUPS_B))

    def gate(acc, ssum, c):
        gates_ref[:, c:c + cw] = jax.nn.sigmoid(acc + gbias_ref[:, c:c + cw]).astype(BF16)

    normed(qa_ref, WIDTH_A, gain_a_ref[0:1, :] * (SM_SCALE * LOG2E))
    normed(ka_ref, WIDTH_A, gain_a_ref[1:2, :])
    plain(va_ref, WIDTH_A)
    normed_groups(qb_refs, gain_b_ref[0:1, :] * SM_SCALE)
    normed_groups(kb_refs, gain_b_ref[1:2, :])
    plain_groups(vb_refs)
    tasks.extend((False, partial(gate, c=c)) for c in range(0, GATE_COLS, cw))
    assert len(tasks) * cw == IN_COLS and WIDTH_B == cw

    def proj(n):
        return jnp.dot(h_scr[...], w_ref[:, n * cw:(n + 1) * cw], preferred_element_type=F32)

    def head_sums(n, acc):
        if not tasks[n][0]:
            return None
        return jnp.dot((acc * acc).astype(BF16), ones_ref[...], preferred_element_type=F32)

    last = len(tasks) - 1
    accs = {0: proj(0), 1: proj(1)}
    sums = {0: head_sums(0, accs[0])}
    for n in range(len(tasks)):
        if n + 2 <= last:
            accs[n + 2] = proj(n + 2)
        if n + 1 <= last:
            sums[n + 1] = head_sums(n + 1, accs[n + 1])
        tasks[n][1](accs.pop(n), sums.pop(n))


def _in_projection(x2, mod3, g, w_bf, tabs, gain_a, gain_b, gate_bias, seq):
    n, d = x2.shape
    tm = TOKEN_TILE
    tiles = seq // tm
    heads = jnp.arange(PROJ_CHUNK) // HEAD_DIM
    seg_ones = (heads[:, None] == heads[None, :]).astype(BF16)
    row = lambda i: (i, 0)
    fixed = lambda i: (0, 0)
    res_block = lambda i: (i // tiles, 0, i % tiles, 0)
    wide = [pl.BlockSpec((tm, WIDTH_A), row)] * 3
    wide_shape = [jax.ShapeDtypeStruct((n, WIDTH_A), BF16)] * 3
    grp = [pl.BlockSpec((1, dil, tm // dil, WIDTH_B), res_block) for _, dil in DILATED_GROUPS] * 3
    grp_shape = [jax.ShapeDtypeStruct((n // seq, dil, seq // dil, WIDTH_B), BF16) for _, dil in DILATED_GROUPS] * 3
    outs = pl.pallas_call(
        _inproj_kernel,
        grid=(n // tm,),
        in_specs=[pl.BlockSpec((tm, d), row),
                  pl.BlockSpec((1, 1, mod3.shape[2]), lambda i: (i // tiles, 0, 0)),
                  _resident((1, d), fixed),
                  _resident((d, IN_COLS), fixed),
                  pl.BlockSpec((tm, LANES), row),
                  pl.BlockSpec((tm, LANES), row),
                  pl.BlockSpec((tm, LANES), row),
                  _resident((2, LANES), fixed),
                  _resident((2, LANES), fixed),
                  _resident((1, GATE_COLS), fixed),
                  _resident((PROJ_CHUNK, PROJ_CHUNK), fixed)],
        out_specs=wide + grp + [pl.BlockSpec((tm, GATE_COLS), row)],
        out_shape=wide_shape + grp_shape + [jax.ShapeDtypeStruct((n, GATE_COLS), BF16)],
        scratch_shapes=[pltpu.VMEM((tm, d), BF16), pltpu.VMEM((WIDTH_B // LANES, tm, LANES), F32)],
        compiler_params=_params(("parallel",), 56),
        name="in_proj",
    )(x2, mod3, g, w_bf, *tabs, gain_a, gain_b, gate_bias, seg_ones)
    return outs[0], outs[1], outs[2], outs[3:6], outs[6:9], outs[9:12], outs[12]


DA_TQ = 256
DA_TK = 512
DA_FINISH_AFTER = 1024
DA_ONES_ROWS = 16


def _diff_attn_kernel(lam_ref, g_ref, q_ref, k_ref, v_ref, o_ref, sa_scr, sb_scr, ma_scr, mb_scr,
                      acca_scr, accb_scr, vt_scr, *, nq, lam_init):
    seq = k_ref.shape[1]
    tq = q_ref.shape[1]
    t = pl.program_id(0)

    @pl.when(t == 0)
    def _():
        sb_scr[...] = jnp.zeros(sb_scr.shape, F32)
        mb_scr[...] = jnp.zeros(mb_scr.shape, F32)
        accb_scr[...] = jnp.ones(accb_scr.shape, F32)

    @pl.when((t == 0) | ((t - 1) % nq == 0))
    def _():
        vt_scr[0:VDIM_A, :] = v_ref[0].T
        vt_scr[VDIM_A:VDIM_A + DA_ONES_ROWS, :] = jnp.ones((DA_ONES_ROWS, seq), BF16)

    def finish(acc_r):
        o = acc_r[0:VDIM_A, :] / acc_r[VDIM_A:VDIM_A + 1, :]
        lv = lam_ref[...]
        lam = (jnp.exp(jnp.sum(lv[0:1] * lv[1:2], axis=-1, keepdims=True))
               - jnp.exp(jnp.sum(lv[2:3] * lv[3:4], axis=-1, keepdims=True)) + lam_init)
        out = o[:, :tq] - lam * o[:, tq:]
        ms = jnp.mean(out * out, axis=0, keepdims=True)
        y = out * lax.rsqrt(ms + EPS) * g_ref[...] * (1.0 - lam_init)
        o_ref[0] = y.T.astype(BF16)

    def step(s_w, m_w, acc_w, s_r, m_r, acc_r):
        q = q_ref[0]
        lane = lax.broadcasted_iota(jnp.int32, q.shape, 1)
        zero = jnp.zeros_like(q)
        qs = jnp.concatenate([jnp.where(lane < HEAD_DIM, q, zero),
                              jnp.where(lane < HEAD_DIM, zero, q)], axis=0)
        m_prev = m_r[0:1]
        m_new = None
        acc = None
        for c in range(0, seq, DA_TK):
            st = lax.dot_general(k_ref[0, c:c + DA_TK, :], qs, NT_DIMS, preferred_element_type=F32)
            s_w[c:c + DA_TK, :] = st
            mc = jnp.max(st, axis=0, keepdims=True)
            m_new = mc if m_new is None else jnp.maximum(m_new, mc)
            pt = jnp.exp2((s_r[c:c + DA_TK, :] - m_prev).astype(BF16))
            part = jnp.dot(vt_scr[:, c:c + DA_TK], pt, preferred_element_type=F32)
            acc = part if acc is None else acc + part
            if c == DA_FINISH_AFTER:
                finish(acc_r)
        m_w[...] = jnp.broadcast_to(m_new, m_w.shape)
        acc_w[...] = acc

    @pl.when(t % 2 == 0)
    def _():
        step(sa_scr, ma_scr, acca_scr, sb_scr, mb_scr, accb_scr)

    @pl.when(t % 2 == 1)
    def _():
        step(sb_scr, mb_scr, accb_scr, sa_scr, ma_scr, acca_scr)


def _diff_attention(qa, ka, va, lam_p, subln_g, lam_init):
    b, s, _ = qa.shape
    nq = s // DA_TQ
    n = b * HEADS_A * nq

    def blk(t):
        t = jnp.clip(t, 0, n - 1)
        return t // (HEADS_A * nq), (t // nq) % HEADS_A, t % nq

    def q_map(t):
        bi, h, i = blk(t)
        return bi, i, h

    def k_map(t):
        bi, h, _ = blk(t)
        return bi, 0, h

    def v_map(t):
        bi, h, _ = blk(t - 1)
        return bi, 0, h

    def o_map(t):
        bi, h, i = blk(t - 2)
        return bi, i, h

    score_buf = pltpu.VMEM((s, 2 * DA_TQ), F32)
    max_buf = pltpu.VMEM((8, 2 * DA_TQ), F32)
    acc_buf = pltpu.VMEM((VDIM_A + DA_ONES_ROWS, 2 * DA_TQ), F32)
    return pl.pallas_call(
        partial(_diff_attn_kernel, nq=nq, lam_init=lam_init),
        grid=(n + 2,),
        in_specs=[pl.BlockSpec(lam_p.shape, lambda t: (0, 0)),
                  pl.BlockSpec((VDIM_A, 1), lambda t: (0, 0)),
                  pl.BlockSpec((1, DA_TQ, VDIM_A), q_map),
                  pl.BlockSpec((1, s, VDIM_A), k_map),
                  pl.BlockSpec((1, s, VDIM_A), v_map)],
        out_specs=pl.BlockSpec((1, DA_TQ, VDIM_A), o_map),
        out_shape=jax.ShapeDtypeStruct((b, s, WIDTH_A), BF16),
        scratch_shapes=[score_buf, score_buf, max_buf, max_buf, acc_buf, acc_buf,
                        pltpu.VMEM((VDIM_A + DA_ONES_ROWS, s), BF16)],
        compiler_params=_params(("arbitrary",), 40),
        name="diff_attn",
    )(lam_p, subln_g, qa, ka, va)


DIL_TQ = 128
DIL_SKEW = 1
DIL_WIN = 256


def _band_scores(q, kw):
    head = lax.broadcasted_iota(jnp.int32, q.shape, 1) // HEAD_DIM
    zero = jnp.zeros_like(q)
    qs = jnp.concatenate([jnp.where(head == h, q, zero) for h in range(HEADS_PER_GROUP_B)], axis=0)
    return lax.dot_general(qs, kw, NT_DIMS, preferred_element_type=F32)


def _band_softmax(s, rel, offset, side):
    s = jnp.where(jnp.abs(rel + offset) <= side, s, NEG_INF)
    m = jnp.max(s, axis=-1, keepdims=True)
    p = jnp.exp(s - m)
    l = jnp.sum(p, axis=-1, keepdims=True)
    return p.astype(BF16), m, l


def _band_output(p, m, l, vw, nq):
    o = jnp.dot(p, vw, preferred_element_type=F32) / l
    lse = jnp.broadcast_to(m + jnp.log(l), o.shape)
    head = lax.broadcasted_iota(jnp.int32, (nq, WIDTH_B), 1) // HEAD_DIM
    o_sel = jnp.zeros((nq, WIDTH_B), F32)
    lse_sel = jnp.zeros((nq, WIDTH_B), F32)
    for h in range(HEADS_PER_GROUP_B):
        rs = slice(h * nq, (h + 1) * nq)
        o_sel = jnp.where(head == h, o[rs], o_sel)
        lse_sel = jnp.where(head == h, lse[rs], lse_sel)
    return o_sel, lse_sel


def _dilated_kernel(*refs):
    q_refs, k_refs, v_refs = refs[0:3], refs[3:6], refs[6:9]
    ob_ref, o_scr, lse_scr = refs[9:12]
    tile = ob_ref.shape[0]
    i = pl.program_id(1)

    blocks = []
    for g, (window, dil) in enumerate(DILATED_GROUPS):
        side = window // (2 * dil)
        length = k_refs[g].shape[2]
        per_res = tile // dil
        nq = min(per_res, DIL_TQ)
        rows = lax.broadcasted_iota(jnp.int32, (HEADS_PER_GROUP_B * nq, DIL_WIN), 0)
        cols = lax.broadcasted_iota(jnp.int32, (HEADS_PER_GROUP_B * nq, DIL_WIN), 1)
        rel = (rows & (nq - 1)) - cols
        for r in range(dil):
            for u in range(per_res // nq):
                t0 = i * per_res + u * nq
                if length == DIL_WIN:
                    ks = 0
                else:
                    ks = pl.multiple_of(jnp.clip(t0 - side, 0, length - DIL_WIN), side)
                blocks.append(dict(g=g, dil=dil, r=r, u=u, nq=nq, t0=t0, ks=ks, side=side, rel=rel))

    def scores(bk):
        q = q_refs[bk["g"]][0, bk["r"], bk["u"] * bk["nq"]:(bk["u"] + 1) * bk["nq"], :]
        kw = k_refs[bk["g"]][0, bk["r"], pl.ds(bk["ks"], DIL_WIN), :]
        bk["s"] = _band_scores(q, kw)

    def softmax(bk):
        bk["pml"] = _band_softmax(bk.pop("s"), bk["rel"], bk["t0"] - bk["ks"], bk["side"])

    def output(bk):
        g, dil, r, u, nq = bk["g"], bk["dil"], bk["r"], bk["u"], bk["nq"]
        vw = v_refs[g][0, r, pl.ds(bk["ks"], DIL_WIN), :]
        o, lse = _band_output(*bk.pop("pml"), vw, nq)
        rows = pl.ds(u * nq * dil + r, nq, stride=dil) if dil > 1 else pl.ds(u * nq, nq)
        for j in range(WIDTH_B // LANES):
            o_scr[g, j, rows, :] = o[:, j * LANES:(j + 1) * LANES]
            lse_scr[g, j, rows, :] = lse[:, j * LANES:(j + 1) * LANES]

    for n in range(len(blocks) + 2 * DIL_SKEW):
        if n < len(blocks):
            scores(blocks[n])
        if 0 <= n - DIL_SKEW < len(blocks):
            softmax(blocks[n - DIL_SKEW])
        if 0 <= n - 2 * DIL_SKEW < len(blocks):
            output(blocks[n - 2 * DIL_SKEW])

    for j in range(WIDTH_B // LANES):
        l0, l1, l2 = lse_scr[0, j], lse_scr[1, j], lse_scr[2, j]
        m = jnp.maximum(jnp.maximum(l0, l1), l2)
        e0, e1, e2 = jnp.exp(l0 - m), jnp.exp(l1 - m), jnp.exp(l2 - m)
        ob = (e0 * o_scr[0, j] + e1 * o_scr[1, j] + e2 * o_scr[2, j]) / (e0 + e1 + e2)
        ob_ref[:, j * LANES:(j + 1) * LANES] = ob.astype(BF16)


def _dilated_attention(qs, ks, vs):
    b = qs[0].shape[0]
    seq = qs[0].shape[1] * qs[0].shape[2]
    tile = TOKEN_TILE
    tiles = seq // tile
    q_specs = [pl.BlockSpec((1, dil, tile // dil, WIDTH_B), lambda bi, i: (bi, 0, i, 0))
               for _, dil in DILATED_GROUPS]
    kv_specs = [pl.BlockSpec((1, dil, seq // dil, WIDTH_B), lambda bi, i: (bi, 0, 0, 0),
                             pipeline_mode=pl.Buffered(1))
                for _, dil in DILATED_GROUPS]
    return pl.pallas_call(
        _dilated_kernel,
        grid=(b, tiles),
        in_specs=q_specs + kv_specs + kv_specs,
        out_specs=pl.BlockSpec((tile, WIDTH_B), lambda bi, i: (bi * tiles + i, 0)),
        out_shape=jax.ShapeDtypeStruct((b * seq, WIDTH_B), BF16),
        scratch_shapes=[pltpu.VMEM((N_GROUPS_B, WIDTH_B // LANES, tile, LANES), F32)] * 2,
        compiler_params=_params(("parallel", "arbitrary"), 48),
        name="dilated_attn",
    )(*qs, *ks, *vs)


def _post_kernel(x_ref, mod_ref, oa_ref, ob_ref, gates_ref, wa_ref, wb_ref, wo_ref, g_ref, wu_ref, wd_ref,
                 out_ref):
    br_a = jnp.dot(oa_ref[...], wa_ref[...], preferred_element_type=F32)
    br_b = jnp.dot(ob_ref[...], wb_ref[...], preferred_element_type=F32)
    g_a = gates_ref[:, 0:D_MODEL].astype(F32)
    g_b = gates_ref[:, D_MODEL:GATE_COLS].astype(F32)
    y = jnp.dot((g_a * br_a + g_b * br_b).astype(BF16), wo_ref[...], preferred_element_type=F32)
    mod = lambda k: mod_ref[0, :, k * D_MODEL:(k + 1) * D_MODEL]
    x = x_ref[...] + mod(2) * y

    ms = jnp.mean(x * x, axis=-1, keepdims=True)
    h = (x * lax.rsqrt(ms + EPS) * g_ref[...] * (1.0 + mod(4)) + mod(3)).astype(BF16)
    u = jnp.maximum(jnp.dot(h, wu_ref[...], preferred_element_type=F32), 0.0)
    dn = jnp.dot((u * u).astype(BF16), wd_ref[...], preferred_element_type=F32)
    out_ref[...] = x + mod(5) * dn


def _post_attention(x2, mod3, oa, ob, gates, wa, wb, wo, g, wu, wd, seq):
    n, d = x2.shape
    tm = TOKEN_TILE
    row = lambda i: (i, 0)
    fixed = lambda i: (0, 0)
    return pl.pallas_call(
        _post_kernel,
        grid=(n // tm,),
        in_specs=[pl.BlockSpec((tm, d), row),
                  pl.BlockSpec((1, 1, mod3.shape[2]), lambda i: ((i * tm) // seq, 0, 0)),
                  pl.BlockSpec((tm, WIDTH_A), row),
                  pl.BlockSpec((tm, WIDTH_B), row),
                  pl.BlockSpec((tm, GATE_COLS), row),
                  _resident(wa.shape, fixed), _resident(wb.shape, fixed), _resident(wo.shape, fixed),
                  _resident((1, d), fixed), _resident(wu.shape, fixed), _resident(wd.shape, fixed)],
        out_specs=pl.BlockSpec((tm, d), row),
        out_shape=jax.ShapeDtypeStruct((n, d), F32),
        compiler_params=_params(("parallel",), 60),
        name="post_attn",
    )(x2, mod3, oa, ob, gates, wa, wb, wo, g, wu, wd)


def kernel(x, c, positions, ada_w, ada_b, norm_mix_g, norm_mlp_g, w_in, qk_gain_a, lambda_a, subln_g_a,
           qk_gain_b, w_branch_a, w_branch_b, gate_bias, w_out, w_mlp_up, w_mlp_down):
    b, s, d = x.shape
    depth = ada_w.shape[0]
    tabs = _rope_tables(positions)
    mod = _ada_mod(c, ada_w, ada_b)
    rep = LANES // HEAD_DIM
    x2 = x.reshape(b * s, d)
    for l in range(depth):
        mod3 = mod[l].reshape(b, 1, 6 * d)
        qa, ka, va, qbs, kbs, vbs, gates = _in_projection(
            x2, mod3, norm_mix_g[l].reshape(1, d), w_in[l].astype(BF16), tabs,
            jnp.tile(qk_gain_a[l], (1, rep)), jnp.tile(qk_gain_b[l], (1, rep)),
            gate_bias[l].reshape(1, GATE_COLS), s)

        lam_init = 0.8 - 0.6 * math.exp(-0.3 * l)
        oa = _diff_attention(qa.reshape(b, s, WIDTH_A), ka.reshape(b, s, WIDTH_A),
                             va.reshape(b, s, WIDTH_A), lambda_a[l], subln_g_a[l].reshape(VDIM_A, 1),
                             lam_init)
        ob = _dilated_attention(qbs, kbs, vbs)

        x2 = _post_attention(x2, mod3, oa.reshape(b * s, WIDTH_A), ob, gates,
                             w_branch_a[l].astype(BF16), w_branch_b[l].astype(BF16), w_out[l].astype(BF16),
                             norm_mlp_g[l].reshape(1, d), w_mlp_up[l].astype(BF16),
                             w_mlp_down[l].astype(BF16), s)
    return x2.reshape(b, s, d)
```

```python
import math
from functools import partial

import jax
import jax.numpy as jnp
from jax import lax
from jax.experimental import pallas as pl
from jax.experimental.pallas import tpu as pltpu

D_MODEL = 1024
HEAD_DIM = 64
HEADS_A = 8
VDIM_A = 2 * HEAD_DIM
WIDTH_A = HEADS_A * VDIM_A
DILATED_GROUPS = ((128, 1), (512, 4), (2048, 16))
N_GROUPS_B = 3
HEADS_PER_GROUP_B = 4
WIDTH_B = HEADS_PER_GROUP_B * HEAD_DIM
QKV_B_COLS = N_GROUPS_B * WIDTH_B
D_FF = 4 * D_MODEL
ROPE_THETA = 500000.0
ROT_HALF = HEAD_DIM // 4 // 2
EPS = 1e-6
NEG_INF = -1e30
GATE_COLS = 2 * D_MODEL
IN_COLS = 3 * WIDTH_A + 3 * QKV_B_COLS + GATE_COLS
SM_SCALE = HEAD_DIM ** -0.5
LOG2E = math.log2(math.e)

LANES = 128
PROJ_CHUNK = 256
TOKEN_TILE = 512
VMEM_BYTES = 64 * 1024 * 1024
MIB = 1024 * 1024

BF16 = jnp.bfloat16
F32 = jnp.float32
NT_DIMS = (((1,), (1,)), ((), ()))


def _params(semantics, vmem_mib):
    return pltpu.CompilerParams(dimension_semantics=semantics,
                                vmem_limit_bytes=min(vmem_mib * MIB, VMEM_BYTES - 4 * MIB))


def _resident(shape, index_map):
    return pl.BlockSpec(shape, index_map, pipeline_mode=pl.Buffered(1))


def _rope_kernel(pos_ref, cos_ref, sin_ref):
    pos = pos_ref[...].astype(F32)
    for f in range(ROT_HALF):
        ang = pos * (ROPE_THETA ** (-(2.0 * f) / (2 * ROT_HALF)))
        cos_ref[f] = jnp.cos(ang)
        sin_ref[f] = jnp.sin(ang)


def _rope_tables(positions):
    n = positions.size
    pos2 = positions.reshape(n // LANES, LANES)
    cos, sin = pl.pallas_call(
        _rope_kernel,
        out_shape=[jax.ShapeDtypeStruct((ROT_HALF, n // LANES, LANES), F32)] * 2,
    )(pos2)
    cos = cos.reshape(ROT_HALF, n).T
    sin = sin.reshape(ROT_HALF, n).T
    one = jnp.ones((n, HEAD_DIM - 2 * ROT_HALF), F32)
    zero = jnp.zeros((n, HEAD_DIM - 2 * ROT_HALF), F32)
    z8 = jnp.zeros((n, ROT_HALF), F32)
    c_tab = jnp.concatenate([cos, cos, one], axis=1)
    s1_tab = jnp.concatenate([z8, sin, zero], axis=1)
    s2_tab = jnp.concatenate([sin, z8, zero], axis=1)
    rep = LANES // HEAD_DIM
    return tuple(jnp.tile(t, (1, rep)) for t in (c_tab, s1_tab, s2_tab))


def _ada_kernel(c_ref, w_ref, b_ref, o_ref):
    c = c_ref[...]
    cond = (c * jax.nn.sigmoid(c)).astype(BF16)
    o_ref[0] = jnp.dot(cond, w_ref[0].astype(BF16), preferred_element_type=F32) + b_ref[0]


def _ada_mod(c, ada_w, ada_b):
    depth, d, n = ada_w.shape
    b = c.shape[0]
    tn = 768
    return pl.pallas_call(
        _ada_kernel,
        grid=(depth, n // tn),
        in_specs=[pl.BlockSpec((b, d), lambda l, j: (0, 0)),
                  pl.BlockSpec((1, d, tn), lambda l, j: (l, 0, j)),
                  pl.BlockSpec((1, 1, tn), lambda l, j: (l, 0, j))],
        out_specs=pl.BlockSpec((1, b, tn), lambda l, j: (l, 0, j)),
        out_shape=jax.ShapeDtypeStruct((depth, b, n), F32),
        compiler_params=_params(("parallel", "parallel"), 32),
    )(c, ada_w, ada_b.reshape(depth, 1, n))


def _head_norm_rope(acc, ssum, gain, c_tab, s1_tab, s2_tab):
    y = acc * lax.rsqrt(ssum * (1.0 / HEAD_DIM) + EPS)
    halves = []
    for j in range(0, acc.shape[1], LANES):
        yj = y[:, j:j + LANES] * gain
        halves.append(yj * c_tab + pltpu.roll(yj, ROT_HALF, 1) * s1_tab
                      - pltpu.roll(yj, LANES - ROT_HALF, 1) * s2_tab)
    return halves


def _inproj_kernel(x_ref, mod_ref, g_ref, w_ref, c_ref, s1_ref, s2_ref, gain_a_ref, gain_b_ref,
                   gbias_ref, ones_ref, qa_ref, ka_ref, va_ref, *rest):
    qb_refs, kb_refs, vb_refs = rest[0:3], rest[3:6], rest[6:9]
    gates_ref, h_scr, y_scr = rest[9:12]
    tm = x_ref.shape[0]
    x = x_ref[...]
    ms = jnp.mean(x * x, axis=-1, keepdims=True)
    shift = mod_ref[0, :, 0:D_MODEL]
    scale = mod_ref[0, :, D_MODEL:2 * D_MODEL]
    h = x * lax.rsqrt(ms + EPS) * g_ref[...] * (1.0 + scale) + shift
    h_scr[...] = h.astype(BF16)

    c_tab, s1_tab, s2_tab = c_ref[...], s1_ref[...], s2_ref[...]
    cw = PROJ_CHUNK

    def split_residues(out_ref, dilation):
        for r in range(dilation):
            for j in range(WIDTH_B // LANES):
                out_ref[0, r, :, j * LANES:(j + 1) * LANES] = (
                    y_scr[j, pl.ds(r, tm // dilation, stride=dilation), :].astype(BF16))

    tasks = []

    def normed(out_ref, width, gain):
        def epilogue(acc, ssum, c):
            for j, half in enumerate(_head_norm_rope(acc, ssum, gain, c_tab, s1_tab, s2_tab)):
                out_ref[:, c + j * LANES:c + (j + 1) * LANES] = half.astype(BF16)
        tasks.extend((True, partial(epilogue, c=c)) for c in range(0, width, cw))

    def plain(out_ref, width):
        def epilogue(acc, ssum, c):
            out_ref[:, c:c + cw] = acc.astype(BF16)
        tasks.extend((False, partial(epilogue, c=c)) for c in range(0, width, cw))

    def normed_groups(out_refs, gain):
        def epilogue(acc, ssum, g):
            for j, half in enumerate(_head_norm_rope(acc, ssum, gain, c_tab, s1_tab, s2_tab)):
                y_scr[j] = half
            split_residues(out_refs[g], DILATED_GROUPS[g][1])
        tasks.extend((True, partial(epilogue, g=g)) for g in range(N_GROUPS_B))

    def plain_groups(out_refs):
        def epilogue(acc, ssum, g):
            for j in range(0, WIDTH_B, LANES):
                y_scr[j // LANES] = acc[:, j:j + LANES]
            split_residues(out_refs[g], DILATED_GROUPS[g][1])
        tasks.extend((False, partial(epilogue, g=g)) for g in range(N_GROUPS_B))

    def gate(acc, ssum, c):
        gates_ref[:, c:c + cw] = jax.nn.sigmoid(acc + gbias_ref[:, c:c + cw]).astype(BF16)

    normed(qa_ref, WIDTH_A, gain_a_ref[0:1, :] * (SM_SCALE * LOG2E))
    normed(ka_ref, WIDTH_A, gain_a_ref[1:2, :])
    plain(va_ref, WIDTH_A)
    normed_groups(qb_refs, gain_b_ref[0:1, :] * SM_SCALE)
    normed_groups(kb_refs, gain_b_ref[1:2, :])
    plain_groups(vb_refs)
    tasks.extend((False, partial(gate, c=c)) for c in range(0, GATE_COLS, cw))
    assert len(tasks) * cw == IN_COLS and WIDTH_B == cw

    def proj(n):
        return jnp.dot(h_scr[...], w_ref[:, n * cw:(n + 1) * cw], preferred_element_type=F32)

    def head_sums(n, acc):
        if not tasks[n][0]:
            return None
        return jnp.dot((acc * acc).astype(BF16), ones_ref[...], preferred_element_type=F32)

    last = len(tasks) - 1
    accs = {0: proj(0), 1: proj(1)}
    sums = {0: head_sums(0, accs[0])}
    for n in range(len(tasks)):
        if n + 2 <= last:
            accs[n + 2] = proj(n + 2)
        if n + 1 <= last:
            sums[n + 1] = head_sums(n + 1, accs[n + 1])
        tasks[n][1](accs.pop(n), sums.pop(n))


def _in_projection(x2, mod3, g, w_bf, tabs, gain_a, gain_b, gate_bias, seq):
    n, d = x2.shape
    tm = TOKEN_TILE
    tiles = seq // tm
    heads = jnp.arange(PROJ_CHUNK) // HEAD_DIM
    seg_ones = (heads[:, None] == heads[None, :]).astype(BF16)
    row = lambda i: (i, 0)
    fixed = lambda i: (0, 0)
    res_block = lambda i: (i // tiles, 0, i % tiles, 0)
    wide = [pl.BlockSpec((tm, WIDTH_A), row)] * 3
    wide_shape = [jax.ShapeDtypeStruct((n, WIDTH_A), BF16)] * 3
    grp = [pl.BlockSpec((1, dil, tm // dil, WIDTH_B), res_block) for _, dil in DILATED_GROUPS] * 3
    grp_shape = [jax.ShapeDtypeStruct((n // seq, dil, seq // dil, WIDTH_B), BF16) for _, dil in DILATED_GROUPS] * 3
    outs = pl.pallas_call(
        _inproj_kernel,
        grid=(n // tm,),
        in_specs=[pl.BlockSpec((tm, d), row),
                  pl.BlockSpec((1, 1, mod3.shape[2]), lambda i: (i // tiles, 0, 0)),
                  _resident((1, d), fixed),
                  _resident((d, IN_COLS), fixed),
                  pl.BlockSpec((tm, LANES), row),
                  pl.BlockSpec((tm, LANES), row),
                  pl.BlockSpec((tm, LANES), row),
                  _resident((2, LANES), fixed),
                  _resident((2, LANES), fixed),
                  _resident((1, GATE_COLS), fixed),
                  _resident((PROJ_CHUNK, PROJ_CHUNK), fixed)],
        out_specs=wide + grp + [pl.BlockSpec((tm, GATE_COLS), row)],
        out_shape=wide_shape + grp_shape + [jax.ShapeDtypeStruct((n, GATE_COLS), BF16)],
        scratch_shapes=[pltpu.VMEM((tm, d), BF16), pltpu.VMEM((WIDTH_B // LANES, tm, LANES), F32)],
        compiler_params=_params(("parallel",), 56),
        name="in_proj",
    )(x2, mod3, g, w_bf, *tabs, gain_a, gain_b, gate_bias, seg_ones)
    return outs[0], outs[1], outs[2], outs[3:6], outs[6:9], outs[9:12], outs[12]


DA_TQ = 256
DA_TK = 512
DA_FINISH_AFTER = 1024
DA_ONES_ROWS = 16


def _diff_attn_kernel(lam_ref, g_ref, q_ref, k_ref, v_ref, o_ref, sa_scr, sb_scr, ma_scr, mb_scr,
                      acca_scr, accb_scr, vt_scr, *, nq, lam_init):
    seq = k_ref.shape[1]
    tq = DA_TQ
    t = pl.program_id(0)

    def load_values():
        vt_scr[0:VDIM_A, :] = v_ref[0].T
        vt_scr[VDIM_A:VDIM_A + DA_ONES_ROWS, :] = jnp.ones((DA_ONES_ROWS, seq), BF16)

    @pl.when(t == 0)
    def _():
        sb_scr[...] = jnp.zeros(sb_scr.shape, F32)
        mb_scr[...] = jnp.zeros(mb_scr.shape, F32)
        accb_scr[...] = jnp.ones(accb_scr.shape, F32)
        load_values()

    def finish(acc_r, rows):
        o = acc_r[0:VDIM_A, :] / acc_r[VDIM_A:VDIM_A + 1, :]
        lv = lam_ref[...]
        lam = (jnp.exp(jnp.sum(lv[0:1] * lv[1:2], axis=-1, keepdims=True))
               - jnp.exp(jnp.sum(lv[2:3] * lv[3:4], axis=-1, keepdims=True)) + lam_init)
        out = o[:, :tq] - lam * o[:, tq:]
        ms = jnp.mean(out * out, axis=0, keepdims=True)
        y = out * lax.rsqrt(ms + EPS) * g_ref[...] * (1.0 - lam_init)
        o_ref[0, rows, :] = y.T.astype(BF16)

    def step(rows, s_w, m_w, acc_w, s_r, m_r, acc_r):
        q = q_ref[0, rows, :]
        lane = lax.broadcasted_iota(jnp.int32, q.shape, 1)
        zero = jnp.zeros_like(q)
        qs = jnp.concatenate([jnp.where(lane < HEAD_DIM, q, zero),
                              jnp.where(lane < HEAD_DIM, zero, q)], axis=0)
        m_prev = m_r[0:1]
        m_new = None
        acc = None
        for c in range(0, seq, DA_TK):
            st = lax.dot_general(k_ref[0, c:c + DA_TK, :], qs, NT_DIMS, preferred_element_type=F32)
            s_w[c:c + DA_TK, :] = st
            mc = jnp.max(st, axis=0, keepdims=True)
            m_new = mc if m_new is None else jnp.maximum(m_new, mc)
            pt = jnp.exp2((s_r[c:c + DA_TK, :] - m_prev).astype(BF16))
            part = jnp.dot(vt_scr[:, c:c + DA_TK], pt, preferred_element_type=F32)
            acc = part if acc is None else acc + part
            if c == DA_FINISH_AFTER:
                finish(acc_r, rows)
        m_w[...] = jnp.broadcast_to(m_new, m_w.shape)
        acc_w[...] = acc

    step(slice(0, tq), sa_scr, ma_scr, acca_scr, sb_scr, mb_scr, accb_scr)

    @pl.when((t > 0) & ((2 * t) % nq == 0))
    def _():
        load_values()

    step(slice(tq, 2 * tq), sb_scr, mb_scr, accb_scr, sa_scr, ma_scr, acca_scr)


def _diff_attention(qa, ka, va, lam_p, subln_g, lam_init):
    b, s, _ = qa.shape
    nq = s // DA_TQ
    assert nq % 2 == 0
    pairs = b * HEADS_A * nq // 2

    def pair(t):
        t = jnp.clip(t, 0, pairs - 1)
        return t // (HEADS_A * nq // 2), (t // (nq // 2)) % HEADS_A, t % (nq // 2)

    def q_map(t):
        bi, h, i = pair(t)
        return bi, i, h

    def kv_map(t):
        bi, h, _ = pair(t)
        return bi, 0, h

    def o_map(t):
        bi, h, i = pair(t - 1)
        return bi, i, h

    score_buf = pltpu.VMEM((s, 2 * DA_TQ), F32)
    max_buf = pltpu.VMEM((8, 2 * DA_TQ), F32)
    acc_buf = pltpu.VMEM((VDIM_A + DA_ONES_ROWS, 2 * DA_TQ), F32)
    return pl.pallas_call(
        partial(_diff_attn_kernel, nq=nq, lam_init=lam_init),
        grid=(pairs + 1,),
        in_specs=[pl.BlockSpec(lam_p.shape, lambda t: (0, 0)),
                  pl.BlockSpec((VDIM_A, 1), lambda t: (0, 0)),
                  pl.BlockSpec((1, 2 * DA_TQ, VDIM_A), q_map),
                  pl.BlockSpec((1, s, VDIM_A), kv_map),
                  pl.BlockSpec((1, s, VDIM_A), kv_map)],
        out_specs=pl.BlockSpec((1, 2 * DA_TQ, VDIM_A), o_map),
        out_shape=jax.ShapeDtypeStruct((b, s, WIDTH_A), BF16),
        scratch_shapes=[score_buf, score_buf, max_buf, max_buf, acc_buf, acc_buf,
                        pltpu.VMEM((VDIM_A + DA_ONES_ROWS, s), BF16)],
        compiler_params=_params(("arbitrary",), 40),
        name="diff_attn",
    )(lam_p, subln_g, qa, ka, va)


DIL_TQ = 128
DIL_SKEW = 1
DIL_WIN = 256


def _band_scores(q, kw):
    head = lax.broadcasted_iota(jnp.int32, q.shape, 1) // HEAD_DIM
    zero = jnp.zeros_like(q)
    qs = jnp.concatenate([jnp.where(head == h, q, zero) for h in range(HEADS_PER_GROUP_B)], axis=0)
    return lax.dot_general(qs, kw, NT_DIMS, preferred_element_type=F32)


def _band_softmax(s, rel, offset, side):
    s = jnp.where(jnp.abs(rel + offset) <= side, s, NEG_INF)
    m = jnp.max(s, axis=-1, keepdims=True)
    p = jnp.exp(s - m)
    l = jnp.sum(p, axis=-1, keepdims=True)
    return p.astype(BF16), m, l


def _band_output(p, m, l, vw, nq):
    o = jnp.dot(p, vw, preferred_element_type=F32) / l
    lse = jnp.broadcast_to(m + jnp.log(l), o.shape)
    head = lax.broadcasted_iota(jnp.int32, (nq, WIDTH_B), 1) // HEAD_DIM
    o_sel = jnp.zeros((nq, WIDTH_B), F32)
    lse_sel = jnp.zeros((nq, WIDTH_B), F32)
    for h in range(HEADS_PER_GROUP_B):
        rs = slice(h * nq, (h + 1) * nq)
        o_sel = jnp.where(head == h, o[rs], o_sel)
        lse_sel = jnp.where(head == h, lse[rs], lse_sel)
    return o_sel, lse_sel


def _dilated_kernel(*refs):
    q_refs, k_refs, v_refs = refs[0:3], refs[3:6], refs[6:9]
    ob_ref, o_scr, lse_scr = refs[9:12]
    tile = ob_ref.shape[0]
    i = pl.program_id(1)

    blocks = []
    for g, (window, dil) in enumerate(DILATED_GROUPS):
        side = window // (2 * dil)
        length = k_refs[g].shape[2]
        per_res = tile // dil
        nq = min(per_res, DIL_TQ)
        rows = lax.broadcasted_iota(jnp.int32, (HEADS_PER_GROUP_B * nq, DIL_WIN), 0)
        cols = lax.broadcasted_iota(jnp.int32, (HEADS_PER_GROUP_B * nq, DIL_WIN), 1)
        rel = (rows & (nq - 1)) - cols
        for r in range(dil):
            for u in range(per_res // nq):
                t0 = i * per_res + u * nq
                if length == DIL_WIN:
                    ks = 0
                else:
                    ks = pl.multiple_of(jnp.clip(t0 - side, 0, length - DIL_WIN), side)
                blocks.append(dict(g=g, dil=dil, r=r, u=u, nq=nq, t0=t0, ks=ks, side=side, rel=rel))

    def scores(bk):
        q = q_refs[bk["g"]][0, bk["r"], bk["u"] * bk["nq"]:(bk["u"] + 1) * bk["nq"], :]
        kw = k_refs[bk["g"]][0, bk["r"], pl.ds(bk["ks"], DIL_WIN), :]
        bk["s"] = _band_scores(q, kw)

    def softmax(bk):
        bk["pml"] = _band_softmax(bk.pop("s"), bk["rel"], bk["t0"] - bk["ks"], bk["side"])

    def output(bk):
        g, dil, r, u, nq = bk["g"], bk["dil"], bk["r"], bk["u"], bk["nq"]
        vw = v_refs[g][0, r, pl.ds(bk["ks"], DIL_WIN), :]
        o, lse = _band_output(*bk.pop("pml"), vw, nq)
        rows = pl.ds(u * nq * dil + r, nq, stride=dil) if dil > 1 else pl.ds(u * nq, nq)
        for j in range(WIDTH_B // LANES):
            o_scr[g, j, rows, :] = o[:, j * LANES:(j + 1) * LANES]
            lse_scr[g, j, rows, :] = lse[:, j * LANES:(j + 1) * LANES]

    for n in range(len(blocks) + 2 * DIL_SKEW):
        if n < len(blocks):
            scores(blocks[n])
        if 0 <= n - DIL_SKEW < len(blocks):
            softmax(blocks[n - DIL_SKEW])
        if 0 <= n - 2 * DIL_SKEW < len(blocks):
            output(blocks[n - 2 * DIL_SKEW])

    for j in range(WIDTH_B // LANES):
        l0, l1, l2 = lse_scr[0, j], lse_scr[1, j], lse_scr[2, j]
        m = jnp.maximum(jnp.maximum(l0, l1), l2)
        e0, e1, e2 = jnp.exp(l0 - m), jnp.exp(l1 - m), jnp.exp(l2 - m)
        ob = (e0 * o_scr[0, j] + e1 * o_scr[1, j] + e2 * o_scr[2, j]) / (e0 + e1 + e2)
        ob_ref[:, j * LANES:(j + 1) * LANES] = ob.astype(BF16)


def _dilated_attention(qs, ks, vs):
    b = qs[0].shape[0]
    seq = qs[0].shape[1] * qs[0].shape[2]
    tile = TOKEN_TILE
    tiles = seq // tile
    q_specs = [pl.BlockSpec((1, dil, tile // dil, WIDTH_B), lambda bi, i: (bi, 0, i, 0))
               for _, dil in DILATED_GROUPS]
    kv_specs = [pl.BlockSpec((1, dil, seq // dil, WIDTH_B), lambda bi, i: (bi, 0, 0, 0),
                             pipeline_mode=pl.Buffered(1))
                for _, dil in DILATED_GROUPS]
    return pl.pallas_call(
        _dilated_kernel,
        grid=(b, tiles),
        in_specs=q_specs + kv_specs + kv_specs,
        out_specs=pl.BlockSpec((tile, WIDTH_B), lambda bi, i: (bi * tiles + i, 0)),
        out_shape=jax.ShapeDtypeStruct((b * seq, WIDTH_B), BF16),
        scratch_shapes=[pltpu.VMEM((N_GROUPS_B, WIDTH_B // LANES, tile, LANES), F32)] * 2,
        compiler_params=_params(("parallel", "arbitrary"), 48),
        name="dilated_attn",
    )(*qs, *ks, *vs)


def _post_kernel(x_ref, mod_ref, oa_ref, ob_ref, gates_ref, wa_ref, wb_ref, wo_ref, g_ref, wu_ref, wd_ref,
                 out_ref):
    br_a = jnp.dot(oa_ref[...], wa_ref[...], preferred_element_type=F32)
    br_b = jnp.dot(ob_ref[...], wb_ref[...], preferred_element_type=F32)
    g_a = gates_ref[:, 0:D_MODEL].astype(F32)
    g_b = gates_ref[:, D_MODEL:GATE_COLS].astype(F32)
    y = jnp.dot((g_a * br_a + g_b * br_b).astype(BF16), wo_ref[...], preferred_element_type=F32)
    mod = lambda k: mod_ref[0, :, k * D_MODEL:(k + 1) * D_MODEL]
    x = x_ref[...] + mod(2) * y

    ms = jnp.mean(x * x, axis=-1, keepdims=True)
    h = (x * lax.rsqrt(ms + EPS) * g_ref[...] * (1.0 + mod(4)) + mod(3)).astype(BF16)
    u = jnp.maximum(jnp.dot(h, wu_ref[...], preferred_element_type=F32), 0.0)
    dn = jnp.dot((u * u).astype(BF16), wd_ref[...], preferred_element_type=F32)
    out_ref[...] = x + mod(5) * dn


def _post_attention(x2, mod3, oa, ob, gates, wa, wb, wo, g, wu, wd, seq):
    n, d = x2.shape
    tm = TOKEN_TILE
    row = lambda i: (i, 0)
    fixed = lambda i: (0, 0)
    return pl.pallas_call(
        _post_kernel,
        grid=(n // tm,),
        in_specs=[pl.BlockSpec((tm, d), row),
                  pl.BlockSpec((1, 1, mod3.shape[2]), lambda i: ((i * tm) // seq, 0, 0)),
                  pl.BlockSpec((tm, WIDTH_A), row),
                  pl.BlockSpec((tm, WIDTH_B), row),
                  pl.BlockSpec((tm, GATE_COLS), row),
                  _resident(wa.shape, fixed), _resident(wb.shape, fixed), _resident(wo.shape, fixed),
                  _resident((1, d), fixed), _resident(wu.shape, fixed), _resident(wd.shape, fixed)],
        out_specs=pl.BlockSpec((tm, d), row),
        out_shape=jax.ShapeDtypeStruct((n, d), F32),
        compiler_params=_params(("parallel",), 60),
        name="post_attn",
    )(x2, mod3, oa, ob, gates, wa, wb, wo, g, wu, wd)


def kernel(x, c, positions, ada_w, ada_b, norm_mix_g, norm_mlp_g, w_in, qk_gain_a, lambda_a, subln_g_a,
           qk_gain_b, w_branch_a, w_branch_b, gate_bias, w_out, w_mlp_up, w_mlp_down):
    b, s, d = x.shape
    depth = ada_w.shape[0]
    tabs = _rope_tables(positions)
    mod = _ada_mod(c, ada_w, ada_b)
    rep = LANES // HEAD_DIM
    x2 = x.reshape(b * s, d)
    for l in range(depth):
        mod3 = mod[l].reshape(b, 1, 6 * d)
        qa, ka, va, qbs, kbs, vbs, gates = _in_projection(
            x2, mod3, norm_mix_g[l].reshape(1, d), w_in[l].astype(BF16), tabs,
            jnp.tile(qk_gain_a[l], (1, rep)), jnp.tile(qk_gain_b[l], (1, rep)),
            gate_bias[l].reshape(1, GATE_COLS), s)

        lam_init = 0.8 - 0.6 * math.exp(-0.3 * l)
        oa = _diff_attention(qa.reshape(b, s, WIDTH_A), ka.reshape(b, s, WIDTH_A),
                             va.reshape(b, s, WIDTH_A), lambda_a[l], subln_g_a[l].reshape(VDIM_A, 1),
                             lam_init)
        ob = _dilated_attention(qbs, kbs, vbs)

        x2 = _post_attention(x2, mod3, oa.reshape(b * s, WIDTH_A), ob, gates,
                             w_branch_a[l].astype(BF16), w_branch_b[l].astype(BF16), w_out[l].astype(BF16),
                             norm_mlp_g[l].reshape(1, d), w_mlp_up[l].astype(BF16),
                             w_mlp_down[l].astype(BF16), s)
    return x2.reshape(b, s, d)
```

```python
import math
from functools import partial

import jax
import jax.numpy as jnp
from jax import lax
from jax.experimental import pallas as pl
from jax.experimental.pallas import tpu as pltpu

D_MODEL = 1024
HEAD_DIM = 64
HEADS_A = 8
VDIM_A = 2 * HEAD_DIM
WIDTH_A = HEADS_A * VDIM_A
DILATED_GROUPS = ((128, 1), (512, 4), (2048, 16))
N_GROUPS_B = 3
HEADS_PER_GROUP_B = 4
WIDTH_B = HEADS_PER_GROUP_B * HEAD_DIM
QKV_B_COLS = N_GROUPS_B * WIDTH_B
D_FF = 4 * D_MODEL
ROPE_THETA = 500000.0
ROT_HALF = HEAD_DIM // 4 // 2
EPS = 1e-6
NEG_INF = -1e30
GATE_COLS = 2 * D_MODEL
IN_COLS = 3 * WIDTH_A + 3 * QKV_B_COLS + GATE_COLS
SM_SCALE = HEAD_DIM ** -0.5
LOG2E = math.log2(math.e)

LANES = 128
PROJ_CHUNK = 256
TOKEN_TILE = 512
VMEM_BYTES = 64 * 1024 * 1024
MIB = 1024 * 1024

BF16 = jnp.bfloat16
F32 = jnp.float32
NT_DIMS = (((1,), (1,)), ((), ()))


def _params(semantics, vmem_mib):
    return pltpu.CompilerParams(dimension_semantics=semantics,
                                vmem_limit_bytes=min(vmem_mib * MIB, VMEM_BYTES - 4 * MIB))


def _resident(shape, index_map):
    return pl.BlockSpec(shape, index_map, pipeline_mode=pl.Buffered(1))


def _rope_kernel(pos_ref, cos_ref, sin_ref):
    pos = pos_ref[...].astype(F32)
    for f in range(ROT_HALF):
        ang = pos * (ROPE_THETA ** (-(2.0 * f) / (2 * ROT_HALF)))
        cos_ref[f] = jnp.cos(ang)
        sin_ref[f] = jnp.sin(ang)


def _rope_tables(positions):
    n = positions.size
    pos2 = positions.reshape(n // LANES, LANES)
    cos, sin = pl.pallas_call(
        _rope_kernel,
        out_shape=[jax.ShapeDtypeStruct((ROT_HALF, n // LANES, LANES), F32)] * 2,
    )(pos2)
    cos = cos.reshape(ROT_HALF, n).T
    sin = sin.reshape(ROT_HALF, n).T
    one = jnp.ones((n, HEAD_DIM - 2 * ROT_HALF), F32)
    zero = jnp.zeros((n, HEAD_DIM - 2 * ROT_HALF), F32)
    z8 = jnp.zeros((n, ROT_HALF), F32)
    c_tab = jnp.concatenate([cos, cos, one], axis=1)
    s1_tab = jnp.concatenate([z8, sin, zero], axis=1)
    s2_tab = jnp.concatenate([sin, z8, zero], axis=1)
    rep = LANES // HEAD_DIM
    return tuple(jnp.tile(t, (1, rep)) for t in (c_tab, s1_tab, s2_tab))


def _ada_kernel(c_ref, w_ref, b_ref, o_ref):
    c = c_ref[...]
    cond = (c * jax.nn.sigmoid(c)).astype(BF16)
    o_ref[0] = jnp.dot(cond, w_ref[0].astype(BF16), preferred_element_type=F32) + b_ref[0]


def _ada_mod(c, ada_w, ada_b):
    depth, d, n = ada_w.shape
    b = c.shape[0]
    tn = 768
    return pl.pallas_call(
        _ada_kernel,
        grid=(depth, n // tn),
        in_specs=[pl.BlockSpec((b, d), lambda l, j: (0, 0)),
                  pl.BlockSpec((1, d, tn), lambda l, j: (l, 0, j)),
                  pl.BlockSpec((1, 1, tn), lambda l, j: (l, 0, j))],
        out_specs=pl.BlockSpec((1, b, tn), lambda l, j: (l, 0, j)),
        out_shape=jax.ShapeDtypeStruct((depth, b, n), F32),
        compiler_params=_params(("parallel", "parallel"), 32),
    )(c, ada_w, ada_b.reshape(depth, 1, n))


def _head_norm_rope(acc, ssum, gain, c_tab, s1_tab, s2_tab):
    y = acc * lax.rsqrt(ssum * (1.0 / HEAD_DIM) + EPS)
    halves = []
    for j in range(0, acc.shape[1], LANES):
        yj = y[:, j:j + LANES] * gain
        halves.append(yj * c_tab + pltpu.roll(yj, ROT_HALF, 1) * s1_tab
                      - pltpu.roll(yj, LANES - ROT_HALF, 1) * s2_tab)
    return halves


def _inproj_kernel(x_ref, mod_ref, g_ref, w_ref, c_ref, s1_ref, s2_ref, gain_a_ref, gain_b_ref,
                   gbias_ref, ones_ref, qa_ref, ka_ref, va_ref, *rest):
    qb_refs, kb_refs, vb_refs = rest[0:3], rest[3:6], rest[6:9]
    gates_ref, h_scr, y_scr = rest[9:12]
    tm = x_ref.shape[0]
    x = x_ref[...]
    ms = jnp.mean(x * x, axis=-1, keepdims=True)
    shift = mod_ref[0, :, 0:D_MODEL]
    scale = mod_ref[0, :, D_MODEL:2 * D_MODEL]
    h = x * lax.rsqrt(ms + EPS) * g_ref[...] * (1.0 + scale) + shift
    h_scr[...] = h.astype(BF16)

    c_tab, s1_tab, s2_tab = c_ref[...], s1_ref[...], s2_ref[...]
    cw = PROJ_CHUNK

    def split_residues(out_ref, dilation):
        for r in range(dilation):
            for j in range(WIDTH_B // LANES):
                out_ref[0, r, :, j * LANES:(j + 1) * LANES] = (
                    y_scr[j, pl.ds(r, tm // dilation, stride=dilation), :].astype(BF16))

    tasks = []

    def normed(out_ref, width, gain):
        def epilogue(acc, ssum, c):
            for j, half in enumerate(_head_norm_rope(acc, ssum, gain, c_tab, s1_tab, s2_tab)):
                out_ref[:, c + j * LANES:c + (j + 1) * LANES] = half.astype(BF16)
        tasks.extend((True, partial(epilogue, c=c)) for c in range(0, width, cw))

    def plain(out_ref, width):
        def epilogue(acc, ssum, c):
            out_ref[:, c:c + cw] = acc.astype(BF16)
        tasks.extend((False, partial(epilogue, c=c)) for c in range(0, width, cw))

    def normed_groups(out_refs, gain):
        def epilogue(acc, ssum, g):
            for j, half in enumerate(_head_norm_rope(acc, ssum, gain, c_tab, s1_tab, s2_tab)):
                y_scr[j] = half
            split_residues(out_refs[g], DILATED_GROUPS[g][1])
        tasks.extend((True, partial(epilogue, g=g)) for g in range(N_GROUPS_B))

    def plain_groups(out_refs):
        def epilogue(acc, ssum, g):
            for j in range(0, WIDTH_B, LANES):
                y_scr[j // LANES] = acc[:, j:j + LANES]
            split_residues(out_refs[g], DILATED_GROUPS[g][1])
        tasks.extend((False, partial(epilogue, g=g)) for g in range(N_GROUPS_B))

    def gate(acc, ssum, c):
        gates_ref[:, c:c + cw] = jax.nn.sigmoid(acc + gbias_ref[:, c:c + cw]).astype(BF16)

    normed(qa_ref, WIDTH_A, gain_a_ref[0:1, :] * (SM_SCALE * LOG2E))
    normed(ka_ref, WIDTH_A, gain_a_ref[1:2, :])
    plain(va_ref, WIDTH_A)
    normed_groups(qb_refs, gain_b_ref[0:1, :] * SM_SCALE)
    normed_groups(kb_refs, gain_b_ref[1:2, :])
    plain_groups(vb_refs)
    tasks.extend((False, partial(gate, c=c)) for c in range(0, GATE_COLS, cw))
    assert len(tasks) * cw == IN_COLS and WIDTH_B == cw

    def proj(n):
        return jnp.dot(h_scr[...], w_ref[:, n * cw:(n + 1) * cw], preferred_element_type=F32)

    def head_sums(n, acc):
        if not tasks[n][0]:
            return None
        return jnp.dot((acc * acc).astype(BF16), ones_ref[...], preferred_element_type=F32)

    last = len(tasks) - 1
    accs = {0: proj(0), 1: proj(1)}
    sums = {0: head_sums(0, accs[0])}
    for n in range(len(tasks)):
        if n + 2 <= last:
            accs[n + 2] = proj(n + 2)
        if n + 1 <= last:
            sums[n + 1] = head_sums(n + 1, accs[n + 1])
        tasks[n][1](accs.pop(n), sums.pop(n))


def _in_projection(x2, mod3, g, w_bf, tabs, gain_a, gain_b, gate_bias, seq):
    n, d = x2.shape
    tm = TOKEN_TILE
    tiles = seq // tm
    heads = jnp.arange(PROJ_CHUNK) // HEAD_DIM
    seg_ones = (heads[:, None] == heads[None, :]).astype(BF16)
    row = lambda i: (i, 0)
    fixed = lambda i: (0, 0)
    res_block = lambda i: (i // tiles, 0, i % tiles, 0)
    wide = [pl.BlockSpec((tm, WIDTH_A), row)] * 3
    wide_shape = [jax.ShapeDtypeStruct((n, WIDTH_A), BF16)] * 3
    grp = [pl.BlockSpec((1, dil, tm // dil, WIDTH_B), res_block) for _, dil in DILATED_GROUPS] * 3
    grp_shape = [jax.ShapeDtypeStruct((n // seq, dil, seq // dil, WIDTH_B), BF16) for _, dil in DILATED_GROUPS] * 3
    outs = pl.pallas_call(
        _inproj_kernel,
        grid=(n // tm,),
        in_specs=[pl.BlockSpec((tm, d), row),
                  pl.BlockSpec((1, 1, mod3.shape[2]), lambda i: (i // tiles, 0, 0)),
                  _resident((1, d), fixed),
                  _resident((d, IN_COLS), fixed),
                  pl.BlockSpec((tm, LANES), row),
                  pl.BlockSpec((tm, LANES), row),
                  pl.BlockSpec((tm, LANES), row),
                  _resident((2, LANES), fixed),
                  _resident((2, LANES), fixed),
                  _resident((1, GATE_COLS), fixed),
                  _resident((PROJ_CHUNK, PROJ_CHUNK), fixed)],
        out_specs=wide + grp + [pl.BlockSpec((tm, GATE_COLS), row)],
        out_shape=wide_shape + grp_shape + [jax.ShapeDtypeStruct((n, GATE_COLS), BF16)],
        scratch_shapes=[pltpu.VMEM((tm, d), BF16), pltpu.VMEM((WIDTH_B // LANES, tm, LANES), F32)],
        compiler_params=_params(("parallel",), 56),
        name="in_proj",
    )(x2, mod3, g, w_bf, *tabs, gain_a, gain_b, gate_bias, seg_ones)
    return outs[0], outs[1], outs[2], outs[3:6], outs[6:9], outs[9:12], outs[12]


DA_TQ = 256
DA_BLOCKS = 4
DA_TK = 512
DA_FINISH_AFTER = 1024
DA_ONES_ROWS = 16


def _diff_attn_kernel(lam_ref, g_ref, q_ref, k_ref, v_ref, o_ref, sa_scr, sb_scr, ma_scr, mb_scr,
                      acc_scr, vt_scr, *, nq, lam_init):
    seq = k_ref.shape[1]
    tq = DA_TQ
    t = pl.program_id(0)

    def load_values():
        vt_scr[0:VDIM_A, :] = v_ref[0].T
        vt_scr[VDIM_A:VDIM_A + DA_ONES_ROWS, :] = jnp.ones((DA_ONES_ROWS, seq), BF16)

    @pl.when(t == 0)
    def _():
        sb_scr[...] = jnp.zeros(sb_scr.shape, F32)
        mb_scr[...] = jnp.zeros(mb_scr.shape, F32)
        acc_scr[...] = jnp.ones(acc_scr.shape, F32)
        load_values()

    def finish(acc_r, rows):
        o = acc_r[0:VDIM_A, :] / acc_r[VDIM_A:VDIM_A + 1, :]
        lv = lam_ref[...]
        lam = (jnp.exp(jnp.sum(lv[0:1] * lv[1:2], axis=-1, keepdims=True))
               - jnp.exp(jnp.sum(lv[2:3] * lv[3:4], axis=-1, keepdims=True)) + lam_init)
        out = o[:, :tq] - lam * o[:, tq:]
        ms = jnp.mean(out * out, axis=0, keepdims=True)
        y = out * lax.rsqrt(ms + EPS) * g_ref[...] * (1.0 - lam_init)
        o_ref[0, rows, :] = y.T.astype(BF16)

    def step(rows, s_w, m_w, acc_w, s_r, m_r, acc_r):
        q = q_ref[0, rows, :]
        lane = lax.broadcasted_iota(jnp.int32, q.shape, 1)
        zero = jnp.zeros_like(q)
        qs = jnp.concatenate([jnp.where(lane < HEAD_DIM, q, zero),
                              jnp.where(lane < HEAD_DIM, zero, q)], axis=0)
        m_prev = m_r[0:1]
        m_new = None
        acc = None
        for c in range(0, seq, DA_TK):
            st = lax.dot_general(k_ref[0, c:c + DA_TK, :], qs, NT_DIMS, preferred_element_type=F32)
            s_w[c:c + DA_TK, :] = st
            mc = jnp.max(st, axis=0, keepdims=True)
            m_new = mc if m_new is None else jnp.maximum(m_new, mc)
            pt = jnp.exp2((s_r[c:c + DA_TK, :] - m_prev).astype(BF16))
            part = jnp.dot(vt_scr[:, c:c + DA_TK], pt, preferred_element_type=F32)
            acc = part if acc is None else acc + part
            if c == DA_FINISH_AFTER:
                finish(acc_r, rows)
        m_w[...] = jnp.broadcast_to(m_new, m_w.shape)
        acc_w[...] = acc

    score_bufs = ((sa_scr, ma_scr), (sb_scr, mb_scr))
    for j in range(DA_BLOCKS):
        (s_w, m_w), (s_r, m_r) = score_bufs[j % 2], score_bufs[1 - j % 2]
        step(slice(j * tq, (j + 1) * tq), s_w, m_w, acc_scr.at[(j - 1) % DA_BLOCKS], s_r, m_r, acc_scr.at[j])
        if j == 0:
            @pl.when((t > 0) & ((DA_BLOCKS * t) % nq == 0))
            def _():
                load_values()


def _diff_attention(qa, ka, va, lam_p, subln_g, lam_init):
    b, s, _ = qa.shape
    nq = s // DA_TQ
    assert nq % DA_BLOCKS == 0 and DA_BLOCKS % 2 == 0
    per_head = nq // DA_BLOCKS
    pairs = b * HEADS_A * per_head

    def pair(t):
        t = jnp.clip(t, 0, pairs - 1)
        return t // (HEADS_A * per_head), (t // per_head) % HEADS_A, t % per_head

    def q_map(t):
        bi, h, i = pair(t)
        return bi, i, h

    def kv_map(t):
        bi, h, _ = pair(t)
        return bi, 0, h

    def o_map(t):
        bi, h, i = pair(t - 1)
        return bi, i, h

    score_buf = pltpu.VMEM((s, 2 * DA_TQ), F32)
    max_buf = pltpu.VMEM((8, 2 * DA_TQ), F32)
    acc_buf = pltpu.VMEM((DA_BLOCKS, VDIM_A + DA_ONES_ROWS, 2 * DA_TQ), F32)
    return pl.pallas_call(
        partial(_diff_attn_kernel, nq=nq, lam_init=lam_init),
        grid=(pairs + 1,),
        in_specs=[pl.BlockSpec(lam_p.shape, lambda t: (0, 0)),
                  pl.BlockSpec((VDIM_A, 1), lambda t: (0, 0)),
                  pl.BlockSpec((1, DA_BLOCKS * DA_TQ, VDIM_A), q_map),
                  pl.BlockSpec((1, s, VDIM_A), kv_map),
                  pl.BlockSpec((1, s, VDIM_A), kv_map)],
        out_specs=pl.BlockSpec((1, DA_BLOCKS * DA_TQ, VDIM_A), o_map),
        out_shape=jax.ShapeDtypeStruct((b, s, WIDTH_A), BF16),
        scratch_shapes=[score_buf, score_buf, max_buf, max_buf, acc_buf,
                        pltpu.VMEM((VDIM_A + DA_ONES_ROWS, s), BF16)],
        compiler_params=_params(("arbitrary",), 40),
        name="diff_attn",
    )(lam_p, subln_g, qa, ka, va)


DIL_TQ = 128
DIL_SKEW = 1
DIL_WIN = 256


def _band_scores(q, kw):
    head = lax.broadcasted_iota(jnp.int32, q.shape, 1) // HEAD_DIM
    zero = jnp.zeros_like(q)
    qs = jnp.concatenate([jnp.where(head == h, q, zero) for h in range(HEADS_PER_GROUP_B)], axis=0)
    return lax.dot_general(qs, kw, NT_DIMS, preferred_element_type=F32)


def _band_softmax(s, rel, offset, side):
    s = jnp.where(jnp.abs(rel + offset) <= side, s, NEG_INF)
    m = jnp.max(s, axis=-1, keepdims=True)
    p = jnp.exp(s - m)
    l = jnp.sum(p, axis=-1, keepdims=True)
    return p.astype(BF16), m, l


def _band_output(p, m, l, vw, nq):
    o = jnp.dot(p, vw, preferred_element_type=F32) / l
    lse = jnp.broadcast_to(m + jnp.log(l), o.shape)
    head = lax.broadcasted_iota(jnp.int32, (nq, WIDTH_B), 1) // HEAD_DIM
    o_sel = jnp.zeros((nq, WIDTH_B), F32)
    lse_sel = jnp.zeros((nq, WIDTH_B), F32)
    for h in range(HEADS_PER_GROUP_B):
        rs = slice(h * nq, (h + 1) * nq)
        o_sel = jnp.where(head == h, o[rs], o_sel)
        lse_sel = jnp.where(head == h, lse[rs], lse_sel)
    return o_sel, lse_sel


def _dilated_kernel(*refs):
    q_refs, k_refs, v_refs = refs[0:3], refs[3:6], refs[6:9]
    ob_ref, o_scr, lse_scr = refs[9:12]
    tile = ob_ref.shape[0]
    i = pl.program_id(1)

    blocks = []
    for g, (window, dil) in enumerate(DILATED_GROUPS):
        side = window // (2 * dil)
        length = k_refs[g].shape[2]
        per_res = tile // dil
        nq = min(per_res, DIL_TQ)
        rows = lax.broadcasted_iota(jnp.int32, (HEADS_PER_GROUP_B * nq, DIL_WIN), 0)
        cols = lax.broadcasted_iota(jnp.int32, (HEADS_PER_GROUP_B * nq, DIL_WIN), 1)
        rel = (rows & (nq - 1)) - cols
        for r in range(dil):
            for u in range(per_res // nq):
                t0 = i * per_res + u * nq
                if length == DIL_WIN:
                    ks = 0
                else:
                    ks = pl.multiple_of(jnp.clip(t0 - side, 0, length - DIL_WIN), side)
                blocks.append(dict(g=g, dil=dil, r=r, u=u, nq=nq, t0=t0, ks=ks, side=side, rel=rel))

    def scores(bk):
        q = q_refs[bk["g"]][0, bk["r"], bk["u"] * bk["nq"]:(bk["u"] + 1) * bk["nq"], :]
        kw = k_refs[bk["g"]][0, bk["r"], pl.ds(bk["ks"], DIL_WIN), :]
        bk["s"] = _band_scores(q, kw)

    def softmax(bk):
        bk["pml"] = _band_softmax(bk.pop("s"), bk["rel"], bk["t0"] - bk["ks"], bk["side"])

    def output(bk):
        g, dil, r, u, nq = bk["g"], bk["dil"], bk["r"], bk["u"], bk["nq"]
        vw = v_refs[g][0, r, pl.ds(bk["ks"], DIL_WIN), :]
        o, lse = _band_output(*bk.pop("pml"), vw, nq)
        rows = pl.ds(u * nq * dil + r, nq, stride=dil) if dil > 1 else pl.ds(u * nq, nq)
        for j in range(WIDTH_B // LANES):
            o_scr[g, j, rows, :] = o[:, j * LANES:(j + 1) * LANES]
            lse_scr[g, j, rows, :] = lse[:, j * LANES:(j + 1) * LANES]

    for n in range(len(blocks) + 2 * DIL_SKEW):
        if n < len(blocks):
            scores(blocks[n])
        if 0 <= n - DIL_SKEW < len(blocks):
            softmax(blocks[n - DIL_SKEW])
        if 0 <= n - 2 * DIL_SKEW < len(blocks):
            output(blocks[n - 2 * DIL_SKEW])

    for j in range(WIDTH_B // LANES):
        l0, l1, l2 = lse_scr[0, j], lse_scr[1, j], lse_scr[2, j]
        m = jnp.maximum(jnp.maximum(l0, l1), l2)
        e0, e1, e2 = jnp.exp(l0 - m), jnp.exp(l1 - m), jnp.exp(l2 - m)
        ob = (e0 * o_scr[0, j] + e1 * o_scr[1, j] + e2 * o_scr[2, j]) / (e0 + e1 + e2)
        ob_ref[:, j * LANES:(j + 1) * LANES] = ob.astype(BF16)


def _dilated_attention(qs, ks, vs):
    b = qs[0].shape[0]
    seq = qs[0].shape[1] * qs[0].shape[2]
    tile = TOKEN_TILE
    tiles = seq // tile
    q_specs = [pl.BlockSpec((1, dil, tile // dil, WIDTH_B), lambda bi, i: (bi, 0, i, 0))
               for _, dil in DILATED_GROUPS]
    kv_specs = [pl.BlockSpec((1, dil, seq // dil, WIDTH_B), lambda bi, i: (bi, 0, 0, 0),
                             pipeline_mode=pl.Buffered(1))
                for _, dil in DILATED_GROUPS]
    return pl.pallas_call(
        _dilated_kernel,
        grid=(b, tiles),
        in_specs=q_specs + kv_specs + kv_specs,
        out_specs=pl.BlockSpec((tile, WIDTH_B), lambda bi, i: (bi * tiles + i, 0)),
        out_shape=jax.ShapeDtypeStruct((b * seq, WIDTH_B), BF16),
        scratch_shapes=[pltpu.VMEM((N_GROUPS_B, WIDTH_B // LANES, tile, LANES), F32)] * 2,
        compiler_params=_params(("parallel", "arbitrary"), 48),
        name="dilated_attn",
    )(*qs, *ks, *vs)


def _post_kernel(x_ref, mod_ref, oa_ref, ob_ref, gates_ref, wa_ref, wb_ref, wo_ref, g_ref, wu_ref, wd_ref,
                 out_ref):
    br_a = jnp.dot(oa_ref[...], wa_ref[...], preferred_element_type=F32)
    br_b = jnp.dot(ob_ref[...], wb_ref[...], preferred_element_type=F32)
    g_a = gates_ref[:, 0:D_MODEL].astype(F32)
    g_b = gates_ref[:, D_MODEL:GATE_COLS].astype(F32)
    y = jnp.dot((g_a * br_a + g_b * br_b).astype(BF16), wo_ref[...], preferred_element_type=F32)
    mod = lambda k: mod_ref[0, :, k * D_MODEL:(k + 1) * D_MODEL]
    x = x_ref[...] + mod(2) * y

    ms = jnp.mean(x * x, axis=-1, keepdims=True)
    h = (x * lax.rsqrt(ms + EPS) * g_ref[...] * (1.0 + mod(4)) + mod(3)).astype(BF16)
    u = jnp.maximum(jnp.dot(h, wu_ref[...], preferred_element_type=F32), 0.0)
    dn = jnp.dot((u * u).astype(BF16), wd_ref[...], preferred_element_type=F32)
    out_ref[...] = x + mod(5) * dn


def _post_attention(x2, mod3, oa, ob, gates, wa, wb, wo, g, wu, wd, seq):
    n, d = x2.shape
    tm = TOKEN_TILE
    row = lambda i: (i, 0)
    fixed = lambda i: (0, 0)
    return pl.pallas_call(
        _post_kernel,
        grid=(n // tm,),
        in_specs=[pl.BlockSpec((tm, d), row),
                  pl.BlockSpec((1, 1, mod3.shape[2]), lambda i: ((i * tm) // seq, 0, 0)),
                  pl.BlockSpec((tm, WIDTH_A), row),
                  pl.BlockSpec((tm, WIDTH_B), row),
                  pl.BlockSpec((tm, GATE_COLS), row),
                  _resident(wa.shape, fixed), _resident(wb.shape, fixed), _resident(wo.shape, fixed),
                  _resident((1, d), fixed), _resident(wu.shape, fixed), _resident(wd.shape, fixed)],
        out_specs=pl.BlockSpec((tm, d), row),
        out_shape=jax.ShapeDtypeStruct((n, d), F32),
        compiler_params=_params(("parallel",), 60),
        name="post_attn",
    )(x2, mod3, oa, ob, gates, wa, wb, wo, g, wu, wd)


def kernel(x, c, positions, ada_w, ada_b, norm_mix_g, norm_mlp_g, w_in, qk_gain_a, lambda_a, subln_g_a,
           qk_gain_b, w_branch_a, w_branch_b, gate_bias, w_out, w_mlp_up, w_mlp_down):
    b, s, d = x.shape
    depth = ada_w.shape[0]
    tabs = _rope_tables(positions)
    mod = _ada_mod(c, ada_w, ada_b)
    rep = LANES // HEAD_DIM
    x2 = x.reshape(b * s, d)
    for l in range(depth):
        mod3 = mod[l].reshape(b, 1, 6 * d)
        qa, ka, va, qbs, kbs, vbs, gates = _in_projection(
            x2, mod3, norm_mix_g[l].reshape(1, d), w_in[l].astype(BF16), tabs,
            jnp.tile(qk_gain_a[l], (1, rep)), jnp.tile(qk_gain_b[l], (1, rep)),
            gate_bias[l].reshape(1, GATE_COLS), s)

        lam_init = 0.8 - 0.6 * math.exp(-0.3 * l)
        oa = _diff_attention(qa.reshape(b, s, WIDTH_A), ka.reshape(b, s, WIDTH_A),
                             va.reshape(b, s, WIDTH_A), lambda_a[l], subln_g_a[l].reshape(VDIM_A, 1),
                             lam_init)
        ob = _dilated_attention(qbs, kbs, vbs)

        x2 = _post_attention(x2, mod3, oa.reshape(b * s, WIDTH_A), ob, gates,
                             w_branch_a[l].astype(BF16), w_branch_b[l].astype(BF16), w_out[l].astype(BF16),
                             norm_mlp_g[l].reshape(1, d), w_mlp_up[l].astype(BF16),
                             w_mlp_down[l].astype(BF16), s)
    return x2.reshape(b, s, d)
```

```python
import math
from functools import partial

import jax
import jax.numpy as jnp
from jax import lax
from jax.experimental import pallas as pl
from jax.experimental.pallas import tpu as pltpu

D_MODEL = 1024
HEAD_DIM = 64
HEADS_A = 8
VDIM_A = 2 * HEAD_DIM
WIDTH_A = HEADS_A * VDIM_A
DILATED_GROUPS = ((128, 1), (512, 4), (2048, 16))
N_GROUPS_B = 3
HEADS_PER_GROUP_B = 4
WIDTH_B = HEADS_PER_GROUP_B * HEAD_DIM
QKV_B_COLS = N_GROUPS_B * WIDTH_B
D_FF = 4 * D_MODEL
ROPE_THETA = 500000.0
ROT_HALF = HEAD_DIM // 4 // 2
EPS = 1e-6
NEG_INF = -1e30
GATE_COLS = 2 * D_MODEL
IN_COLS = 3 * WIDTH_A + 3 * QKV_B_COLS + GATE_COLS
SM_SCALE = HEAD_DIM ** -0.5
LOG2E = math.log2(math.e)

LANES = 128
PROJ_CHUNK = 256
TOKEN_TILE = 512
VMEM_BYTES = 64 * 1024 * 1024
MIB = 1024 * 1024

BF16 = jnp.bfloat16
F32 = jnp.float32
NT_DIMS = (((1,), (1,)), ((), ()))


def _params(semantics, vmem_mib):
    return pltpu.CompilerParams(dimension_semantics=semantics,
                                vmem_limit_bytes=min(vmem_mib * MIB, VMEM_BYTES - 4 * MIB))


def _resident(shape, index_map):
    return pl.BlockSpec(shape, index_map, pipeline_mode=pl.Buffered(1))


def _rope_kernel(pos_ref, cos_ref, sin_ref):
    pos = pos_ref[...].astype(F32)
    for f in range(ROT_HALF):
        ang = pos * (ROPE_THETA ** (-(2.0 * f) / (2 * ROT_HALF)))
        cos_ref[f] = jnp.cos(ang)
        sin_ref[f] = jnp.sin(ang)


def _rope_tables(positions):
    n = positions.size
    pos2 = positions.reshape(n // LANES, LANES)
    cos, sin = pl.pallas_call(
        _rope_kernel,
        out_shape=[jax.ShapeDtypeStruct((ROT_HALF, n // LANES, LANES), F32)] * 2,
    )(pos2)
    cos = cos.reshape(ROT_HALF, n).T
    sin = sin.reshape(ROT_HALF, n).T
    one = jnp.ones((n, HEAD_DIM - 2 * ROT_HALF), F32)
    zero = jnp.zeros((n, HEAD_DIM - 2 * ROT_HALF), F32)
    z8 = jnp.zeros((n, ROT_HALF), F32)
    c_tab = jnp.concatenate([cos, cos, one], axis=1)
    s1_tab = jnp.concatenate([z8, sin, zero], axis=1)
    s2_tab = jnp.concatenate([sin, z8, zero], axis=1)
    rep = LANES // HEAD_DIM
    return tuple(jnp.tile(t, (1, rep)) for t in (c_tab, s1_tab, s2_tab))


def _ada_kernel(c_ref, w_ref, b_ref, o_ref):
    c = c_ref[...]
    cond = (c * jax.nn.sigmoid(c)).astype(BF16)
    o_ref[0] = jnp.dot(cond, w_ref[0].astype(BF16), preferred_element_type=F32) + b_ref[0]


def _ada_mod(c, ada_w, ada_b):
    depth, d, n = ada_w.shape
    b = c.shape[0]
    tn = 768
    return pl.pallas_call(
        _ada_kernel,
        grid=(depth, n // tn),
        in_specs=[pl.BlockSpec((b, d), lambda l, j: (0, 0)),
                  pl.BlockSpec((1, d, tn), lambda l, j: (l, 0, j)),
                  pl.BlockSpec((1, 1, tn), lambda l, j: (l, 0, j))],
        out_specs=pl.BlockSpec((1, b, tn), lambda l, j: (l, 0, j)),
        out_shape=jax.ShapeDtypeStruct((depth, b, n), F32),
        compiler_params=_params(("parallel", "parallel"), 32),
    )(c, ada_w, ada_b.reshape(depth, 1, n))


def _head_norm_rope(acc, ssum, gain, c_tab, s1_tab, s2_tab):
    y = acc * lax.rsqrt(ssum * (1.0 / HEAD_DIM) + EPS)
    halves = []
    for j in range(0, acc.shape[1], LANES):
        yj = y[:, j:j + LANES] * gain
        halves.append(yj * c_tab + pltpu.roll(yj, ROT_HALF, 1) * s1_tab
                      - pltpu.roll(yj, LANES - ROT_HALF, 1) * s2_tab)
    return halves


def _inproj_kernel(x_ref, mod_ref, g_ref, w_ref, c_ref, s1_ref, s2_ref, gain_a_ref, gain_b_ref,
                   gbias_ref, ones_ref, qa_ref, ka_ref, va_ref, *rest):
    qb_refs, kb_refs, vb_refs = rest[0:3], rest[3:6], rest[6:9]
    gates_ref, h_scr, y_scr = rest[9:12]
    tm = x_ref.shape[0]
    x = x_ref[...]
    ms = jnp.mean(x * x, axis=-1, keepdims=True)
    shift = mod_ref[0, :, 0:D_MODEL]
    scale = mod_ref[0, :, D_MODEL:2 * D_MODEL]
    h = x * lax.rsqrt(ms + EPS) * g_ref[...] * (1.0 + scale) + shift
    h_scr[...] = h.astype(BF16)

    c_tab, s1_tab, s2_tab = c_ref[...], s1_ref[...], s2_ref[...]
    cw = PROJ_CHUNK

    def split_residues(out_ref, dilation):
        for r in range(dilation):
            for j in range(WIDTH_B // LANES):
                out_ref[0, r, :, j * LANES:(j + 1) * LANES] = (
                    y_scr[j, pl.ds(r, tm // dilation, stride=dilation), :].astype(BF16))

    tasks = []

    def normed(out_ref, width, gain):
        def epilogue(acc, ssum, c):
            for j, half in enumerate(_head_norm_rope(acc, ssum, gain, c_tab, s1_tab, s2_tab)):
                out_ref[:, c + j * LANES:c + (j + 1) * LANES] = half.astype(BF16)
        tasks.extend((True, partial(epilogue, c=c)) for c in range(0, width, cw))

    def plain(out_ref, width):
        def epilogue(acc, ssum, c):
            out_ref[:, c:c + cw] = acc.astype(BF16)
        tasks.extend((False, partial(epilogue, c=c)) for c in range(0, width, cw))

    def normed_groups(out_refs, gain):
        def epilogue(acc, ssum, g):
            for j, half in enumerate(_head_norm_rope(acc, ssum, gain, c_tab, s1_tab, s2_tab)):
                y_scr[j] = half
            split_residues(out_refs[g], DILATED_GROUPS[g][1])
        tasks.extend((True, partial(epilogue, g=g)) for g in range(N_GROUPS_B))

    def plain_groups(out_refs):
        def epilogue(acc, ssum, g):
            for j in range(0, WIDTH_B, LANES):
                y_scr[j // LANES] = acc[:, j:j + LANES]
            split_residues(out_refs[g], DILATED_GROUPS[g][1])
        tasks.extend((False, partial(epilogue, g=g)) for g in range(N_GROUPS_B))

    def gate(acc, ssum, c):
        gates_ref[:, c:c + cw] = jax.nn.sigmoid(acc + gbias_ref[:, c:c + cw]).astype(BF16)

    normed(qa_ref, WIDTH_A, gain_a_ref[0:1, :] * (SM_SCALE * LOG2E))
    normed(ka_ref, WIDTH_A, gain_a_ref[1:2, :])
    plain(va_ref, WIDTH_A)
    normed_groups(qb_refs, gain_b_ref[0:1, :] * SM_SCALE)
    normed_groups(kb_refs, gain_b_ref[1:2, :])
    plain_groups(vb_refs)
    tasks.extend((False, partial(gate, c=c)) for c in range(0, GATE_COLS, cw))
    assert len(tasks) * cw == IN_COLS and WIDTH_B == cw

    def proj(n):
        return jnp.dot(h_scr[...], w_ref[:, n * cw:(n + 1) * cw], preferred_element_type=F32)

    def head_sums(n, acc):
        if not tasks[n][0]:
            return None
        return jnp.dot((acc * acc).astype(BF16), ones_ref[...], preferred_element_type=F32)

    last = len(tasks) - 1
    accs = {0: proj(0), 1: proj(1)}
    sums = {0: head_sums(0, accs[0])}
    for n in range(len(tasks)):
        if n + 2 <= last:
            accs[n + 2] = proj(n + 2)
        if n + 1 <= last:
            sums[n + 1] = head_sums(n + 1, accs[n + 1])
        tasks[n][1](accs.pop(n), sums.pop(n))


def _in_projection(x2, mod3, g, w_bf, tabs, gain_a, gain_b, gate_bias, seq):
    n, d = x2.shape
    tm = TOKEN_TILE
    tiles = seq // tm
    heads = jnp.arange(PROJ_CHUNK) // HEAD_DIM
    seg_ones = (heads[:, None] == heads[None, :]).astype(BF16)
    row = lambda i: (i, 0)
    fixed = lambda i: (0, 0)
    res_block = lambda i: (i // tiles, 0, i % tiles, 0)
    wide = [pl.BlockSpec((tm, WIDTH_A), row)] * 3
    wide_shape = [jax.ShapeDtypeStruct((n, WIDTH_A), BF16)] * 3
    grp = [pl.BlockSpec((1, dil, tm // dil, WIDTH_B), res_block) for _, dil in DILATED_GROUPS] * 3
    grp_shape = [jax.ShapeDtypeStruct((n // seq, dil, seq // dil, WIDTH_B), BF16) for _, dil in DILATED_GROUPS] * 3
    outs = pl.pallas_call(
        _inproj_kernel,
        grid=(n // tm,),
        in_specs=[pl.BlockSpec((tm, d), row),
                  pl.BlockSpec((1, 1, mod3.shape[2]), lambda i: (i // tiles, 0, 0)),
                  _resident((1, d), fixed),
                  _resident((d, IN_COLS), fixed),
                  pl.BlockSpec((tm, LANES), row),
                  pl.BlockSpec((tm, LANES), row),
                  pl.BlockSpec((tm, LANES), row),
                  _resident((2, LANES), fixed),
                  _resident((2, LANES), fixed),
                  _resident((1, GATE_COLS), fixed),
                  _resident((PROJ_CHUNK, PROJ_CHUNK), fixed)],
        out_specs=wide + grp + [pl.BlockSpec((tm, GATE_COLS), row)],
        out_shape=wide_shape + grp_shape + [jax.ShapeDtypeStruct((n, GATE_COLS), BF16)],
        scratch_shapes=[pltpu.VMEM((tm, d), BF16), pltpu.VMEM((WIDTH_B // LANES, tm, LANES), F32)],
        compiler_params=_params(("parallel",), 56),
        name="in_proj",
    )(x2, mod3, g, w_bf, *tabs, gain_a, gain_b, gate_bias, seg_ones)
    return outs[0], outs[1], outs[2], outs[3:6], outs[6:9], outs[9:12], outs[12]


DA_TQ = 256
DA_BLOCKS = 2
DA_TK = 512
DA_FINISH_AFTER = 1024
DA_ONES_ROWS = 16


def _diff_attn_kernel(lam_ref, g_ref, q_ref, k_ref, v_ref, o_ref, sa_scr, sb_scr, ma_scr, mb_scr,
                      acc_scr, vt_scr, *, nq, lam_init):
    seq = k_ref.shape[1]
    tq = DA_TQ
    t = pl.program_id(0)

    def load_values():
        vt_scr[0:VDIM_A, :] = v_ref[0].T
        vt_scr[VDIM_A:VDIM_A + DA_ONES_ROWS, :] = jnp.ones((DA_ONES_ROWS, seq), BF16)

    @pl.when(t == 0)
    def _():
        sb_scr[...] = jnp.zeros(sb_scr.shape, F32)
        mb_scr[...] = jnp.zeros(mb_scr.shape, F32)
        acc_scr[...] = jnp.ones(acc_scr.shape, F32)
        load_values()

    def finish(acc_r, rows):
        o = acc_r[0:VDIM_A, :] / acc_r[VDIM_A:VDIM_A + 1, :]
        lv = lam_ref[...]
        lam = (jnp.exp(jnp.sum(lv[0:1] * lv[1:2], axis=-1, keepdims=True))
               - jnp.exp(jnp.sum(lv[2:3] * lv[3:4], axis=-1, keepdims=True)) + lam_init)
        out = o[:, :tq] - lam * o[:, tq:]
        ms = jnp.mean(out * out, axis=0, keepdims=True)
        y = out * lax.rsqrt(ms + EPS) * g_ref[...] * (1.0 - lam_init)
        o_ref[0, rows, :] = y.T.astype(BF16)

    def step(rows, s_w, m_w, acc_w, s_r, m_r, acc_r):
        q = q_ref[0, rows, :]
        lane = lax.broadcasted_iota(jnp.int32, q.shape, 1)
        zero = jnp.zeros_like(q)
        qs = jnp.concatenate([jnp.where(lane < HEAD_DIM, q, zero),
                              jnp.where(lane < HEAD_DIM, zero, q)], axis=0)
        m_prev = m_r[0:1]
        m_new = None
        acc = None
        for c in range(0, seq, DA_TK):
            st = lax.dot_general(k_ref[0, c:c + DA_TK, :], qs, NT_DIMS, preferred_element_type=F32)
            s_w[c:c + DA_TK, :] = st
            mc = jnp.max(st, axis=0, keepdims=True)
            m_new = mc if m_new is None else jnp.maximum(m_new, mc)
            pt = jnp.exp2((s_r[c:c + DA_TK, :] - m_prev).astype(BF16))
            part = jnp.dot(vt_scr[:, c:c + DA_TK], pt, preferred_element_type=F32)
            acc = part if acc is None else acc + part
            if c == DA_FINISH_AFTER:
                finish(acc_r, rows)
        m_w[...] = jnp.broadcast_to(m_new, m_w.shape)
        acc_w[...] = acc

    score_bufs = ((sa_scr, ma_scr), (sb_scr, mb_scr))
    for j in range(DA_BLOCKS):
        (s_w, m_w), (s_r, m_r) = score_bufs[j % 2], score_bufs[1 - j % 2]
        step(slice(j * tq, (j + 1) * tq), s_w, m_w, acc_scr.at[(j - 1) % DA_BLOCKS], s_r, m_r, acc_scr.at[j])
        if j == 0:
            @pl.when((t > 0) & ((DA_BLOCKS * t) % nq == 0))
            def _():
                load_values()


def _diff_attention(qa, ka, va, lam_p, subln_g, lam_init):
    b, s, _ = qa.shape
    nq = s // DA_TQ
    assert nq % DA_BLOCKS == 0 and DA_BLOCKS % 2 == 0
    per_head = nq // DA_BLOCKS
    pairs = b * HEADS_A * per_head

    def pair(t):
        t = jnp.clip(t, 0, pairs - 1)
        return t // (HEADS_A * per_head), (t // per_head) % HEADS_A, t % per_head

    def q_map(t):
        bi, h, i = pair(t)
        return bi, i, h

    def kv_map(t):
        bi, h, _ = pair(t)
        return bi, 0, h

    def o_map(t):
        bi, h, i = pair(t - 1)
        return bi, i, h

    score_buf = pltpu.VMEM((s, 2 * DA_TQ), F32)
    max_buf = pltpu.VMEM((8, 2 * DA_TQ), F32)
    acc_buf = pltpu.VMEM((DA_BLOCKS, VDIM_A + DA_ONES_ROWS, 2 * DA_TQ), F32)
    return pl.pallas_call(
        partial(_diff_attn_kernel, nq=nq, lam_init=lam_init),
        grid=(pairs + 1,),
        in_specs=[pl.BlockSpec(lam_p.shape, lambda t: (0, 0)),
                  pl.BlockSpec((VDIM_A, 1), lambda t: (0, 0)),
                  pl.BlockSpec((1, DA_BLOCKS * DA_TQ, VDIM_A), q_map),
                  pl.BlockSpec((1, s, VDIM_A), kv_map),
                  pl.BlockSpec((1, s, VDIM_A), kv_map)],
        out_specs=pl.BlockSpec((1, DA_BLOCKS * DA_TQ, VDIM_A), o_map),
        out_shape=jax.ShapeDtypeStruct((b, s, WIDTH_A), BF16),
        scratch_shapes=[score_buf, score_buf, max_buf, max_buf, acc_buf,
                        pltpu.VMEM((VDIM_A + DA_ONES_ROWS, s), BF16)],
        compiler_params=_params(("arbitrary",), 40),
        name="diff_attn",
    )(lam_p, subln_g, qa, ka, va)


DIL_TQ = 128
DIL_SKEW = 1
DIL_WIN = 256


def _band_scores(q, kw):
    head = lax.broadcasted_iota(jnp.int32, q.shape, 1) // HEAD_DIM
    zero = jnp.zeros_like(q)
    qs = jnp.concatenate([jnp.where(head == h, q, zero) for h in range(HEADS_PER_GROUP_B)], axis=0)
    return lax.dot_general(qs, kw, NT_DIMS, preferred_element_type=F32)


def _band_softmax(s, rel, offset, side):
    s = jnp.where(jnp.abs(rel + offset) <= side, s, NEG_INF)
    m = jnp.max(s, axis=-1, keepdims=True)
    p = jnp.exp(s - m)
    l = jnp.sum(p, axis=-1, keepdims=True)
    return p.astype(BF16), m, l


def _band_output(p, m, l, vw, nq):
    o = jnp.dot(p, vw, preferred_element_type=F32) / l
    lse = jnp.broadcast_to(m + jnp.log(l), o.shape)
    head = lax.broadcasted_iota(jnp.int32, (nq, WIDTH_B), 1) // HEAD_DIM
    o_sel = jnp.zeros((nq, WIDTH_B), F32)
    lse_sel = jnp.zeros((nq, WIDTH_B), F32)
    for h in range(HEADS_PER_GROUP_B):
        rs = slice(h * nq, (h + 1) * nq)
        o_sel = jnp.where(head == h, o[rs], o_sel)
        lse_sel = jnp.where(head == h, lse[rs], lse_sel)
    return o_sel, lse_sel


def _dilated_kernel(*refs):
    q_refs, k_refs, v_refs = refs[0:3], refs[3:6], refs[6:9]
    ob_ref, o_scr, lse_scr = refs[9:12]
    tile = ob_ref.shape[0]
    i = pl.program_id(1)

    blocks = []
    for g, (window, dil) in enumerate(DILATED_GROUPS):
        side = window // (2 * dil)
        length = k_refs[g].shape[2]
        per_res = tile // dil
        nq = min(per_res, DIL_TQ)
        rows = lax.broadcasted_iota(jnp.int32, (HEADS_PER_GROUP_B * nq, DIL_WIN), 0)
        cols = lax.broadcasted_iota(jnp.int32, (HEADS_PER_GROUP_B * nq, DIL_WIN), 1)
        rel = (rows & (nq - 1)) - cols
        for r in range(dil):
            for u in range(per_res // nq):
                t0 = i * per_res + u * nq
                if length == DIL_WIN:
                    ks = 0
                else:
                    ks = pl.multiple_of(jnp.clip(t0 - side, 0, length - DIL_WIN), side)
                blocks.append(dict(g=g, dil=dil, r=r, u=u, nq=nq, t0=t0, ks=ks, side=side, rel=rel))

    def scores(bk):
        q = q_refs[bk["g"]][0, bk["r"], bk["u"] * bk["nq"]:(bk["u"] + 1) * bk["nq"], :]
        kw = k_refs[bk["g"]][0, bk["r"], pl.ds(bk["ks"], DIL_WIN), :]
        bk["s"] = _band_scores(q, kw)

    def softmax(bk):
        bk["pml"] = _band_softmax(bk.pop("s"), bk["rel"], bk["t0"] - bk["ks"], bk["side"])

    def output(bk):
        g, dil, r, u, nq = bk["g"], bk["dil"], bk["r"], bk["u"], bk["nq"]
        vw = v_refs[g][0, r, pl.ds(bk["ks"], DIL_WIN), :]
        o, lse = _band_output(*bk.pop("pml"), vw, nq)
        rows = pl.ds(u * nq * dil + r, nq, stride=dil) if dil > 1 else pl.ds(u * nq, nq)
        for j in range(WIDTH_B // LANES):
            o_scr[g, j, rows, :] = o[:, j * LANES:(j + 1) * LANES]
            lse_scr[g, j, rows, :] = lse[:, j * LANES:(j + 1) * LANES]

    for n in range(len(blocks) + 2 * DIL_SKEW):
        if n < len(blocks):
            scores(blocks[n])
        if 0 <= n - DIL_SKEW < len(blocks):
            softmax(blocks[n - DIL_SKEW])
        if 0 <= n - 2 * DIL_SKEW < len(blocks):
            output(blocks[n - 2 * DIL_SKEW])

    for j in range(WIDTH_B // LANES):
        l0, l1, l2 = lse_scr[0, j], lse_scr[1, j], lse_scr[2, j]
        m = jnp.maximum(jnp.maximum(l0, l1), l2)
        e0, e1, e2 = jnp.exp(l0 - m), jnp.exp(l1 - m), jnp.exp(l2 - m)
        ob = (e0 * o_scr[0, j] + e1 * o_scr[1, j] + e2 * o_scr[2, j]) / (e0 + e1 + e2)
        ob_ref[:, j * LANES:(j + 1) * LANES] = ob.astype(BF16)


def _dilated_attention(qs, ks, vs):
    b = qs[0].shape[0]
    seq = qs[0].shape[1] * qs[0].shape[2]
    tile = TOKEN_TILE
    tiles = seq // tile
    q_specs = [pl.BlockSpec((1, dil, tile // dil, WIDTH_B), lambda bi, i: (bi, 0, i, 0))
               for _, dil in DILATED_GROUPS]
    kv_specs = [pl.BlockSpec((1, dil, seq // dil, WIDTH_B), lambda bi, i: (bi, 0, 0, 0),
                             pipeline_mode=pl.Buffered(1))
                for _, dil in DILATED_GROUPS]
    return pl.pallas_call(
        _dilated_kernel,
        grid=(b, tiles),
        in_specs=q_specs + kv_specs + kv_specs,
        out_specs=pl.BlockSpec((tile, WIDTH_B), lambda bi, i: (bi * tiles + i, 0)),
        out_shape=jax.ShapeDtypeStruct((b * seq, WIDTH_B), BF16),
        scratch_shapes=[pltpu.VMEM((N_GROUPS_B, WIDTH_B // LANES, tile, LANES), F32)] * 2,
        compiler_params=_params(("parallel", "arbitrary"), 48),
        name="dilated_attn",
    )(*qs, *ks, *vs)


def _post_kernel(x_ref, mod_ref, oa_ref, ob_ref, gates_ref, wa_ref, wb_ref, wo_ref, g_ref, wu_ref, wd_ref,
                 out_ref):
    br_a = jnp.dot(oa_ref[...], wa_ref[...], preferred_element_type=F32)
    br_b = jnp.dot(ob_ref[...], wb_ref[...], preferred_element_type=F32)
    g_a = gates_ref[:, 0:D_MODEL].astype(F32)
    g_b = gates_ref[:, D_MODEL:GATE_COLS].astype(F32)
    y = jnp.dot((g_a * br_a + g_b * br_b).astype(BF16), wo_ref[...], preferred_element_type=F32)
    mod = lambda k: mod_ref[0, :, k * D_MODEL:(k + 1) * D_MODEL]
    x = x_ref[...] + mod(2) * y

    ms = jnp.mean(x * x, axis=-1, keepdims=True)
    h = (x * lax.rsqrt(ms + EPS) * g_ref[...] * (1.0 + mod(4)) + mod(3)).astype(BF16)
    u = jnp.maximum(jnp.dot(h, wu_ref[...], preferred_element_type=F32), 0.0)
    dn = jnp.dot((u * u).astype(BF16), wd_ref[...], preferred_element_type=F32)
    out_ref[...] = x + mod(5) * dn


def _post_attention(x2, mod3, oa, ob, gates, wa, wb, wo, g, wu, wd, seq):
    n, d = x2.shape
    tm = TOKEN_TILE
    row = lambda i: (i, 0)
    fixed = lambda i: (0, 0)
    return pl.pallas_call(
        _post_kernel,
        grid=(n // tm,),
        in_specs=[pl.BlockSpec((tm, d), row),
                  pl.BlockSpec((1, 1, mod3.shape[2]), lambda i: ((i * tm) // seq, 0, 0)),
                  pl.BlockSpec((tm, WIDTH_A), row),
                  pl.BlockSpec((tm, WIDTH_B), row),
                  pl.BlockSpec((tm, GATE_COLS), row),
                  _resident(wa.shape, fixed), _resident(wb.shape, fixed), _resident(wo.shape, fixed),
                  _resident((1, d), fixed), _resident(wu.shape, fixed), _resident(wd.shape, fixed)],
        out_specs=pl.BlockSpec((tm, d), row),
        out_shape=jax.ShapeDtypeStruct((n, d), F32),
        compiler_params=_params(("parallel",), 60),
        name="post_attn",
    )(x2, mod3, oa, ob, gates, wa, wb, wo, g, wu, wd)


def kernel(x, c, positions, ada_w, ada_b, norm_mix_g, norm_mlp_g, w_in, qk_gain_a, lambda_a, subln_g_a,
           qk_gain_b, w_branch_a, w_branch_b, gate_bias, w_out, w_mlp_up, w_mlp_down):
    b, s, d = x.shape
    depth = ada_w.shape[0]
    tabs = _rope_tables(positions)
    mod = _ada_mod(c, ada_w, ada_b)
    rep = LANES // HEAD_DIM
    x2 = x.reshape(b * s, d)
    for l in range(depth):
        mod3 = mod[l].reshape(b, 1, 6 * d)
        qa, ka, va, qbs, kbs, vbs, gates = _in_projection(
            x2, mod3, norm_mix_g[l].reshape(1, d), w_in[l].astype(BF16), tabs,
            jnp.tile(qk_gain_a[l], (1, rep)), jnp.tile(qk_gain_b[l], (1, rep)),
            gate_bias[l].reshape(1, GATE_COLS), s)

        lam_init = 0.8 - 0.6 * math.exp(-0.3 * l)
        oa = _diff_attention(qa.reshape(b, s, WIDTH_A), ka.reshape(b, s, WIDTH_A),
                             va.reshape(b, s, WIDTH_A), lambda_a[l], subln_g_a[l].reshape(VDIM_A, 1),
                             lam_init)
        ob = _dilated_attention(qbs, kbs, vbs)

        x2 = _post_attention(x2, mod3, oa.reshape(b * s, WIDTH_A), ob, gates,
                             w_branch_a[l].astype(BF16), w_branch_b[l].astype(BF16), w_out[l].astype(BF16),
                             norm_mlp_g[l].reshape(1, d), w_mlp_up[l].astype(BF16),
                             w_mlp_down[l].astype(BF16), s)
    return x2.reshape(b, s, d)
```

```python
import math
from functools import partial

import jax
import jax.numpy as jnp
from jax import lax
from jax.experimental import pallas as pl
from jax.experimental.pallas import tpu as pltpu

D_MODEL = 1024
HEAD_DIM = 64
HEADS_A = 8
VDIM_A = 2 * HEAD_DIM
WIDTH_A = HEADS_A * VDIM_A
DILATED_GROUPS = ((128, 1), (512, 4), (2048, 16))
N_GROUPS_B = 3
HEADS_PER_GROUP_B = 4
WIDTH_B = HEADS_PER_GROUP_B * HEAD_DIM
QKV_B_COLS = N_GROUPS_B * WIDTH_B
D_FF = 4 * D_MODEL
ROPE_THETA = 500000.0
ROT_HALF = HEAD_DIM // 4 // 2
EPS = 1e-6
NEG_INF = -1e30
GATE_COLS = 2 * D_MODEL
IN_COLS = 3 * WIDTH_A + 3 * QKV_B_COLS + GATE_COLS
SM_SCALE = HEAD_DIM ** -0.5
LOG2E = math.log2(math.e)

LANES = 128
PROJ_CHUNK = 256
TOKEN_TILE = 512
VMEM_BYTES = 64 * 1024 * 1024
MIB = 1024 * 1024

BF16 = jnp.bfloat16
F32 = jnp.float32
NT_DIMS = (((1,), (1,)), ((), ()))


def _params(semantics, vmem_mib):
    return pltpu.CompilerParams(dimension_semantics=semantics,
                                vmem_limit_bytes=min(vmem_mib * MIB, VMEM_BYTES - 4 * MIB))


def _resident(shape, index_map):
    return pl.BlockSpec(shape, index_map, pipeline_mode=pl.Buffered(1))


def _rope_kernel(pos_ref, cos_ref, sin_ref):
    pos = pos_ref[...].astype(F32)
    for f in range(ROT_HALF):
        ang = pos * (ROPE_THETA ** (-(2.0 * f) / (2 * ROT_HALF)))
        cos_ref[f] = jnp.cos(ang)
        sin_ref[f] = jnp.sin(ang)


def _rope_tables(positions):
    n = positions.size
    pos2 = positions.reshape(n // LANES, LANES)
    cos, sin = pl.pallas_call(
        _rope_kernel,
        out_shape=[jax.ShapeDtypeStruct((ROT_HALF, n // LANES, LANES), F32)] * 2,
    )(pos2)
    cos = cos.reshape(ROT_HALF, n).T
    sin = sin.reshape(ROT_HALF, n).T
    one = jnp.ones((n, HEAD_DIM - 2 * ROT_HALF), F32)
    zero = jnp.zeros((n, HEAD_DIM - 2 * ROT_HALF), F32)
    z8 = jnp.zeros((n, ROT_HALF), F32)
    c_tab = jnp.concatenate([cos, cos, one], axis=1)
    s1_tab = jnp.concatenate([z8, sin, zero], axis=1)
    s2_tab = jnp.concatenate([sin, z8, zero], axis=1)
    rep = LANES // HEAD_DIM
    return tuple(jnp.tile(t, (1, rep)) for t in (c_tab, s1_tab, s2_tab))


def _ada_kernel(c_ref, w_ref, b_ref, o_ref):
    c = c_ref[...]
    cond = (c * jax.nn.sigmoid(c)).astype(BF16)
    o_ref[0] = jnp.dot(cond, w_ref[0].astype(BF16), preferred_element_type=F32) + b_ref[0]


def _ada_mod(c, ada_w, ada_b):
    depth, d, n = ada_w.shape
    b = c.shape[0]
    tn = 768
    return pl.pallas_call(
        _ada_kernel,
        grid=(depth, n // tn),
        in_specs=[pl.BlockSpec((b, d), lambda l, j: (0, 0)),
                  pl.BlockSpec((1, d, tn), lambda l, j: (l, 0, j)),
                  pl.BlockSpec((1, 1, tn), lambda l, j: (l, 0, j))],
        out_specs=pl.BlockSpec((1, b, tn), lambda l, j: (l, 0, j)),
        out_shape=jax.ShapeDtypeStruct((depth, b, n), F32),
        compiler_params=_params(("parallel", "parallel"), 32),
    )(c, ada_w, ada_b.reshape(depth, 1, n))


def _head_norm_rope(acc, ssum, gain, c_tab, s1_tab, s2_tab):
    y = acc * lax.rsqrt(ssum * (1.0 / HEAD_DIM) + EPS)
    halves = []
    for j in range(0, acc.shape[1], LANES):
        yj = y[:, j:j + LANES] * gain
        halves.append(yj * c_tab + pltpu.roll(yj, ROT_HALF, 1) * s1_tab
                      - pltpu.roll(yj, LANES - ROT_HALF, 1) * s2_tab)
    return halves


def _inproj_kernel(x_ref, mod_ref, g_ref, w_ref, c_ref, s1_ref, s2_ref, gain_a_ref, gain_b_ref,
                   gbias_ref, ones_ref, qa_ref, ka_ref, va_ref, *rest):
    qb_refs, kb_refs, vb_refs = rest[0:3], rest[3:6], rest[6:9]
    gates_ref, h_scr, y_scr = rest[9:12]
    tm = x_ref.shape[0]
    x = x_ref[...]
    ms = jnp.mean(x * x, axis=-1, keepdims=True)
    shift = mod_ref[0, :, 0:D_MODEL]
    scale = mod_ref[0, :, D_MODEL:2 * D_MODEL]
    h = x * lax.rsqrt(ms + EPS) * g_ref[...] * (1.0 + scale) + shift
    h_scr[...] = h.astype(BF16)

    c_tab, s1_tab, s2_tab = c_ref[...], s1_ref[...], s2_ref[...]
    cw = PROJ_CHUNK

    def split_residues(out_ref, dilation):
        for r in range(dilation):
            for j in range(WIDTH_B // LANES):
                out_ref[0, r, :, j * LANES:(j + 1) * LANES] = (
                    y_scr[j, pl.ds(r, tm // dilation, stride=dilation), :].astype(BF16))

    tasks = []

    def normed(out_ref, width, gain):
        def epilogue(acc, ssum, c):
            for j, half in enumerate(_head_norm_rope(acc, ssum, gain, c_tab, s1_tab, s2_tab)):
                out_ref[:, c + j * LANES:c + (j + 1) * LANES] = half.astype(BF16)
        tasks.extend((True, partial(epilogue, c=c)) for c in range(0, width, cw))

    def plain(out_ref, width):
        def epilogue(acc, ssum, c):
            out_ref[:, c:c + cw] = acc.astype(BF16)
        tasks.extend((False, partial(epilogue, c=c)) for c in range(0, width, cw))

    def normed_groups(out_refs, gain):
        def epilogue(acc, ssum, g):
            for j, half in enumerate(_head_norm_rope(acc, ssum, gain, c_tab, s1_tab, s2_tab)):
                y_scr[j] = half
            split_residues(out_refs[g], DILATED_GROUPS[g][1])
        tasks.extend((True, partial(epilogue, g=g)) for g in range(N_GROUPS_B))

    def plain_groups(out_refs):
        def epilogue(acc, ssum, g):
            for j in range(0, WIDTH_B, LANES):
                y_scr[j // LANES] = acc[:, j:j + LANES]
            split_residues(out_refs[g], DILATED_GROUPS[g][1])
        tasks.extend((False, partial(epilogue, g=g)) for g in range(N_GROUPS_B))

    def gate(acc, ssum, c):
        gates_ref[:, c:c + cw] = jax.nn.sigmoid(acc + gbias_ref[:, c:c + cw]).astype(BF16)

    normed(qa_ref, WIDTH_A, gain_a_ref[0:1, :] * (SM_SCALE * LOG2E))
    normed(ka_ref, WIDTH_A, gain_a_ref[1:2, :])
    plain(va_ref, WIDTH_A)
    normed_groups(qb_refs, gain_b_ref[0:1, :] * SM_SCALE)
    normed_groups(kb_refs, gain_b_ref[1:2, :])
    plain_groups(vb_refs)
    tasks.extend((False, partial(gate, c=c)) for c in range(0, GATE_COLS, cw))
    assert len(tasks) * cw == IN_COLS and WIDTH_B == cw

    def proj(n):
        return jnp.dot(h_scr[...], w_ref[:, n * cw:(n + 1) * cw], preferred_element_type=F32)

    def head_sums(n, acc):
        if not tasks[n][0]:
            return None
        return jnp.dot((acc * acc).astype(BF16), ones_ref[...], preferred_element_type=F32)

    last = len(tasks) - 1
    accs = {0: proj(0), 1: proj(1)}
    sums = {0: head_sums(0, accs[0])}
    for n in range(len(tasks)):
        if n + 2 <= last:
            accs[n + 2] = proj(n + 2)
        if n + 1 <= last:
            sums[n + 1] = head_sums(n + 1, accs[n + 1])
        tasks[n][1](accs.pop(n), sums.pop(n))


def _in_projection(x2, mod3, g, w_bf, tabs, gain_a, gain_b, gate_bias, seq):
    n, d = x2.shape
    tm = TOKEN_TILE
    tiles = seq // tm
    heads = jnp.arange(PROJ_CHUNK) // HEAD_DIM
    seg_ones = (heads[:, None] == heads[None, :]).astype(BF16)
    row = lambda i: (i, 0)
    fixed = lambda i: (0, 0)
    res_block = lambda i: (i // tiles, 0, i % tiles, 0)
    wide = [pl.BlockSpec((tm, WIDTH_A), row)] * 3
    wide_shape = [jax.ShapeDtypeStruct((n, WIDTH_A), BF16)] * 3
    grp = [pl.BlockSpec((1, dil, tm // dil, WIDTH_B), res_block) for _, dil in DILATED_GROUPS] * 3
    grp_shape = [jax.ShapeDtypeStruct((n // seq, dil, seq // dil, WIDTH_B), BF16) for _, dil in DILATED_GROUPS] * 3
    outs = pl.pallas_call(
        _inproj_kernel,
        grid=(n // tm,),
        in_specs=[pl.BlockSpec((tm, d), row),
                  pl.BlockSpec((1, 1, mod3.shape[2]), lambda i: (i // tiles, 0, 0)),
                  _resident((1, d), fixed),
                  _resident((d, IN_COLS), fixed),
                  pl.BlockSpec((tm, LANES), row),
                  pl.BlockSpec((tm, LANES), row),
                  pl.BlockSpec((tm, LANES), row),
                  _resident((2, LANES), fixed),
                  _resident((2, LANES), fixed),
                  _resident((1, GATE_COLS), fixed),
                  _resident((PROJ_CHUNK, PROJ_CHUNK), fixed)],
        out_specs=wide + grp + [pl.BlockSpec((tm, GATE_COLS), row)],
        out_shape=wide_shape + grp_shape + [jax.ShapeDtypeStruct((n, GATE_COLS), BF16)],
        scratch_shapes=[pltpu.VMEM((tm, d), BF16), pltpu.VMEM((WIDTH_B // LANES, tm, LANES), F32)],
        compiler_params=_params(("parallel",), 56),
        name="in_proj",
    )(x2, mod3, g, w_bf, *tabs, gain_a, gain_b, gate_bias, seg_ones)
    return outs[0], outs[1], outs[2], outs[3:6], outs[6:9], outs[9:12], outs[12]


DA_TQ = 256
DA_BLOCKS = 2
DA_TK = 512
DA_FINISH_AFTER = 2560
DA_ONES_ROWS = 16


def _diff_attn_kernel(lam_ref, g_ref, q_ref, k_ref, v_ref, o_ref, sa_scr, sb_scr, ma_scr, mb_scr,
                      acc_scr, vt_scr, *, nq, lam_init):
    seq = k_ref.shape[1]
    tq = DA_TQ
    t = pl.program_id(0)

    def load_values():
        vt_scr[0:VDIM_A, :] = v_ref[0].T
        vt_scr[VDIM_A:VDIM_A + DA_ONES_ROWS, :] = jnp.ones((DA_ONES_ROWS, seq), BF16)

    @pl.when(t == 0)
    def _():
        sb_scr[...] = jnp.zeros(sb_scr.shape, F32)
        mb_scr[...] = jnp.zeros(mb_scr.shape, F32)
        acc_scr[...] = jnp.ones(acc_scr.shape, F32)
        load_values()

    def finish(acc_r, rows):
        o = acc_r[0:VDIM_A, :] / acc_r[VDIM_A:VDIM_A + 1, :]
        lv = lam_ref[...]
        lam = (jnp.exp(jnp.sum(lv[0:1] * lv[1:2], axis=-1, keepdims=True))
               - jnp.exp(jnp.sum(lv[2:3] * lv[3:4], axis=-1, keepdims=True)) + lam_init)
        out = o[:, :tq] - lam * o[:, tq:]
        ms = jnp.mean(out * out, axis=0, keepdims=True)
        y = out * lax.rsqrt(ms + EPS) * g_ref[...] * (1.0 - lam_init)
        o_ref[0, rows, :] = y.T.astype(BF16)

    def step(rows, s_w, m_w, acc_w, s_r, m_r, acc_r):
        q = q_ref[0, rows, :]
        lane = lax.broadcasted_iota(jnp.int32, q.shape, 1)
        zero = jnp.zeros_like(q)
        qs = jnp.concatenate([jnp.where(lane < HEAD_DIM, q, zero),
                              jnp.where(lane < HEAD_DIM, zero, q)], axis=0)
        m_prev = m_r[0:1]
        m_new = None
        acc = None
        for c in range(0, seq, DA_TK):
            st = lax.dot_general(k_ref[0, c:c + DA_TK, :], qs, NT_DIMS, preferred_element_type=F32)
            s_w[c:c + DA_TK, :] = st
            mc = jnp.max(st, axis=0, keepdims=True)
            m_new = mc if m_new is None else jnp.maximum(m_new, mc)
            pt = jnp.exp2((s_r[c:c + DA_TK, :] - m_prev).astype(BF16))
            part = jnp.dot(vt_scr[:, c:c + DA_TK], pt, preferred_element_type=F32)
            acc = part if acc is None else acc + part
            if c == DA_FINISH_AFTER:
                finish(acc_r, rows)
        m_w[...] = jnp.broadcast_to(m_new, m_w.shape)
        acc_w[...] = acc

    score_bufs = ((sa_scr, ma_scr), (sb_scr, mb_scr))
    for j in range(DA_BLOCKS):
        (s_w, m_w), (s_r, m_r) = score_bufs[j % 2], score_bufs[1 - j % 2]
        step(slice(j * tq, (j + 1) * tq), s_w, m_w, acc_scr.at[(j - 1) % DA_BLOCKS], s_r, m_r, acc_scr.at[j])
        if j == 0:
            @pl.when((t > 0) & ((DA_BLOCKS * t) % nq == 0))
            def _():
                load_values()


def _diff_attention(qa, ka, va, lam_p, subln_g, lam_init):
    b, s, _ = qa.shape
    nq = s // DA_TQ
    assert nq % DA_BLOCKS == 0 and DA_BLOCKS % 2 == 0
    per_head = nq // DA_BLOCKS
    pairs = b * HEADS_A * per_head

    def pair(t):
        t = jnp.clip(t, 0, pairs - 1)
        return t // (HEADS_A * per_head), (t // per_head) % HEADS_A, t % per_head

    def q_map(t):
        bi, h, i = pair(t)
        return bi, i, h

    def kv_map(t):
        bi, h, _ = pair(t)
        return bi, 0, h

    def o_map(t):
        bi, h, i = pair(t - 1)
        return bi, i, h

    score_buf = pltpu.VMEM((s, 2 * DA_TQ), F32)
    max_buf = pltpu.VMEM((8, 2 * DA_TQ), F32)
    acc_buf = pltpu.VMEM((DA_BLOCKS, VDIM_A + DA_ONES_ROWS, 2 * DA_TQ), F32)
    return pl.pallas_call(
        partial(_diff_attn_kernel, nq=nq, lam_init=lam_init),
        grid=(pairs + 1,),
        in_specs=[pl.BlockSpec(lam_p.shape, lambda t: (0, 0)),
                  pl.BlockSpec((VDIM_A, 1), lambda t: (0, 0)),
                  pl.BlockSpec((1, DA_BLOCKS * DA_TQ, VDIM_A), q_map),
                  pl.BlockSpec((1, s, VDIM_A), kv_map),
                  pl.BlockSpec((1, s, VDIM_A), kv_map)],
        out_specs=pl.BlockSpec((1, DA_BLOCKS * DA_TQ, VDIM_A), o_map),
        out_shape=jax.ShapeDtypeStruct((b, s, WIDTH_A), BF16),
        scratch_shapes=[score_buf, score_buf, max_buf, max_buf, acc_buf,
                        pltpu.VMEM((VDIM_A + DA_ONES_ROWS, s), BF16)],
        compiler_params=_params(("arbitrary",), 40),
        name="diff_attn",
    )(lam_p, subln_g, qa, ka, va)


DIL_TQ = 128
DIL_SKEW = 1
DIL_WIN = 256


def _band_scores(q, kw):
    head = lax.broadcasted_iota(jnp.int32, q.shape, 1) // HEAD_DIM
    zero = jnp.zeros_like(q)
    qs = jnp.concatenate([jnp.where(head == h, q, zero) for h in range(HEADS_PER_GROUP_B)], axis=0)
    return lax.dot_general(qs, kw, NT_DIMS, preferred_element_type=F32)


def _band_softmax(s, rel, offset, side):
    s = jnp.where(jnp.abs(rel + offset) <= side, s, NEG_INF)
    m = jnp.max(s, axis=-1, keepdims=True)
    p = jnp.exp(s - m)
    l = jnp.sum(p, axis=-1, keepdims=True)
    return p.astype(BF16), m, l


def _band_output(p, m, l, vw, nq):
    o = jnp.dot(p, vw, preferred_element_type=F32) / l
    lse = jnp.broadcast_to(m + jnp.log(l), o.shape)
    head = lax.broadcasted_iota(jnp.int32, (nq, WIDTH_B), 1) // HEAD_DIM
    o_sel = jnp.zeros((nq, WIDTH_B), F32)
    lse_sel = jnp.zeros((nq, WIDTH_B), F32)
    for h in range(HEADS_PER_GROUP_B):
        rs = slice(h * nq, (h + 1) * nq)
        o_sel = jnp.where(head == h, o[rs], o_sel)
        lse_sel = jnp.where(head == h, lse[rs], lse_sel)
    return o_sel, lse_sel


def _dilated_kernel(*refs):
    q_refs, k_refs, v_refs = refs[0:3], refs[3:6], refs[6:9]
    ob_ref, o_scr, lse_scr = refs[9:12]
    tile = ob_ref.shape[0]
    i = pl.program_id(1)

    blocks = []
    for g, (window, dil) in enumerate(DILATED_GROUPS):
        side = window // (2 * dil)
        length = k_refs[g].shape[2]
        per_res = tile // dil
        nq = min(per_res, DIL_TQ)
        rows = lax.broadcasted_iota(jnp.int32, (HEADS_PER_GROUP_B * nq, DIL_WIN), 0)
        cols = lax.broadcasted_iota(jnp.int32, (HEADS_PER_GROUP_B * nq, DIL_WIN), 1)
        rel = (rows & (nq - 1)) - cols
        for r in range(dil):
            for u in range(per_res // nq):
                t0 = i * per_res + u * nq
                if length == DIL_WIN:
                    ks = 0
                else:
                    ks = pl.multiple_of(jnp.clip(t0 - side, 0, length - DIL_WIN), side)
                blocks.append(dict(g=g, dil=dil, r=r, u=u, nq=nq, t0=t0, ks=ks, side=side, rel=rel))

    def scores(bk):
        q = q_refs[bk["g"]][0, bk["r"], bk["u"] * bk["nq"]:(bk["u"] + 1) * bk["nq"], :]
        kw = k_refs[bk["g"]][0, bk["r"], pl.ds(bk["ks"], DIL_WIN), :]
        bk["s"] = _band_scores(q, kw)

    def softmax(bk):
        bk["pml"] = _band_softmax(bk.pop("s"), bk["rel"], bk["t0"] - bk["ks"], bk["side"])

    def output(bk):
        g, dil, r, u, nq = bk["g"], bk["dil"], bk["r"], bk["u"], bk["nq"]
        vw = v_refs[g][0, r, pl.ds(bk["ks"], DIL_WIN), :]
        o, lse = _band_output(*bk.pop("pml"), vw, nq)
        rows = pl.ds(u * nq * dil + r, nq, stride=dil) if dil > 1 else pl.ds(u * nq, nq)
        for j in range(WIDTH_B // LANES):
            o_scr[g, j, rows, :] = o[:, j * LANES:(j + 1) * LANES]
            lse_scr[g, j, rows, :] = lse[:, j * LANES:(j + 1) * LANES]

    for n in range(len(blocks) + 2 * DIL_SKEW):
        if n < len(blocks):
            scores(blocks[n])
        if 0 <= n - DIL_SKEW < len(blocks):
            softmax(blocks[n - DIL_SKEW])
        if 0 <= n - 2 * DIL_SKEW < len(blocks):
            output(blocks[n - 2 * DIL_SKEW])

    for j in range(WIDTH_B // LANES):
        l0, l1, l2 = lse_scr[0, j], lse_scr[1, j], lse_scr[2, j]
        m = jnp.maximum(jnp.maximum(l0, l1), l2)
        e0, e1, e2 = jnp.exp(l0 - m), jnp.exp(l1 - m), jnp.exp(l2 - m)
        ob = (e0 * o_scr[0, j] + e1 * o_scr[1, j] + e2 * o_scr[2, j]) / (e0 + e1 + e2)
        ob_ref[:, j * LANES:(j + 1) * LANES] = ob.astype(BF16)


def _dilated_attention(qs, ks, vs):
    b = qs[0].shape[0]
    seq = qs[0].shape[1] * qs[0].shape[2]
    tile = TOKEN_TILE
    tiles = seq // tile
    q_specs = [pl.BlockSpec((1, dil, tile // dil, WIDTH_B), lambda bi, i: (bi, 0, i, 0))
               for _, dil in DILATED_GROUPS]
    kv_specs = [pl.BlockSpec((1, dil, seq // dil, WIDTH_B), lambda bi, i: (bi, 0, 0, 0),
                             pipeline_mode=pl.Buffered(1))
                for _, dil in DILATED_GROUPS]
    return pl.pallas_call(
        _dilated_kernel,
        grid=(b, tiles),
        in_specs=q_specs + kv_specs + kv_specs,
        out_specs=pl.BlockSpec((tile, WIDTH_B), lambda bi, i: (bi * tiles + i, 0)),
        out_shape=jax.ShapeDtypeStruct((b * seq, WIDTH_B), BF16),
        scratch_shapes=[pltpu.VMEM((N_GROUPS_B, WIDTH_B // LANES, tile, LANES), F32)] * 2,
        compiler_params=_params(("parallel", "arbitrary"), 48),
        name="dilated_attn",
    )(*qs, *ks, *vs)


def _post_kernel(x_ref, mod_ref, oa_ref, ob_ref, gates_ref, wa_ref, wb_ref, wo_ref, g_ref, wu_ref, wd_ref,
                 out_ref):
    br_a = jnp.dot(oa_ref[...], wa_ref[...], preferred_element_type=F32)
    br_b = jnp.dot(ob_ref[...], wb_ref[...], preferred_element_type=F32)
    g_a = gates_ref[:, 0:D_MODEL].astype(F32)
    g_b = gates_ref[:, D_MODEL:GATE_COLS].astype(F32)
    y = jnp.dot((g_a * br_a + g_b * br_b).astype(BF16), wo_ref[...], preferred_element_type=F32)
    mod = lambda k: mod_ref[0, :, k * D_MODEL:(k + 1) * D_MODEL]
    x = x_ref[...] + mod(2) * y

    ms = jnp.mean(x * x, axis=-1, keepdims=True)
    h = (x * lax.rsqrt(ms + EPS) * g_ref[...] * (1.0 + mod(4)) + mod(3)).astype(BF16)
    u = jnp.maximum(jnp.dot(h, wu_ref[...], preferred_element_type=F32), 0.0)
    dn = jnp.dot((u * u).astype(BF16), wd_ref[...], preferred_element_type=F32)
    out_ref[...] = x + mod(5) * dn


def _post_attention(x2, mod3, oa, ob, gates, wa, wb, wo, g, wu, wd, seq):
    n, d = x2.shape
    tm = TOKEN_TILE
    row = lambda i: (i, 0)
    fixed = lambda i: (0, 0)
    return pl.pallas_call(
        _post_kernel,
        grid=(n // tm,),
        in_specs=[pl.BlockSpec((tm, d), row),
                  pl.BlockSpec((1, 1, mod3.shape[2]), lambda i: ((i * tm) // seq, 0, 0)),
                  pl.BlockSpec((tm, WIDTH_A), row),
                  pl.BlockSpec((tm, WIDTH_B), row),
                  pl.BlockSpec((tm, GATE_COLS), row),
                  _resident(wa.shape, fixed), _resident(wb.shape, fixed), _resident(wo.shape, fixed),
                  _resident((1, d), fixed), _resident(wu.shape, fixed), _resident(wd.shape, fixed)],
        out_specs=pl.BlockSpec((tm, d), row),
        out_shape=jax.ShapeDtypeStruct((n, d), F32),
        compiler_params=_params(("parallel",), 60),
        name="post_attn",
    )(x2, mod3, oa, ob, gates, wa, wb, wo, g, wu, wd)


def kernel(x, c, positions, ada_w, ada_b, norm_mix_g, norm_mlp_g, w_in, qk_gain_a, lambda_a, subln_g_a,
           qk_gain_b, w_branch_a, w_branch_b, gate_bias, w_out, w_mlp_up, w_mlp_down):
    b, s, d = x.shape
    depth = ada_w.shape[0]
    tabs = _rope_tables(positions)
    mod = _ada_mod(c, ada_w, ada_b)
    rep = LANES // HEAD_DIM
    x2 = x.reshape(b * s, d)
    for l in range(depth):
        mod3 = mod[l].reshape(b, 1, 6 * d)
        qa, ka, va, qbs, kbs, vbs, gates = _in_projection(
            x2, mod3, norm_mix_g[l].reshape(1, d), w_in[l].astype(BF16), tabs,
            jnp.tile(qk_gain_a[l], (1, rep)), jnp.tile(qk_gain_b[l], (1, rep)),
            gate_bias[l].reshape(1, GATE_COLS), s)

        lam_init = 0.8 - 0.6 * math.exp(-0.3 * l)
        oa = _diff_attention(qa.reshape(b, s, WIDTH_A), ka.reshape(b, s, WIDTH_A),
                             va.reshape(b, s, WIDTH_A), lambda_a[l], subln_g_a[l].reshape(VDIM_A, 1),
                             lam_init)
        ob = _dilated_attention(qbs, kbs, vbs)

        x2 = _post_attention(x2, mod3, oa.reshape(b * s, WIDTH_A), ob, gates,
                             w_branch_a[l].astype(BF16), w_branch_b[l].astype(BF16), w_out[l].astype(BF16),
                             norm_mlp_g[l].reshape(1, d), w_mlp_up[l].astype(BF16),
                             w_mlp_down[l].astype(BF16), s)
    return x2.reshape(b, s, d)
```

```python
import math
from functools import partial

import jax
import jax.numpy as jnp
from jax import lax
from jax.experimental import pallas as pl
from jax.experimental.pallas import tpu as pltpu

D_MODEL = 1024
HEAD_DIM = 64
HEADS_A = 8
VDIM_A = 2 * HEAD_DIM
WIDTH_A = HEADS_A * VDIM_A
DILATED_GROUPS = ((128, 1), (512, 4), (2048, 16))
N_GROUPS_B = 3
HEADS_PER_GROUP_B = 4
WIDTH_B = HEADS_PER_GROUP_B * HEAD_DIM
QKV_B_COLS = N_GROUPS_B * WIDTH_B
D_FF = 4 * D_MODEL
ROPE_THETA = 500000.0
ROT_HALF = HEAD_DIM // 4 // 2
EPS = 1e-6
NEG_INF = -1e30
GATE_COLS = 2 * D_MODEL
IN_COLS = 3 * WIDTH_A + 3 * QKV_B_COLS + GATE_COLS
SM_SCALE = HEAD_DIM ** -0.5
LOG2E = math.log2(math.e)

LANES = 128
PROJ_CHUNK = 256
TOKEN_TILE = 512
VMEM_BYTES = 64 * 1024 * 1024
MIB = 1024 * 1024

BF16 = jnp.bfloat16
F32 = jnp.float32
NT_DIMS = (((1,), (1,)), ((), ()))


def _params(semantics, vmem_mib):
    return pltpu.CompilerParams(dimension_semantics=semantics,
                                vmem_limit_bytes=min(vmem_mib * MIB, VMEM_BYTES - 4 * MIB))


def _resident(shape, index_map):
    return pl.BlockSpec(shape, index_map, pipeline_mode=pl.Buffered(1))


def _rope_kernel(pos_ref, cos_ref, sin_ref):
    pos = pos_ref[...].astype(F32)
    for f in range(ROT_HALF):
        ang = pos * (ROPE_THETA ** (-(2.0 * f) / (2 * ROT_HALF)))
        cos_ref[f] = jnp.cos(ang)
        sin_ref[f] = jnp.sin(ang)


def _rope_tables(positions):
    n = positions.size
    pos2 = positions.reshape(n // LANES, LANES)
    cos, sin = pl.pallas_call(
        _rope_kernel,
        out_shape=[jax.ShapeDtypeStruct((ROT_HALF, n // LANES, LANES), F32)] * 2,
    )(pos2)
    cos = cos.reshape(ROT_HALF, n).T
    sin = sin.reshape(ROT_HALF, n).T
    one = jnp.ones((n, HEAD_DIM - 2 * ROT_HALF), F32)
    zero = jnp.zeros((n, HEAD_DIM - 2 * ROT_HALF), F32)
    z8 = jnp.zeros((n, ROT_HALF), F32)
    c_tab = jnp.concatenate([cos, cos, one], axis=1)
    s1_tab = jnp.concatenate([z8, sin, zero], axis=1)
    s2_tab = jnp.concatenate([sin, z8, zero], axis=1)
    rep = LANES // HEAD_DIM
    return tuple(jnp.tile(t, (1, rep)) for t in (c_tab, s1_tab, s2_tab))


def _ada_kernel(c_ref, w_ref, b_ref, o_ref):
    c = c_ref[...]
    cond = (c * jax.nn.sigmoid(c)).astype(BF16)
    o_ref[0] = jnp.dot(cond, w_ref[0].astype(BF16), preferred_element_type=F32) + b_ref[0]


def _ada_mod(c, ada_w, ada_b):
    depth, d, n = ada_w.shape
    b = c.shape[0]
    tn = 768
    return pl.pallas_call(
        _ada_kernel,
        grid=(depth, n // tn),
        in_specs=[pl.BlockSpec((b, d), lambda l, j: (0, 0)),
                  pl.BlockSpec((1, d, tn), lambda l, j: (l, 0, j)),
                  pl.BlockSpec((1, 1, tn), lambda l, j: (l, 0, j))],
        out_specs=pl.BlockSpec((1, b, tn), lambda l, j: (l, 0, j)),
        out_shape=jax.ShapeDtypeStruct((depth, b, n), F32),
        compiler_params=_params(("parallel", "parallel"), 32),
    )(c, ada_w, ada_b.reshape(depth, 1, n))


def _head_norm_rope(acc, ssum, gain, c_tab, s1_tab, s2_tab):
    y = acc * lax.rsqrt(ssum * (1.0 / HEAD_DIM) + EPS)
    halves = []
    for j in range(0, acc.shape[1], LANES):
        yj = y[:, j:j + LANES] * gain
        halves.append(yj * c_tab + pltpu.roll(yj, ROT_HALF, 1) * s1_tab
                      - pltpu.roll(yj, LANES - ROT_HALF, 1) * s2_tab)
    return halves


def _inproj_kernel(x_ref, mod_ref, g_ref, w_ref, c_ref, s1_ref, s2_ref, gain_a_ref, gain_b_ref,
                   gbias_ref, ones_ref, qa_ref, ka_ref, va_ref, *rest):
    qb_refs, kb_refs, vb_refs = rest[0:3], rest[3:6], rest[6:9]
    gates_ref, h_scr, y_scr = rest[9:12]
    tm = x_ref.shape[0]
    x = x_ref[...]
    ms = jnp.mean(x * x, axis=-1, keepdims=True)
    shift = mod_ref[0, :, 0:D_MODEL]
    scale = mod_ref[0, :, D_MODEL:2 * D_MODEL]
    h = x * lax.rsqrt(ms + EPS) * g_ref[...] * (1.0 + scale) + shift
    h_scr[...] = h.astype(BF16)

    c_tab, s1_tab, s2_tab = c_ref[...], s1_ref[...], s2_ref[...]
    cw = PROJ_CHUNK

    def split_residues(out_ref, dilation):
        for r in range(dilation):
            for j in range(WIDTH_B // LANES):
                out_ref[0, r, :, j * LANES:(j + 1) * LANES] = (
                    y_scr[j, pl.ds(r, tm // dilation, stride=dilation), :].astype(BF16))

    tasks = []

    def normed(out_ref, width, gain):
        def epilogue(acc, ssum, c):
            for j, half in enumerate(_head_norm_rope(acc, ssum, gain, c_tab, s1_tab, s2_tab)):
                out_ref[:, c + j * LANES:c + (j + 1) * LANES] = half.astype(BF16)
        tasks.extend((True, partial(epilogue, c=c)) for c in range(0, width, cw))

    def plain(out_ref, width):
        def epilogue(acc, ssum, c):
            out_ref[:, c:c + cw] = acc.astype(BF16)
        tasks.extend((False, partial(epilogue, c=c)) for c in range(0, width, cw))

    def normed_groups(out_refs, gain):
        def epilogue(acc, ssum, g):
            for j, half in enumerate(_head_norm_rope(acc, ssum, gain, c_tab, s1_tab, s2_tab)):
                y_scr[j] = half
            split_residues(out_refs[g], DILATED_GROUPS[g][1])
        tasks.extend((True, partial(epilogue, g=g)) for g in range(N_GROUPS_B))

    def plain_groups(out_refs):
        def epilogue(acc, ssum, g):
            for j in range(0, WIDTH_B, LANES):
                y_scr[j // LANES] = acc[:, j:j + LANES]
            split_residues(out_refs[g], DILATED_GROUPS[g][1])
        tasks.extend((False, partial(epilogue, g=g)) for g in range(N_GROUPS_B))

    def gate(acc, ssum, c):
        gates_ref[:, c:c + cw] = jax.nn.sigmoid(acc + gbias_ref[:, c:c + cw]).astype(BF16)

    normed(qa_ref, WIDTH_A, gain_a_ref[0:1, :] * (SM_SCALE * LOG2E))
    normed(ka_ref, WIDTH_A, gain_a_ref[1:2, :])
    plain(va_ref, WIDTH_A)
    normed_groups(qb_refs, gain_b_ref[0:1, :] * SM_SCALE)
    normed_groups(kb_refs, gain_b_ref[1:2, :])
    plain_groups(vb_refs)
    tasks.extend((False, partial(gate, c=c)) for c in range(0, GATE_COLS, cw))
    assert len(tasks) * cw == IN_COLS and WIDTH_B == cw

    def proj(n):
        return jnp.dot(h_scr[...], w_ref[:, n * cw:(n + 1) * cw], preferred_element_type=F32)

    def head_sums(n, acc):
        if not tasks[n][0]:
            return None
        return jnp.dot((acc * acc).astype(BF16), ones_ref[...], preferred_element_type=F32)

    last = len(tasks) - 1
    accs = {0: proj(0), 1: proj(1)}
    sums = {0: head_sums(0, accs[0])}
    for n in range(len(tasks)):
        if n + 2 <= last:
            accs[n + 2] = proj(n + 2)
        if n + 1 <= last:
            sums[n + 1] = head_sums(n + 1, accs[n + 1])
        tasks[n][1](accs.pop(n), sums.pop(n))


def _in_projection(x2, mod3, g, w_bf, tabs, gain_a, gain_b, gate_bias, seq):
    n, d = x2.shape
    tm = TOKEN_TILE
    tiles = seq // tm
    heads = jnp.arange(PROJ_CHUNK) // HEAD_DIM
    seg_ones = (heads[:, None] == heads[None, :]).astype(BF16)
    row = lambda i: (i, 0)
    fixed = lambda i: (0, 0)
    res_block = lambda i: (i // tiles, 0, i % tiles, 0)
    wide = [pl.BlockSpec((tm, WIDTH_A), row)] * 3
    wide_shape = [jax.ShapeDtypeStruct((n, WIDTH_A), BF16)] * 3
    grp = [pl.BlockSpec((1, dil, tm // dil, WIDTH_B), res_block) for _, dil in DILATED_GROUPS] * 3
    grp_shape = [jax.ShapeDtypeStruct((n // seq, dil, seq // dil, WIDTH_B), BF16) for _, dil in DILATED_GROUPS] * 3
    outs = pl.pallas_call(
        _inproj_kernel,
        grid=(n // tm,),
        in_specs=[pl.BlockSpec((tm, d), row),
                  pl.BlockSpec((1, 1, mod3.shape[2]), lambda i: (i // tiles, 0, 0)),
                  _resident((1, d), fixed),
                  _resident((d, IN_COLS), fixed),
                  pl.BlockSpec((tm, LANES), row),
                  pl.BlockSpec((tm, LANES), row),
                  pl.BlockSpec((tm, LANES), row),
                  _resident((2, LANES), fixed),
                  _resident((2, LANES), fixed),
                  _resident((1, GATE_COLS), fixed),
                  _resident((PROJ_CHUNK, PROJ_CHUNK), fixed)],
        out_specs=wide + grp + [pl.BlockSpec((tm, GATE_COLS), row)],
        out_shape=wide_shape + grp_shape + [jax.ShapeDtypeStruct((n, GATE_COLS), BF16)],
        scratch_shapes=[pltpu.VMEM((tm, d), BF16), pltpu.VMEM((WIDTH_B // LANES, tm, LANES), F32)],
        compiler_params=_params(("parallel",), 56),
        name="in_proj",
    )(x2, mod3, g, w_bf, *tabs, gain_a, gain_b, gate_bias, seg_ones)
    return outs[0], outs[1], outs[2], outs[3:6], outs[6:9], outs[9:12], outs[12]


DA_TQ = 256
DA_BLOCKS = 2
DA_TK = 512
DA_FINISH_AFTER = 1024
DA_ONES_ROWS = 16


def _diff_attn_kernel(lam_ref, g_ref, q_ref, k_ref, v_ref, o_ref, sa_scr, sb_scr, ma_scr, mb_scr,
                      acc_scr, vt_scr, *, nq, lam_init):
    seq = k_ref.shape[1]
    tq = DA_TQ
    t = pl.program_id(0)

    def load_values():
        vt_scr[0:VDIM_A, :] = v_ref[0].T
        vt_scr[VDIM_A:VDIM_A + DA_ONES_ROWS, :] = jnp.ones((DA_ONES_ROWS, seq), BF16)

    @pl.when(t == 0)
    def _():
        sb_scr[...] = jnp.zeros(sb_scr.shape, F32)
        mb_scr[...] = jnp.zeros(mb_scr.shape, F32)
        acc_scr[...] = jnp.ones(acc_scr.shape, F32)
        load_values()

    def finish(acc_r, rows):
        o = acc_r[0:VDIM_A, :] / acc_r[VDIM_A:VDIM_A + 1, :]
        lv = lam_ref[...]
        lam = (jnp.exp(jnp.sum(lv[0:1] * lv[1:2], axis=-1, keepdims=True))
               - jnp.exp(jnp.sum(lv[2:3] * lv[3:4], axis=-1, keepdims=True)) + lam_init)
        out = o[:, :tq] - lam * o[:, tq:]
        ms = jnp.mean(out * out, axis=0, keepdims=True)
        y = out * lax.rsqrt(ms + EPS) * g_ref[...] * (1.0 - lam_init)
        o_ref[0, rows, :] = y.T.astype(BF16)

    def step(rows, s_w, m_w, acc_w, s_r, m_r, acc_r):
        q = q_ref[0, rows, :]
        lane = lax.broadcasted_iota(jnp.int32, q.shape, 1)
        zero = jnp.zeros_like(q)
        qs = jnp.concatenate([jnp.where(lane < HEAD_DIM, q, zero),
                              jnp.where(lane < HEAD_DIM, zero, q)], axis=0)
        m_prev = m_r[0:1]
        m_new = None
        acc = None
        for c in range(0, seq, DA_TK):
            st = lax.dot_general(k_ref[0, c:c + DA_TK, :], qs, NT_DIMS, preferred_element_type=F32)
            s_w[c:c + DA_TK, :] = st
            mc = jnp.max(st, axis=0, keepdims=True)
            m_new = mc if m_new is None else jnp.maximum(m_new, mc)
            pt = jnp.exp2((s_r[c:c + DA_TK, :] - m_prev).astype(BF16))
            part = jnp.dot(vt_scr[:, c:c + DA_TK], pt, preferred_element_type=F32)
            acc = part if acc is None else acc + part
            if c == DA_FINISH_AFTER:
                finish(acc_r, rows)
        m_w[...] = jnp.broadcast_to(m_new, m_w.shape)
        acc_w[...] = acc

    score_bufs = ((sa_scr, ma_scr), (sb_scr, mb_scr))
    for j in range(DA_BLOCKS):
        (s_w, m_w), (s_r, m_r) = score_bufs[j % 2], score_bufs[1 - j % 2]
        step(slice(j * tq, (j + 1) * tq), s_w, m_w, acc_scr.at[(j - 1) % DA_BLOCKS], s_r, m_r, acc_scr.at[j])
        if j == 0:
            @pl.when((t > 0) & ((DA_BLOCKS * t) % nq == 0))
            def _():
                load_values()


def _diff_attention(qa, ka, va, lam_p, subln_g, lam_init):
    b, s, _ = qa.shape
    nq = s // DA_TQ
    assert nq % DA_BLOCKS == 0 and DA_BLOCKS % 2 == 0
    per_head = nq // DA_BLOCKS
    pairs = b * HEADS_A * per_head

    def pair(t):
        t = jnp.clip(t, 0, pairs - 1)
        return t // (HEADS_A * per_head), (t // per_head) % HEADS_A, t % per_head

    def q_map(t):
        bi, h, i = pair(t)
        return bi, i, h

    def kv_map(t):
        bi, h, _ = pair(t)
        return bi, 0, h

    def o_map(t):
        bi, h, i = pair(t - 1)
        return bi, i, h

    score_buf = pltpu.VMEM((s, 2 * DA_TQ), F32)
    max_buf = pltpu.VMEM((8, 2 * DA_TQ), F32)
    acc_buf = pltpu.VMEM((DA_BLOCKS, VDIM_A + DA_ONES_ROWS, 2 * DA_TQ), F32)
    return pl.pallas_call(
        partial(_diff_attn_kernel, nq=nq, lam_init=lam_init),
        grid=(pairs + 1,),
        in_specs=[pl.BlockSpec(lam_p.shape, lambda t: (0, 0)),
                  pl.BlockSpec((VDIM_A, 1), lambda t: (0, 0)),
                  pl.BlockSpec((1, DA_BLOCKS * DA_TQ, VDIM_A), q_map),
                  pl.BlockSpec((1, s, VDIM_A), kv_map),
                  pl.BlockSpec((1, s, VDIM_A), kv_map)],
        out_specs=pl.BlockSpec((1, DA_BLOCKS * DA_TQ, VDIM_A), o_map),
        out_shape=jax.ShapeDtypeStruct((b, s, WIDTH_A), BF16),
        scratch_shapes=[score_buf, score_buf, max_buf, max_buf, acc_buf,
                        pltpu.VMEM((VDIM_A + DA_ONES_ROWS, s), BF16)],
        compiler_params=_params(("arbitrary",), 40),
        name="diff_attn",
    )(lam_p, subln_g, qa, ka, va)


DIL_TQ = 128
DIL_SKEW = 1
DIL_WIN = 256


def _band_scores(q, kw):
    head = lax.broadcasted_iota(jnp.int32, q.shape, 1) // HEAD_DIM
    zero = jnp.zeros_like(q)
    qs = jnp.concatenate([jnp.where(head == h, q, zero) for h in range(HEADS_PER_GROUP_B)], axis=0)
    return lax.dot_general(qs, kw, NT_DIMS, preferred_element_type=F32)


def _band_softmax(s, bias):
    s = s + bias
    m = jnp.max(s, axis=-1, keepdims=True)
    p = jnp.exp(s - m)
    l = jnp.sum(p, axis=-1, keepdims=True)
    return p.astype(BF16), m, l


def _band_output(p, m, l, vw, nq):
    o = jnp.dot(p, vw, preferred_element_type=F32) / l
    lse = jnp.broadcast_to(m + jnp.log(l), o.shape)
    head = lax.broadcasted_iota(jnp.int32, (nq, WIDTH_B), 1) // HEAD_DIM
    o_sel = jnp.zeros((nq, WIDTH_B), F32)
    lse_sel = jnp.zeros((nq, WIDTH_B), F32)
    for h in range(HEADS_PER_GROUP_B):
        rs = slice(h * nq, (h + 1) * nq)
        o_sel = jnp.where(head == h, o[rs], o_sel)
        lse_sel = jnp.where(head == h, lse[rs], lse_sel)
    return o_sel, lse_sel


def _band_bias(nq, offsets, side):
    rel = (jnp.arange(HEADS_PER_GROUP_B * nq)[:, None] & (nq - 1)) - jnp.arange(DIL_WIN)[None, :]
    dist = rel[None] + jnp.asarray(offsets, jnp.int32)[:, None, None]
    return jnp.where(jnp.abs(dist) <= side, 0.0, NEG_INF).astype(F32)


def _dilated_kernel(*refs):
    q_refs, k_refs, v_refs = refs[0:3], refs[3:6], refs[6:9]
    edge_bias_ref, full_bias_ref, ob_ref, o_scr, lse_scr = refs[9:14]
    tile = ob_ref.shape[0]
    i = pl.program_id(1)

    blocks = []
    for g, (window, dil) in enumerate(DILATED_GROUPS):
        side = window // (2 * dil)
        length = k_refs[g].shape[2]
        per_res = tile // dil
        nq = min(per_res, DIL_TQ)
        for r in range(dil):
            for u in range(per_res // nq):
                t0 = i * per_res + u * nq
                if length == DIL_WIN:
                    ks, bias = 0, (full_bias_ref, i)
                else:
                    ks = pl.multiple_of(jnp.clip(t0 - side, 0, length - DIL_WIN), side)
                    bias = (edge_bias_ref, (t0 - ks) // side)
                blocks.append(dict(g=g, dil=dil, r=r, u=u, nq=nq, ks=ks, bias=bias))

    def scores(bk):
        q = q_refs[bk["g"]][0, bk["r"], bk["u"] * bk["nq"]:(bk["u"] + 1) * bk["nq"], :]
        kw = k_refs[bk["g"]][0, bk["r"], pl.ds(bk["ks"], DIL_WIN), :]
        bk["s"] = _band_scores(q, kw)

    def softmax(bk):
        table, variant = bk["bias"]
        bk["pml"] = _band_softmax(bk.pop("s"), table[variant])

    def output(bk):
        g, dil, r, u, nq = bk["g"], bk["dil"], bk["r"], bk["u"], bk["nq"]
        vw = v_refs[g][0, r, pl.ds(bk["ks"], DIL_WIN), :]
        o, lse = _band_output(*bk.pop("pml"), vw, nq)
        rows = pl.ds(u * nq * dil + r, nq, stride=dil) if dil > 1 else pl.ds(u * nq, nq)
        for j in range(WIDTH_B // LANES):
            o_scr[g, j, rows, :] = o[:, j * LANES:(j + 1) * LANES]
            lse_scr[g, j, rows, :] = lse[:, j * LANES:(j + 1) * LANES]

    for n in range(len(blocks) + 2 * DIL_SKEW):
        if n < len(blocks):
            scores(blocks[n])
        if 0 <= n - DIL_SKEW < len(blocks):
            softmax(blocks[n - DIL_SKEW])
        if 0 <= n - 2 * DIL_SKEW < len(blocks):
            output(blocks[n - 2 * DIL_SKEW])

    for j in range(WIDTH_B // LANES):
        l0, l1, l2 = lse_scr[0, j], lse_scr[1, j], lse_scr[2, j]
        m = jnp.maximum(jnp.maximum(l0, l1), l2)
        e0, e1, e2 = jnp.exp(l0 - m), jnp.exp(l1 - m), jnp.exp(l2 - m)
        ob = (e0 * o_scr[0, j] + e1 * o_scr[1, j] + e2 * o_scr[2, j]) / (e0 + e1 + e2)
        ob_ref[:, j * LANES:(j + 1) * LANES] = ob.astype(BF16)


def _dilated_attention(qs, ks, vs):
    b = qs[0].shape[0]
    seq = qs[0].shape[1] * qs[0].shape[2]
    tile = TOKEN_TILE
    tiles = seq // tile
    q_specs = [pl.BlockSpec((1, dil, tile // dil, WIDTH_B), lambda bi, i: (bi, 0, i, 0))
               for _, dil in DILATED_GROUPS]
    kv_specs = [pl.BlockSpec((1, dil, seq // dil, WIDTH_B), lambda bi, i: (bi, 0, 0, 0),
                             pipeline_mode=pl.Buffered(1))
                for _, dil in DILATED_GROUPS]
    sides = {window // (2 * dil) for window, dil in DILATED_GROUPS}
    assert len(sides) == 1 and DIL_WIN == 2 * DIL_TQ and all(s_ == DIL_TQ // 2 for s_ in sides)
    side = sides.pop()
    short = [tile // dil for _, dil in DILATED_GROUPS if seq // dil == DIL_WIN]
    assert len(set(short)) == 1
    edge_bias = _band_bias(DIL_TQ, [0, side, 2 * side], side)
    full_bias = _band_bias(short[0], [i * short[0] for i in range(tiles)], side)
    bias_specs = [_resident(edge_bias.shape, lambda bi, i: (0, 0, 0)),
                  _resident(full_bias.shape, lambda bi, i: (0, 0, 0))]
    return pl.pallas_call(
        _dilated_kernel,
        grid=(b, tiles),
        in_specs=q_specs + kv_specs + kv_specs + bias_specs,
        out_specs=pl.BlockSpec((tile, WIDTH_B), lambda bi, i: (bi * tiles + i, 0)),
        out_shape=jax.ShapeDtypeStruct((b * seq, WIDTH_B), BF16),
        scratch_shapes=[pltpu.VMEM((N_GROUPS_B, WIDTH_B // LANES, tile, LANES), F32)] * 2,
        compiler_params=_params(("parallel", "arbitrary"), 48),
        name="dilated_attn",
    )(*qs, *ks, *vs, edge_bias, full_bias)


def _post_kernel(x_ref, mod_ref, oa_ref, ob_ref, gates_ref, wa_ref, wb_ref, wo_ref, g_ref, wu_ref, wd_ref,
                 out_ref):
    br_a = jnp.dot(oa_ref[...], wa_ref[...], preferred_element_type=F32)
    br_b = jnp.dot(ob_ref[...], wb_ref[...], preferred_element_type=F32)
    g_a = gates_ref[:, 0:D_MODEL].astype(F32)
    g_b = gates_ref[:, D_MODEL:GATE_COLS].astype(F32)
    y = jnp.dot((g_a * br_a + g_b * br_b).astype(BF16), wo_ref[...], preferred_element_type=F32)
    mod = lambda k: mod_ref[0, :, k * D_MODEL:(k + 1) * D_MODEL]
    x = x_ref[...] + mod(2) * y

    ms = jnp.mean(x * x, axis=-1, keepdims=True)
    h = (x * lax.rsqrt(ms + EPS) * g_ref[...] * (1.0 + mod(4)) + mod(3)).astype(BF16)
    u = jnp.maximum(jnp.dot(h, wu_ref[...], preferred_element_type=F32), 0.0)
    dn = jnp.dot((u * u).astype(BF16), wd_ref[...], preferred_element_type=F32)
    out_ref[...] = x + mod(5) * dn


def _post_attention(x2, mod3, oa, ob, gates, wa, wb, wo, g, wu, wd, seq):
    n, d = x2.shape
    tm = TOKEN_TILE
    row = lambda i: (i, 0)
    fixed = lambda i: (0, 0)
    return pl.pallas_call(
        _post_kernel,
        grid=(n // tm,),
        in_specs=[pl.BlockSpec((tm, d), row),
                  pl.BlockSpec((1, 1, mod3.shape[2]), lambda i: ((i * tm) // seq, 0, 0)),
                  pl.BlockSpec((tm, WIDTH_A), row),
                  pl.BlockSpec((tm, WIDTH_B), row),
                  pl.BlockSpec((tm, GATE_COLS), row),
                  _resident(wa.shape, fixed), _resident(wb.shape, fixed), _resident(wo.shape, fixed),
                  _resident((1, d), fixed), _resident(wu.shape, fixed), _resident(wd.shape, fixed)],
        out_specs=pl.BlockSpec((tm, d), row),
        out_shape=jax.ShapeDtypeStruct((n, d), F32),
        compiler_params=_params(("parallel",), 60),
        name="post_attn",
    )(x2, mod3, oa, ob, gates, wa, wb, wo, g, wu, wd)


def kernel(x, c, positions, ada_w, ada_b, norm_mix_g, norm_mlp_g, w_in, qk_gain_a, lambda_a, subln_g_a,
           qk_gain_b, w_branch_a, w_branch_b, gate_bias, w_out, w_mlp_up, w_mlp_down):
    b, s, d = x.shape
    depth = ada_w.shape[0]
    tabs = _rope_tables(positions)
    mod = _ada_mod(c, ada_w, ada_b)
    rep = LANES // HEAD_DIM
    x2 = x.reshape(b * s, d)
    for l in range(depth):
        mod3 = mod[l].reshape(b, 1, 6 * d)
        qa, ka, va, qbs, kbs, vbs, gates = _in_projection(
            x2, mod3, norm_mix_g[l].reshape(1, d), w_in[l].astype(BF16), tabs,
            jnp.tile(qk_gain_a[l], (1, rep)), jnp.tile(qk_gain_b[l], (1, rep)),
            gate_bias[l].reshape(1, GATE_COLS), s)

        lam_init = 0.8 - 0.6 * math.exp(-0.3 * l)
        oa = _diff_attention(qa.reshape(b, s, WIDTH_A), ka.reshape(b, s, WIDTH_A),
                             va.reshape(b, s, WIDTH_A), lambda_a[l], subln_g_a[l].reshape(VDIM_A, 1),
                             lam_init)
        ob = _dilated_attention(qbs, kbs, vbs)

        x2 = _post_attention(x2, mod3, oa.reshape(b * s, WIDTH_A), ob, gates,
                             w_branch_a[l].astype(BF16), w_branch_b[l].astype(BF16), w_out[l].astype(BF16),
                             norm_mlp_g[l].reshape(1, d), w_mlp_up[l].astype(BF16),
                             w_mlp_down[l].astype(BF16), s)
    return x2.reshape(b, s, d)
```

```python
import math
from functools import partial

import jax
import jax.numpy as jnp
from jax import lax
from jax.experimental import pallas as pl
from jax.experimental.pallas import tpu as pltpu

D_MODEL = 1024
HEAD_DIM = 64
HEADS_A = 8
VDIM_A = 2 * HEAD_DIM
WIDTH_A = HEADS_A * VDIM_A
DILATED_GROUPS = ((128, 1), (512, 4), (2048, 16))
N_GROUPS_B = 3
HEADS_PER_GROUP_B = 4
WIDTH_B = HEADS_PER_GROUP_B * HEAD_DIM
QKV_B_COLS = N_GROUPS_B * WIDTH_B
D_FF = 4 * D_MODEL
ROPE_THETA = 500000.0
ROT_HALF = HEAD_DIM // 4 // 2
EPS = 1e-6
NEG_INF = -1e30
GATE_COLS = 2 * D_MODEL
IN_COLS = 3 * WIDTH_A + 3 * QKV_B_COLS + GATE_COLS
SM_SCALE = HEAD_DIM ** -0.5
LOG2E = math.log2(math.e)

LANES = 128
PROJ_CHUNK = 256
TOKEN_TILE = 512
VMEM_BYTES = 64 * 1024 * 1024
MIB = 1024 * 1024

BF16 = jnp.bfloat16
F32 = jnp.float32
NT_DIMS = (((1,), (1,)), ((), ()))


def _params(semantics, vmem_mib):
    return pltpu.CompilerParams(dimension_semantics=semantics,
                                vmem_limit_bytes=min(vmem_mib * MIB, VMEM_BYTES - 4 * MIB))


def _resident(shape, index_map):
    return pl.BlockSpec(shape, index_map, pipeline_mode=pl.Buffered(1))


def _rope_kernel(pos_ref, cos_ref, sin_ref):
    pos = pos_ref[...].astype(F32)
    for f in range(ROT_HALF):
        ang = pos * (ROPE_THETA ** (-(2.0 * f) / (2 * ROT_HALF)))
        cos_ref[f] = jnp.cos(ang)
        sin_ref[f] = jnp.sin(ang)


def _rope_tables(positions):
    n = positions.size
    pos2 = positions.reshape(n // LANES, LANES)
    cos, sin = pl.pallas_call(
        _rope_kernel,
        out_shape=[jax.ShapeDtypeStruct((ROT_HALF, n // LANES, LANES), F32)] * 2,
    )(pos2)
    cos = cos.reshape(ROT_HALF, n).T
    sin = sin.reshape(ROT_HALF, n).T
    one = jnp.ones((n, HEAD_DIM - 2 * ROT_HALF), F32)
    zero = jnp.zeros((n, HEAD_DIM - 2 * ROT_HALF), F32)
    z8 = jnp.zeros((n, ROT_HALF), F32)
    c_tab = jnp.concatenate([cos, cos, one], axis=1)
    s1_tab = jnp.concatenate([z8, sin, zero], axis=1)
    s2_tab = jnp.concatenate([sin, z8, zero], axis=1)
    rep = LANES // HEAD_DIM
    return tuple(jnp.tile(t, (1, rep)) for t in (c_tab, s1_tab, s2_tab))


def _ada_kernel(c_ref, w_ref, b_ref, o_ref):
    c = c_ref[...]
    cond = (c * jax.nn.sigmoid(c)).astype(BF16)
    o_ref[0] = jnp.dot(cond, w_ref[0].astype(BF16), preferred_element_type=F32) + b_ref[0]


def _ada_mod(c, ada_w, ada_b):
    depth, d, n = ada_w.shape
    b = c.shape[0]
    tn = 768
    return pl.pallas_call(
        _ada_kernel,
        grid=(depth, n // tn),
        in_specs=[pl.BlockSpec((b, d), lambda l, j: (0, 0)),
                  pl.BlockSpec((1, d, tn), lambda l, j: (l, 0, j)),
                  pl.BlockSpec((1, 1, tn), lambda l, j: (l, 0, j))],
        out_specs=pl.BlockSpec((1, b, tn), lambda l, j: (l, 0, j)),
        out_shape=jax.ShapeDtypeStruct((depth, b, n), F32),
        compiler_params=_params(("parallel", "parallel"), 32),
    )(c, ada_w, ada_b.reshape(depth, 1, n))


def _head_norm_rope(acc, ssum, gain, c_tab, s1_tab, s2_tab):
    y = acc * lax.rsqrt(ssum * (1.0 / HEAD_DIM) + EPS)
    halves = []
    for j in range(0, acc.shape[1], LANES):
        yj = y[:, j:j + LANES] * gain
        halves.append(yj * c_tab + pltpu.roll(yj, ROT_HALF, 1) * s1_tab
                      - pltpu.roll(yj, LANES - ROT_HALF, 1) * s2_tab)
    return halves


def _inproj_kernel(x_ref, mod_ref, g_ref, w_ref, c_ref, s1_ref, s2_ref, gain_a_ref, gain_b_ref,
                   gbias_ref, ones_ref, qa_ref, ka_ref, va_ref, *rest):
    qb_refs, kb_refs, vb_refs = rest[0:3], rest[3:6], rest[6:9]
    gates_ref, h_scr, y_scr = rest[9:12]
    tm = x_ref.shape[0]
    x = x_ref[...]
    ms = jnp.mean(x * x, axis=-1, keepdims=True)
    shift = mod_ref[0, :, 0:D_MODEL]
    scale = mod_ref[0, :, D_MODEL:2 * D_MODEL]
    h = x * lax.rsqrt(ms + EPS) * g_ref[...] * (1.0 + scale) + shift
    h_scr[...] = h.astype(BF16)

    c_tab, s1_tab, s2_tab = c_ref[...], s1_ref[...], s2_ref[...]
    cw = PROJ_CHUNK

    def split_residues(out_ref, dilation):
        for r in range(dilation):
            for j in range(WIDTH_B // LANES):
                out_ref[0, r, :, j * LANES:(j + 1) * LANES] = (
                    y_scr[j, pl.ds(r, tm // dilation, stride=dilation), :].astype(BF16))

    tasks = []

    def normed(out_ref, width, gain):
        def epilogue(acc, ssum, c):
            for j, half in enumerate(_head_norm_rope(acc, ssum, gain, c_tab, s1_tab, s2_tab)):
                out_ref[:, c + j * LANES:c + (j + 1) * LANES] = half.astype(BF16)
        tasks.extend((True, partial(epilogue, c=c)) for c in range(0, width, cw))

    def plain(out_ref, width):
        def epilogue(acc, ssum, c):
            out_ref[:, c:c + cw] = acc.astype(BF16)
        tasks.extend((False, partial(epilogue, c=c)) for c in range(0, width, cw))

    def normed_groups(out_refs, gain):
        def epilogue(acc, ssum, g):
            for j, half in enumerate(_head_norm_rope(acc, ssum, gain, c_tab, s1_tab, s2_tab)):
                y_scr[j] = half
            split_residues(out_refs[g], DILATED_GROUPS[g][1])
        tasks.extend((True, partial(epilogue, g=g)) for g in range(N_GROUPS_B))

    def plain_groups(out_refs):
        def epilogue(acc, ssum, g):
            for j in range(0, WIDTH_B, LANES):
                y_scr[j // LANES] = acc[:, j:j + LANES]
            split_residues(out_refs[g], DILATED_GROUPS[g][1])
        tasks.extend((False, partial(epilogue, g=g)) for g in range(N_GROUPS_B))

    def gate(acc, ssum, c):
        gates_ref[:, c:c + cw] = jax.nn.sigmoid(acc + gbias_ref[:, c:c + cw]).astype(BF16)

    normed(qa_ref, WIDTH_A, gain_a_ref[0:1, :] * (SM_SCALE * LOG2E))
    normed(ka_ref, WIDTH_A, gain_a_ref[1:2, :])
    plain(va_ref, WIDTH_A)
    normed_groups(qb_refs, gain_b_ref[0:1, :] * SM_SCALE)
    normed_groups(kb_refs, gain_b_ref[1:2, :])
    plain_groups(vb_refs)
    tasks.extend((False, partial(gate, c=c)) for c in range(0, GATE_COLS, cw))
    assert len(tasks) * cw == IN_COLS and WIDTH_B == cw

    def proj(n):
        return jnp.dot(h_scr[...], w_ref[:, n * cw:(n + 1) * cw], preferred_element_type=F32)

    def head_sums(n, acc):
        if not tasks[n][0]:
            return None
        return jnp.dot((acc * acc).astype(BF16), ones_ref[...], preferred_element_type=F32)

    last = len(tasks) - 1
    accs = {0: proj(0), 1: proj(1)}
    sums = {0: head_sums(0, accs[0])}
    for n in range(len(tasks)):
        if n + 2 <= last:
            accs[n + 2] = proj(n + 2)
        if n + 1 <= last:
            sums[n + 1] = head_sums(n + 1, accs[n + 1])
        tasks[n][1](accs.pop(n), sums.pop(n))


def _in_projection(x2, mod3, g, w_bf, tabs, gain_a, gain_b, gate_bias, seq):
    n, d = x2.shape
    tm = TOKEN_TILE
    tiles = seq // tm
    heads = jnp.arange(PROJ_CHUNK) // HEAD_DIM
    seg_ones = (heads[:, None] == heads[None, :]).astype(BF16)
    row = lambda i: (i, 0)
    fixed = lambda i: (0, 0)
    res_block = lambda i: (i // tiles, 0, i % tiles, 0)
    wide = [pl.BlockSpec((tm, WIDTH_A), row)] * 3
    wide_shape = [jax.ShapeDtypeStruct((n, WIDTH_A), BF16)] * 3
    grp = [pl.BlockSpec((1, dil, tm // dil, WIDTH_B), res_block) for _, dil in DILATED_GROUPS] * 3
    grp_shape = [jax.ShapeDtypeStruct((n // seq, dil, seq // dil, WIDTH_B), BF16) for _, dil in DILATED_GROUPS] * 3
    outs = pl.pallas_call(
        _inproj_kernel,
        grid=(n // tm,),
        in_specs=[pl.BlockSpec((tm, d), row),
                  pl.BlockSpec((1, 1, mod3.shape[2]), lambda i: (i // tiles, 0, 0)),
                  _resident((1, d), fixed),
                  _resident((d, IN_COLS), fixed),
                  pl.BlockSpec((tm, LANES), row),
                  pl.BlockSpec((tm, LANES), row),
                  pl.BlockSpec((tm, LANES), row),
                  _resident((2, LANES), fixed),
                  _resident((2, LANES), fixed),
                  _resident((1, GATE_COLS), fixed),
                  _resident((PROJ_CHUNK, PROJ_CHUNK), fixed)],
        out_specs=wide + grp + [pl.BlockSpec((tm, GATE_COLS), row)],
        out_shape=wide_shape + grp_shape + [jax.ShapeDtypeStruct((n, GATE_COLS), BF16)],
        scratch_shapes=[pltpu.VMEM((tm, d), BF16), pltpu.VMEM((WIDTH_B // LANES, tm, LANES), F32)],
        compiler_params=_params(("parallel",), 56),
        name="in_proj",
    )(x2, mod3, g, w_bf, *tabs, gain_a, gain_b, gate_bias, seg_ones)
    return outs[0], outs[1], outs[2], outs[3:6], outs[6:9], outs[9:12], outs[12]


DA_TQ = 256
DA_BLOCKS = 2
DA_TK = 512
DA_FINISH_AFTER = 1024
DA_ONES_ROWS = 16


def _diff_attn_kernel(lam_ref, g_ref, q_ref, k_ref, v_ref, o_ref, sa_scr, sb_scr, ma_scr, mb_scr,
                      acc_scr, vt_scr, *, nq, lam_init):
    seq = k_ref.shape[1]
    tq = DA_TQ
    t = pl.program_id(0)

    def load_values():
        vt_scr[0:VDIM_A, :] = v_ref[0].T
        vt_scr[VDIM_A:VDIM_A + DA_ONES_ROWS, :] = jnp.ones((DA_ONES_ROWS, seq), BF16)

    @pl.when(t == 0)
    def _():
        sb_scr[...] = jnp.zeros(sb_scr.shape, F32)
        mb_scr[...] = jnp.zeros(mb_scr.shape, F32)
        acc_scr[...] = jnp.ones(acc_scr.shape, F32)
        load_values()

    def finish(acc_r, rows):
        o = acc_r[0:VDIM_A, :] / acc_r[VDIM_A:VDIM_A + 1, :]
        lv = lam_ref[...]
        lam = (jnp.exp(jnp.sum(lv[0:1] * lv[1:2], axis=-1, keepdims=True))
               - jnp.exp(jnp.sum(lv[2:3] * lv[3:4], axis=-1, keepdims=True)) + lam_init)
        out = o[:, :tq] - lam * o[:, tq:]
        ms = jnp.mean(out * out, axis=0, keepdims=True)
        y = out * lax.rsqrt(ms + EPS) * g_ref[...] * (1.0 - lam_init)
        o_ref[0, rows, :] = y.T.astype(BF16)

    def step(rows, s_w, m_w, acc_w, s_r, m_r, acc_r):
        q = q_ref[0, rows, :]
        lane = lax.broadcasted_iota(jnp.int32, q.shape, 1)
        zero = jnp.zeros_like(q)
        qs = jnp.concatenate([jnp.where(lane < HEAD_DIM, q, zero),
                              jnp.where(lane < HEAD_DIM, zero, q)], axis=0)
        m_prev = m_r[0:1]
        m_new = None
        acc = None
        for c in range(0, seq, DA_TK):
            st = lax.dot_general(k_ref[0, c:c + DA_TK, :], qs, NT_DIMS, preferred_element_type=F32)
            s_w[c:c + DA_TK, :] = st
            mc = jnp.max(st, axis=0, keepdims=True)
            m_new = mc if m_new is None else jnp.maximum(m_new, mc)
            pt = jnp.exp2((s_r[c:c + DA_TK, :] - m_prev).astype(BF16))
            part = jnp.dot(vt_scr[:, c:c + DA_TK], pt, preferred_element_type=F32)
            acc = part if acc is None else acc + part
            if c == DA_FINISH_AFTER:
                finish(acc_r, rows)
        m_w[...] = jnp.broadcast_to(m_new, m_w.shape)
        acc_w[...] = acc

    score_bufs = ((sa_scr, ma_scr), (sb_scr, mb_scr))
    for j in range(DA_BLOCKS):
        (s_w, m_w), (s_r, m_r) = score_bufs[j % 2], score_bufs[1 - j % 2]
        step(slice(j * tq, (j + 1) * tq), s_w, m_w, acc_scr.at[(j - 1) % DA_BLOCKS], s_r, m_r, acc_scr.at[j])
        if j == 0:
            @pl.when((t > 0) & ((DA_BLOCKS * t) % nq == 0))
            def _():
                load_values()


def _diff_attention(qa, ka, va, lam_p, subln_g, lam_init):
    b, s, _ = qa.shape
    nq = s // DA_TQ
    assert nq % DA_BLOCKS == 0 and DA_BLOCKS % 2 == 0
    per_head = nq // DA_BLOCKS
    pairs = b * HEADS_A * per_head

    def pair(t):
        t = jnp.clip(t, 0, pairs - 1)
        return t // (HEADS_A * per_head), (t // per_head) % HEADS_A, t % per_head

    def q_map(t):
        bi, h, i = pair(t)
        return bi, i, h

    def kv_map(t):
        bi, h, _ = pair(t)
        return bi, 0, h

    def o_map(t):
        bi, h, i = pair(t - 1)
        return bi, i, h

    score_buf = pltpu.VMEM((s, 2 * DA_TQ), F32)
    max_buf = pltpu.VMEM((8, 2 * DA_TQ), F32)
    acc_buf = pltpu.VMEM((DA_BLOCKS, VDIM_A + DA_ONES_ROWS, 2 * DA_TQ), F32)
    return pl.pallas_call(
        partial(_diff_attn_kernel, nq=nq, lam_init=lam_init),
        grid=(pairs + 1,),
        in_specs=[pl.BlockSpec(lam_p.shape, lambda t: (0, 0)),
                  pl.BlockSpec((VDIM_A, 1), lambda t: (0, 0)),
                  pl.BlockSpec((1, DA_BLOCKS * DA_TQ, VDIM_A), q_map),
                  pl.BlockSpec((1, s, VDIM_A), kv_map),
                  pl.BlockSpec((1, s, VDIM_A), kv_map)],
        out_specs=pl.BlockSpec((1, DA_BLOCKS * DA_TQ, VDIM_A), o_map),
        out_shape=jax.ShapeDtypeStruct((b, s, WIDTH_A), BF16),
        scratch_shapes=[score_buf, score_buf, max_buf, max_buf, acc_buf,
                        pltpu.VMEM((VDIM_A + DA_ONES_ROWS, s), BF16)],
        compiler_params=_params(("arbitrary",), 40),
        name="diff_attn",
    )(lam_p, subln_g, qa, ka, va)


DIL_TQ = 128
DIL_SKEW = 2
DIL_WIN = 256


def _band_scores(q, kw):
    head = lax.broadcasted_iota(jnp.int32, q.shape, 1) // HEAD_DIM
    zero = jnp.zeros_like(q)
    qs = jnp.concatenate([jnp.where(head == h, q, zero) for h in range(HEADS_PER_GROUP_B)], axis=0)
    return lax.dot_general(qs, kw, NT_DIMS, preferred_element_type=F32)


def _band_softmax(s, rel, offset, side):
    s = jnp.where(jnp.abs(rel + offset) <= side, s, NEG_INF)
    m = jnp.max(s, axis=-1, keepdims=True)
    p = jnp.exp(s - m)
    l = jnp.sum(p, axis=-1, keepdims=True)
    return p.astype(BF16), m, l


def _band_output(p, m, l, vw, nq):
    o = jnp.dot(p, vw, preferred_element_type=F32) / l
    lse = jnp.broadcast_to(m + jnp.log(l), o.shape)
    head = lax.broadcasted_iota(jnp.int32, (nq, WIDTH_B), 1) // HEAD_DIM
    o_sel = jnp.zeros((nq, WIDTH_B), F32)
    lse_sel = jnp.zeros((nq, WIDTH_B), F32)
    for h in range(HEADS_PER_GROUP_B):
        rs = slice(h * nq, (h + 1) * nq)
        o_sel = jnp.where(head == h, o[rs], o_sel)
        lse_sel = jnp.where(head == h, lse[rs], lse_sel)
    return o_sel, lse_sel


def _dilated_kernel(*refs):
    q_refs, k_refs, v_refs = refs[0:3], refs[3:6], refs[6:9]
    ob_ref, o_scr, lse_scr = refs[9:12]
    tile = ob_ref.shape[0]
    i = pl.program_id(1)

    blocks = []
    for g, (window, dil) in enumerate(DILATED_GROUPS):
        side = window // (2 * dil)
        length = k_refs[g].shape[2]
        per_res = tile // dil
        nq = min(per_res, DIL_TQ)
        rows = lax.broadcasted_iota(jnp.int32, (HEADS_PER_GROUP_B * nq, DIL_WIN), 0)
        cols = lax.broadcasted_iota(jnp.int32, (HEADS_PER_GROUP_B * nq, DIL_WIN), 1)
        rel = (rows & (nq - 1)) - cols
        for r in range(dil):
            for u in range(per_res // nq):
                t0 = i * per_res + u * nq
                if length == DIL_WIN:
                    ks = 0
                else:
                    ks = pl.multiple_of(jnp.clip(t0 - side, 0, length - DIL_WIN), side)
                blocks.append(dict(g=g, dil=dil, r=r, u=u, nq=nq, t0=t0, ks=ks, side=side, rel=rel))

    def scores(bk):
        q = q_refs[bk["g"]][0, bk["r"], bk["u"] * bk["nq"]:(bk["u"] + 1) * bk["nq"], :]
        kw = k_refs[bk["g"]][0, bk["r"], pl.ds(bk["ks"], DIL_WIN), :]
        bk["s"] = _band_scores(q, kw)

    def softmax(bk):
        bk["pml"] = _band_softmax(bk.pop("s"), bk["rel"], bk["t0"] - bk["ks"], bk["side"])

    def output(bk):
        g, dil, r, u, nq = bk["g"], bk["dil"], bk["r"], bk["u"], bk["nq"]
        vw = v_refs[g][0, r, pl.ds(bk["ks"], DIL_WIN), :]
        o, lse = _band_output(*bk.pop("pml"), vw, nq)
        rows = pl.ds(u * nq * dil + r, nq, stride=dil) if dil > 1 else pl.ds(u * nq, nq)
        for j in range(WIDTH_B // LANES):
            o_scr[g, j, rows, :] = o[:, j * LANES:(j + 1) * LANES]
            lse_scr[g, j, rows, :] = lse[:, j * LANES:(j + 1) * LANES]

    for n in range(len(blocks) + 2 * DIL_SKEW):
        if n < len(blocks):
            scores(blocks[n])
        if 0 <= n - DIL_SKEW < len(blocks):
            softmax(blocks[n - DIL_SKEW])
        if 0 <= n - 2 * DIL_SKEW < len(blocks):
            output(blocks[n - 2 * DIL_SKEW])

    for j in range(WIDTH_B // LANES):
        l0, l1, l2 = lse_scr[0, j], lse_scr[1, j], lse_scr[2, j]
        m = jnp.maximum(jnp.maximum(l0, l1), l2)
        e0, e1, e2 = jnp.exp(l0 - m), jnp.exp(l1 - m), jnp.exp(l2 - m)
        ob = (e0 * o_scr[0, j] + e1 * o_scr[1, j] + e2 * o_scr[2, j]) / (e0 + e1 + e2)
        ob_ref[:, j * LANES:(j + 1) * LANES] = ob.astype(BF16)


def _dilated_attention(qs, ks, vs):
    b = qs[0].shape[0]
    seq = qs[0].shape[1] * qs[0].shape[2]
    tile = TOKEN_TILE
    tiles = seq // tile
    q_specs = [pl.BlockSpec((1, dil, tile // dil, WIDTH_B), lambda bi, i: (bi, 0, i, 0))
               for _, dil in DILATED_GROUPS]
    kv_specs = [pl.BlockSpec((1, dil, seq // dil, WIDTH_B), lambda bi, i: (bi, 0, 0, 0),
                             pipeline_mode=pl.Buffered(1))
                for _, dil in DILATED_GROUPS]
    return pl.pallas_call(
        _dilated_kernel,
        grid=(b, tiles),
        in_specs=q_specs + kv_specs + kv_specs,
        out_specs=pl.BlockSpec((tile, WIDTH_B), lambda bi, i: (bi * tiles + i, 0)),
        out_shape=jax.ShapeDtypeStruct((b * seq, WIDTH_B), BF16),
        scratch_shapes=[pltpu.VMEM((N_GROUPS_B, WIDTH_B // LANES, tile, LANES), F32)] * 2,
        compiler_params=_params(("parallel", "arbitrary"), 48),
        name="dilated_attn",
    )(*qs, *ks, *vs)


def _post_kernel(x_ref, mod_ref, oa_ref, ob_ref, gates_ref, wa_ref, wb_ref, wo_ref, g_ref, wu_ref, wd_ref,
                 out_ref):
    br_a = jnp.dot(oa_ref[...], wa_ref[...], preferred_element_type=F32)
    br_b = jnp.dot(ob_ref[...], wb_ref[...], preferred_element_type=F32)
    g_a = gates_ref[:, 0:D_MODEL].astype(F32)
    g_b = gates_ref[:, D_MODEL:GATE_COLS].astype(F32)
    y = jnp.dot((g_a * br_a + g_b * br_b).astype(BF16), wo_ref[...], preferred_element_type=F32)
    mod = lambda k: mod_ref[0, :, k * D_MODEL:(k + 1) * D_MODEL]
    x = x_ref[...] + mod(2) * y

    ms = jnp.mean(x * x, axis=-1, keepdims=True)
    h = (x * lax.rsqrt(ms + EPS) * g_ref[...] * (1.0 + mod(4)) + mod(3)).astype(BF16)
    u = jnp.maximum(jnp.dot(h, wu_ref[...], preferred_element_type=F32), 0.0)
    dn = jnp.dot((u * u).astype(BF16), wd_ref[...], preferred_element_type=F32)
    out_ref[...] = x + mod(5) * dn


def _post_attention(x2, mod3, oa, ob, gates, wa, wb, wo, g, wu, wd, seq):
    n, d = x2.shape
    tm = TOKEN_TILE
    row = lambda i: (i, 0)
    fixed = lambda i: (0, 0)
    return pl.pallas_call(
        _post_kernel,
        grid=(n // tm,),
        in_specs=[pl.BlockSpec((tm, d), row),
                  pl.BlockSpec((1, 1, mod3.shape[2]), lambda i: ((i * tm) // seq, 0, 0)),
                  pl.BlockSpec((tm, WIDTH_A), row),
                  pl.BlockSpec((tm, WIDTH_B), row),
                  pl.BlockSpec((tm, GATE_COLS), row),
                  _resident(wa.shape, fixed), _resident(wb.shape, fixed), _resident(wo.shape, fixed),
                  _resident((1, d), fixed), _resident(wu.shape, fixed), _resident(wd.shape, fixed)],
        out_specs=pl.BlockSpec((tm, d), row),
        out_shape=jax.ShapeDtypeStruct((n, d), F32),
        compiler_params=_params(("parallel",), 60),
        name="post_attn",
    )(x2, mod3, oa, ob, gates, wa, wb, wo, g, wu, wd)


def kernel(x, c, positions, ada_w, ada_b, norm_mix_g, norm_mlp_g, w_in, qk_gain_a, lambda_a, subln_g_a,
           qk_gain_b, w_branch_a, w_branch_b, gate_bias, w_out, w_mlp_up, w_mlp_down):
    b, s, d = x.shape
    depth = ada_w.shape[0]
    tabs = _rope_tables(positions)
    mod = _ada_mod(c, ada_w, ada_b)
    rep = LANES // HEAD_DIM
    x2 = x.reshape(b * s, d)
    for l in range(depth):
        mod3 = mod[l].reshape(b, 1, 6 * d)
        qa, ka, va, qbs, kbs, vbs, gates = _in_projection(
            x2, mod3, norm_mix_g[l].reshape(1, d), w_in[l].astype(BF16), tabs,
            jnp.tile(qk_gain_a[l], (1, rep)), jnp.tile(qk_gain_b[l], (1, rep)),
            gate_bias[l].reshape(1, GATE_COLS), s)

        lam_init = 0.8 - 0.6 * math.exp(-0.3 * l)
        oa = _diff_attention(qa.reshape(b, s, WIDTH_A), ka.reshape(b, s, WIDTH_A),
                             va.reshape(b, s, WIDTH_A), lambda_a[l], subln_g_a[l].reshape(VDIM_A, 1),
                             lam_init)
        ob = _dilated_attention(qbs, kbs, vbs)

        x2 = _post_attention(x2, mod3, oa.reshape(b * s, WIDTH_A), ob, gates,
                             w_branch_a[l].astype(BF16), w_branch_b[l].astype(BF16), w_out[l].astype(BF16),
                             norm_mlp_g[l].reshape(1, d), w_mlp_up[l].astype(BF16),
                             w_mlp_down[l].astype(BF16), s)
    return x2.reshape(b, s, d)
```

```python
import math
from functools import partial

import jax
import jax.numpy as jnp
from jax import lax
from jax.experimental import pallas as pl
from jax.experimental.pallas import tpu as pltpu

D_MODEL = 1024
HEAD_DIM = 64
HEADS_A = 8
VDIM_A = 2 * HEAD_DIM
WIDTH_A = HEADS_A * VDIM_A
DILATED_GROUPS = ((128, 1), (512, 4), (2048, 16))
N_GROUPS_B = 3
HEADS_PER_GROUP_B = 4
WIDTH_B = HEADS_PER_GROUP_B * HEAD_DIM
QKV_B_COLS = N_GROUPS_B * WIDTH_B
D_FF = 4 * D_MODEL
ROPE_THETA = 500000.0
ROT_HALF = HEAD_DIM // 4 // 2
EPS = 1e-6
NEG_INF = -1e30
GATE_COLS = 2 * D_MODEL
IN_COLS = 3 * WIDTH_A + 3 * QKV_B_COLS + GATE_COLS
SM_SCALE = HEAD_DIM ** -0.5
LOG2E = math.log2(math.e)

LANES = 128
PROJ_CHUNK = 256
TOKEN_TILE = 512
VMEM_BYTES = 64 * 1024 * 1024
MIB = 1024 * 1024

BF16 = jnp.bfloat16
F32 = jnp.float32
NT_DIMS = (((1,), (1,)), ((), ()))


def _params(semantics, vmem_mib):
    return pltpu.CompilerParams(dimension_semantics=semantics,
                                vmem_limit_bytes=min(vmem_mib * MIB, VMEM_BYTES - 4 * MIB))


def _resident(shape, index_map):
    return pl.BlockSpec(shape, index_map, pipeline_mode=pl.Buffered(1))


def _rope_kernel(pos_ref, cos_ref, sin_ref):
    pos = pos_ref[...].astype(F32)
    for f in range(ROT_HALF):
        ang = pos * (ROPE_THETA ** (-(2.0 * f) / (2 * ROT_HALF)))
        cos_ref[f] = jnp.cos(ang)
        sin_ref[f] = jnp.sin(ang)


def _rope_tables(positions):
    n = positions.size
    pos2 = positions.reshape(n // LANES, LANES)
    cos, sin = pl.pallas_call(
        _rope_kernel,
        out_shape=[jax.ShapeDtypeStruct((ROT_HALF, n // LANES, LANES), F32)] * 2,
    )(pos2)
    cos = cos.reshape(ROT_HALF, n).T
    sin = sin.reshape(ROT_HALF, n).T
    one = jnp.ones((n, HEAD_DIM - 2 * ROT_HALF), F32)
    zero = jnp.zeros((n, HEAD_DIM - 2 * ROT_HALF), F32)
    z8 = jnp.zeros((n, ROT_HALF), F32)
    c_tab = jnp.concatenate([cos, cos, one], axis=1)
    s1_tab = jnp.concatenate([z8, sin, zero], axis=1)
    s2_tab = jnp.concatenate([sin, z8, zero], axis=1)
    rep = LANES // HEAD_DIM
    return tuple(jnp.tile(t, (1, rep)) for t in (c_tab, s1_tab, s2_tab))


def _ada_kernel(c_ref, w_ref, b_ref, o_ref):
    c = c_ref[...]
    cond = (c * jax.nn.sigmoid(c)).astype(BF16)
    o_ref[0] = jnp.dot(cond, w_ref[0].astype(BF16), preferred_element_type=F32) + b_ref[0]


def _ada_mod(c, ada_w, ada_b):
    depth, d, n = ada_w.shape
    b = c.shape[0]
    tn = 768
    return pl.pallas_call(
        _ada_kernel,
        grid=(depth, n // tn),
        in_specs=[pl.BlockSpec((b, d), lambda l, j: (0, 0)),
                  pl.BlockSpec((1, d, tn), lambda l, j: (l, 0, j)),
                  pl.BlockSpec((1, 1, tn), lambda l, j: (l, 0, j))],
        out_specs=pl.BlockSpec((1, b, tn), lambda l, j: (l, 0, j)),
        out_shape=jax.ShapeDtypeStruct((depth, b, n), F32),
        compiler_params=_params(("parallel", "parallel"), 32),
    )(c, ada_w, ada_b.reshape(depth, 1, n))


def _head_norm_rope(acc, ssum, gain, c_tab, s1_tab, s2_tab):
    y = acc * lax.rsqrt(ssum * (1.0 / HEAD_DIM) + EPS)
    halves = []
    for j in range(0, acc.shape[1], LANES):
        yj = y[:, j:j + LANES] * gain
        halves.append(yj * c_tab + pltpu.roll(yj, ROT_HALF, 1) * s1_tab
                      - pltpu.roll(yj, LANES - ROT_HALF, 1) * s2_tab)
    return halves


def _inproj_kernel(x_ref, mod_ref, g_ref, w_ref, c_ref, s1_ref, s2_ref, gain_a_ref, gain_b_ref,
                   gbias_ref, ones_ref, qa_ref, ka_ref, va_ref, *rest):
    qb_refs, kb_refs, vb_refs = rest[0:3], rest[3:6], rest[6:9]
    gates_ref, h_scr, y_scr = rest[9:12]
    tm = x_ref.shape[0]
    x = x_ref[...]
    ms = jnp.mean(x * x, axis=-1, keepdims=True)
    shift = mod_ref[0, :, 0:D_MODEL]
    scale = mod_ref[0, :, D_MODEL:2 * D_MODEL]
    h = x * lax.rsqrt(ms + EPS) * g_ref[...] * (1.0 + scale) + shift
    h_scr[...] = h.astype(BF16)

    c_tab, s1_tab, s2_tab = c_ref[...], s1_ref[...], s2_ref[...]
    cw = PROJ_CHUNK

    def split_residues(out_ref, dilation):
        for r in range(dilation):
            for j in range(WIDTH_B // LANES):
                out_ref[0, r, :, j * LANES:(j + 1) * LANES] = (
                    y_scr[j, pl.ds(r, tm // dilation, stride=dilation), :].astype(BF16))

    tasks = []

    def normed(out_ref, width, gain):
        def epilogue(acc, ssum, c):
            for j, half in enumerate(_head_norm_rope(acc, ssum, gain, c_tab, s1_tab, s2_tab)):
                out_ref[:, c + j * LANES:c + (j + 1) * LANES] = half.astype(BF16)
        tasks.extend((True, partial(epilogue, c=c)) for c in range(0, width, cw))

    def plain(out_ref, width):
        def epilogue(acc, ssum, c):
            out_ref[:, c:c + cw] = acc.astype(BF16)
        tasks.extend((False, partial(epilogue, c=c)) for c in range(0, width, cw))

    def normed_groups(out_refs, gain):
        def epilogue(acc, ssum, g):
            for j, half in enumerate(_head_norm_rope(acc, ssum, gain, c_tab, s1_tab, s2_tab)):
                y_scr[j] = half
            split_residues(out_refs[g], DILATED_GROUPS[g][1])
        tasks.extend((True, partial(epilogue, g=g)) for g in range(N_GROUPS_B))

    def plain_groups(out_refs):
        def epilogue(acc, ssum, g):
            for j in range(0, WIDTH_B, LANES):
                y_scr[j // LANES] = acc[:, j:j + LANES]
            split_residues(out_refs[g], DILATED_GROUPS[g][1])
        tasks.extend((False, partial(epilogue, g=g)) for g in range(N_GROUPS_B))

    def gate(acc, ssum, c):
        gates_ref[:, c:c + cw] = jax.nn.sigmoid(acc + gbias_ref[:, c:c + cw]).astype(BF16)

    normed(qa_ref, WIDTH_A, gain_a_ref[0:1, :] * (SM_SCALE * LOG2E))
    normed(ka_ref, WIDTH_A, gain_a_ref[1:2, :])
    plain(va_ref, WIDTH_A)
    normed_groups(qb_refs, gain_b_ref[0:1, :] * SM_SCALE)
    normed_groups(kb_refs, gain_b_ref[1:2, :])
    plain_groups(vb_refs)
    tasks.extend((False, partial(gate, c=c)) for c in range(0, GATE_COLS, cw))
    assert len(tasks) * cw == IN_COLS and WIDTH_B == cw

    def proj(n):
        return jnp.dot(h_scr[...], w_ref[:, n * cw:(n + 1) * cw], preferred_element_type=F32)

    def head_sums(n, acc):
        if not tasks[n][0]:
            return None
        return jnp.dot((acc * acc).astype(BF16), ones_ref[...], preferred_element_type=F32)

    last = len(tasks) - 1
    accs = {0: proj(0), 1: proj(1)}
    sums = {0: head_sums(0, accs[0])}
    for n in range(len(tasks)):
        if n + 2 <= last:
            accs[n + 2] = proj(n + 2)
        if n + 1 <= last:
            sums[n + 1] = head_sums(n + 1, accs[n + 1])
        tasks[n][1](accs.pop(n), sums.pop(n))


def _in_projection(x2, mod3, g, w_bf, tabs, gain_a, gain_b, gate_bias, seq):
    n, d = x2.shape
    tm = TOKEN_TILE
    tiles = seq // tm
    heads = jnp.arange(PROJ_CHUNK) // HEAD_DIM
    seg_ones = (heads[:, None] == heads[None, :]).astype(BF16)
    row = lambda i: (i, 0)
    fixed = lambda i: (0, 0)
    res_block = lambda i: (i // tiles, 0, i % tiles, 0)
    wide = [pl.BlockSpec((tm, WIDTH_A), row)] * 3
    wide_shape = [jax.ShapeDtypeStruct((n, WIDTH_A), BF16)] * 3
    grp = [pl.BlockSpec((1, dil, tm // dil, WIDTH_B), res_block) for _, dil in DILATED_GROUPS] * 3
    grp_shape = [jax.ShapeDtypeStruct((n // seq, dil, seq // dil, WIDTH_B), BF16) for _, dil in DILATED_GROUPS] * 3
    outs = pl.pallas_call(
        _inproj_kernel,
        grid=(n // tm,),
        in_specs=[pl.BlockSpec((tm, d), row),
                  pl.BlockSpec((1, 1, mod3.shape[2]), lambda i: (i // tiles, 0, 0)),
                  _resident((1, d), fixed),
                  _resident((d, IN_COLS), fixed),
                  pl.BlockSpec((tm, LANES), row),
                  pl.BlockSpec((tm, LANES), row),
                  pl.BlockSpec((tm, LANES), row),
                  _resident((2, LANES), fixed),
                  _resident((2, LANES), fixed),
                  _resident((1, GATE_COLS), fixed),
                  _resident((PROJ_CHUNK, PROJ_CHUNK), fixed)],
        out_specs=wide + grp + [pl.BlockSpec((tm, GATE_COLS), row)],
        out_shape=wide_shape + grp_shape + [jax.ShapeDtypeStruct((n, GATE_COLS), BF16)],
        scratch_shapes=[pltpu.VMEM((tm, d), BF16), pltpu.VMEM((WIDTH_B // LANES, tm, LANES), F32)],
        compiler_params=_params(("parallel",), 56),
        name="in_proj",
    )(x2, mod3, g, w_bf, *tabs, gain_a, gain_b, gate_bias, seg_ones)
    return outs[0], outs[1], outs[2], outs[3:6], outs[6:9], outs[9:12], outs[12]


DA_TQ = 256
DA_BLOCKS = 16
DA_TK = 512
DA_FINISH_AFTER = 1024
DA_ONES_ROWS = 16


def _diff_attn_kernel(lam_ref, g_ref, q_ref, k_ref, v_ref, o_ref, sa_scr, sb_scr, ma_scr, mb_scr,
                      acc_scr, vt_scr, *, nq, lam_init):
    seq = k_ref.shape[1]
    tq = DA_TQ
    t = pl.program_id(0)

    def load_values():
        vt_scr[0:VDIM_A, :] = v_ref[0].T
        vt_scr[VDIM_A:VDIM_A + DA_ONES_ROWS, :] = jnp.ones((DA_ONES_ROWS, seq), BF16)

    @pl.when(t == 0)
    def _():
        sb_scr[...] = jnp.zeros(sb_scr.shape, F32)
        mb_scr[...] = jnp.zeros(mb_scr.shape, F32)
        acc_scr[...] = jnp.ones(acc_scr.shape, F32)
        load_values()

    def finish(acc_r, rows):
        o = acc_r[0:VDIM_A, :] / acc_r[VDIM_A:VDIM_A + 1, :]
        lv = lam_ref[...]
        lam = (jnp.exp(jnp.sum(lv[0:1] * lv[1:2], axis=-1, keepdims=True))
               - jnp.exp(jnp.sum(lv[2:3] * lv[3:4], axis=-1, keepdims=True)) + lam_init)
        out = o[:, :tq] - lam * o[:, tq:]
        ms = jnp.mean(out * out, axis=0, keepdims=True)
        y = out * lax.rsqrt(ms + EPS) * g_ref[...] * (1.0 - lam_init)
        o_ref[0, rows, :] = y.T.astype(BF16)

    def step(rows, s_w, m_w, acc_w, s_r, m_r, acc_r):
        q = q_ref[0, rows, :]
        lane = lax.broadcasted_iota(jnp.int32, q.shape, 1)
        zero = jnp.zeros_like(q)
        qs = jnp.concatenate([jnp.where(lane < HEAD_DIM, q, zero),
                              jnp.where(lane < HEAD_DIM, zero, q)], axis=0)
        m_prev = m_r[0:1]
        m_new = None
        acc = None
        for c in range(0, seq, DA_TK):
            st = lax.dot_general(k_ref[0, c:c + DA_TK, :], qs, NT_DIMS, preferred_element_type=F32)
            s_w[c:c + DA_TK, :] = st
            mc = jnp.max(st, axis=0, keepdims=True)
            m_new = mc if m_new is None else jnp.maximum(m_new, mc)
            pt = jnp.exp2((s_r[c:c + DA_TK, :] - m_prev).astype(BF16))
            part = jnp.dot(vt_scr[:, c:c + DA_TK], pt, preferred_element_type=F32)
            acc = part if acc is None else acc + part
            if c == DA_FINISH_AFTER:
                finish(acc_r, rows)
        m_w[...] = jnp.broadcast_to(m_new, m_w.shape)
        acc_w[...] = acc

    score_bufs = ((sa_scr, ma_scr), (sb_scr, mb_scr))

    def two_blocks(jj, carry):
        for half in range(2):
            j = 2 * jj + half
            rows = pl.ds(pl.multiple_of(j * tq, tq), tq)
            (s_w, m_w), (s_r, m_r) = score_bufs[half], score_bufs[1 - half]
            step(rows, s_w, m_w, acc_scr.at[(j + DA_BLOCKS - 1) % DA_BLOCKS], s_r, m_r, acc_scr.at[j])
            if half == 0:
                @pl.when((jj == 0) & (t > 0) & ((DA_BLOCKS * t) % nq == 0))
                def _():
                    load_values()
        return carry

    lax.fori_loop(0, DA_BLOCKS // 2, two_blocks, 0)


def _diff_attention(qa, ka, va, lam_p, subln_g, lam_init):
    b, s, _ = qa.shape
    nq = s // DA_TQ
    assert nq % DA_BLOCKS == 0 and DA_BLOCKS % 2 == 0
    per_head = nq // DA_BLOCKS
    pairs = b * HEADS_A * per_head

    def pair(t):
        t = jnp.clip(t, 0, pairs - 1)
        return t // (HEADS_A * per_head), (t // per_head) % HEADS_A, t % per_head

    def q_map(t):
        bi, h, i = pair(t)
        return bi, i, h

    def kv_map(t):
        bi, h, _ = pair(t)
        return bi, 0, h

    def o_map(t):
        bi, h, i = pair(t - 1)
        return bi, i, h

    score_buf = pltpu.VMEM((s, 2 * DA_TQ), F32)
    max_buf = pltpu.VMEM((8, 2 * DA_TQ), F32)
    acc_buf = pltpu.VMEM((DA_BLOCKS, VDIM_A + DA_ONES_ROWS, 2 * DA_TQ), F32)
    return pl.pallas_call(
        partial(_diff_attn_kernel, nq=nq, lam_init=lam_init),
        grid=(pairs + 1,),
        in_specs=[pl.BlockSpec(lam_p.shape, lambda t: (0, 0)),
                  pl.BlockSpec((VDIM_A, 1), lambda t: (0, 0)),
                  pl.BlockSpec((1, DA_BLOCKS * DA_TQ, VDIM_A), q_map),
                  pl.BlockSpec((1, s, VDIM_A), kv_map),
                  pl.BlockSpec((1, s, VDIM_A), kv_map)],
        out_specs=pl.BlockSpec((1, DA_BLOCKS * DA_TQ, VDIM_A), o_map),
        out_shape=jax.ShapeDtypeStruct((b, s, WIDTH_A), BF16),
        scratch_shapes=[score_buf, score_buf, max_buf, max_buf, acc_buf,
                        pltpu.VMEM((VDIM_A + DA_ONES_ROWS, s), BF16)],
        compiler_params=_params(("arbitrary",), 40),
        name="diff_attn",
    )(lam_p, subln_g, qa, ka, va)


DIL_TQ = 128
DIL_SKEW = 1
DIL_WIN = 256


def _band_scores(q, kw):
    head = lax.broadcasted_iota(jnp.int32, q.shape, 1) // HEAD_DIM
    zero = jnp.zeros_like(q)
    qs = jnp.concatenate([jnp.where(head == h, q, zero) for h in range(HEADS_PER_GROUP_B)], axis=0)
    return lax.dot_general(qs, kw, NT_DIMS, preferred_element_type=F32)


def _band_softmax(s, rel, offset, side):
    s = jnp.where(jnp.abs(rel + offset) <= side, s, NEG_INF)
    m = jnp.max(s, axis=-1, keepdims=True)
    p = jnp.exp(s - m)
    l = jnp.sum(p, axis=-1, keepdims=True)
    return p.astype(BF16), m, l


def _band_output(p, m, l, vw, nq):
    o = jnp.dot(p, vw, preferred_element_type=F32) / l
    lse = jnp.broadcast_to(m + jnp.log(l), o.shape)
    head = lax.broadcasted_iota(jnp.int32, (nq, WIDTH_B), 1) // HEAD_DIM
    o_sel = jnp.zeros((nq, WIDTH_B), F32)
    lse_sel = jnp.zeros((nq, WIDTH_B), F32)
    for h in range(HEADS_PER_GROUP_B):
        rs = slice(h * nq, (h + 1) * nq)
        o_sel = jnp.where(head == h, o[rs], o_sel)
        lse_sel = jnp.where(head == h, lse[rs], lse_sel)
    return o_sel, lse_sel


def _dilated_kernel(*refs):
    q_refs, k_refs, v_refs = refs[0:3], refs[3:6], refs[6:9]
    ob_ref, o_scr, lse_scr = refs[9:12]
    tile = ob_ref.shape[0]
    i = pl.program_id(1)

    blocks = []
    for g, (window, dil) in enumerate(DILATED_GROUPS):
        side = window // (2 * dil)
        length = k_refs[g].shape[2]
        per_res = tile // dil
        nq = min(per_res, DIL_TQ)
        rows = lax.broadcasted_iota(jnp.int32, (HEADS_PER_GROUP_B * nq, DIL_WIN), 0)
        cols = lax.broadcasted_iota(jnp.int32, (HEADS_PER_GROUP_B * nq, DIL_WIN), 1)
        rel = (rows & (nq - 1)) - cols
        for r in range(dil):
            for u in range(per_res // nq):
                t0 = i * per_res + u * nq
                if length == DIL_WIN:
                    ks = 0
                else:
                    ks = pl.multiple_of(jnp.clip(t0 - side, 0, length - DIL_WIN), side)
                blocks.append(dict(g=g, dil=dil, r=r, u=u, nq=nq, t0=t0, ks=ks, side=side, rel=rel))

    def scores(bk):
        q = q_refs[bk["g"]][0, bk["r"], bk["u"] * bk["nq"]:(bk["u"] + 1) * bk["nq"], :]
        kw = k_refs[bk["g"]][0, bk["r"], pl.ds(bk["ks"], DIL_WIN), :]
        bk["s"] = _band_scores(q, kw)

    def softmax(bk):
        bk["pml"] = _band_softmax(bk.pop("s"), bk["rel"], bk["t0"] - bk["ks"], bk["side"])

    def output(bk):
        g, dil, r, u, nq = bk["g"], bk["dil"], bk["r"], bk["u"], bk["nq"]
        vw = v_refs[g][0, r, pl.ds(bk["ks"], DIL_WIN), :]
        o, lse = _band_output(*bk.pop("pml"), vw, nq)
        rows = pl.ds(u * nq * dil + r, nq, stride=dil) if dil > 1 else pl.ds(u * nq, nq)
        for j in range(WIDTH_B // LANES):
            o_scr[g, j, rows, :] = o[:, j * LANES:(j + 1) * LANES]
            lse_scr[g, j, rows, :] = lse[:, j * LANES:(j + 1) * LANES]

    for n in range(len(blocks) + 2 * DIL_SKEW):
        if n < len(blocks):
            scores(blocks[n])
        if 0 <= n - DIL_SKEW < len(blocks):
            softmax(blocks[n - DIL_SKEW])
        if 0 <= n - 2 * DIL_SKEW < len(blocks):
            output(blocks[n - 2 * DIL_SKEW])

    for j in range(WIDTH_B // LANES):
        l0, l1, l2 = lse_scr[0, j], lse_scr[1, j], lse_scr[2, j]
        m = jnp.maximum(jnp.maximum(l0, l1), l2)
        e0, e1, e2 = jnp.exp(l0 - m), jnp.exp(l1 - m), jnp.exp(l2 - m)
        ob = (e0 * o_scr[0, j] + e1 * o_scr[1, j] + e2 * o_scr[2, j]) / (e0 + e1 + e2)
        ob_ref[:, j * LANES:(j + 1) * LANES] = ob.astype(BF16)


def _dilated_attention(qs, ks, vs):
    b = qs[0].shape[0]
    seq = qs[0].shape[1] * qs[0].shape[2]
    tile = TOKEN_TILE
    tiles = seq // tile
    q_specs = [pl.BlockSpec((1, dil, tile // dil, WIDTH_B), lambda bi, i: (bi, 0, i, 0))
               for _, dil in DILATED_GROUPS]
    kv_specs = [pl.BlockSpec((1, dil, seq // dil, WIDTH_B), lambda bi, i: (bi, 0, 0, 0),
                             pipeline_mode=pl.Buffered(1))
                for _, dil in DILATED_GROUPS]
    return pl.pallas_call(
        _dilated_kernel,
        grid=(b, tiles),
        in_specs=q_specs + kv_specs + kv_specs,
        out_specs=pl.BlockSpec((tile, WIDTH_B), lambda bi, i: (bi * tiles + i, 0)),
        out_shape=jax.ShapeDtypeStruct((b * seq, WIDTH_B), BF16),
        scratch_shapes=[pltpu.VMEM((N_GROUPS_B, WIDTH_B // LANES, tile, LANES), F32)] * 2,
        compiler_params=_params(("parallel", "arbitrary"), 48),
        name="dilated_attn",
    )(*qs, *ks, *vs)


def _post_kernel(x_ref, mod_ref, oa_ref, ob_ref, gates_ref, wa_ref, wb_ref, wo_ref, g_ref, wu_ref, wd_ref,
                 out_ref):
    br_a = jnp.dot(oa_ref[...], wa_ref[...], preferred_element_type=F32)
    br_b = jnp.dot(ob_ref[...], wb_ref[...], preferred_element_type=F32)
    g_a = gates_ref[:, 0:D_MODEL].astype(F32)
    g_b = gates_ref[:, D_MODEL:GATE_COLS].astype(F32)
    y = jnp.dot((g_a * br_a + g_b * br_b).astype(BF16), wo_ref[...], preferred_element_type=F32)
    mod = lambda k: mod_ref[0, :, k * D_MODEL:(k + 1) * D_MODEL]
    x = x_ref[...] + mod(2) * y

    ms = jnp.mean(x * x, axis=-1, keepdims=True)
    h = (x * lax.rsqrt(ms + EPS) * g_ref[...] * (1.0 + mod(4)) + mod(3)).astype(BF16)
    u = jnp.maximum(jnp.dot(h, wu_ref[...], preferred_element_type=F32), 0.0)
    dn = jnp.dot((u * u).astype(BF16), wd_ref[...], preferred_element_type=F32)
    out_ref[...] = x + mod(5) * dn


def _post_attention(x2, mod3, oa, ob, gates, wa, wb, wo, g, wu, wd, seq):
    n, d = x2.shape
    tm = TOKEN_TILE
    row = lambda i: (i, 0)
    fixed = lambda i: (0, 0)
    return pl.pallas_call(
        _post_kernel,
        grid=(n // tm,),
        in_specs=[pl.BlockSpec((tm, d), row),
                  pl.BlockSpec((1, 1, mod3.shape[2]), lambda i: ((i * tm) // seq, 0, 0)),
                  pl.BlockSpec((tm, WIDTH_A), row),
                  pl.BlockSpec((tm, WIDTH_B), row),
                  pl.BlockSpec((tm, GATE_COLS), row),
                  _resident(wa.shape, fixed), _resident(wb.shape, fixed), _resident(wo.shape, fixed),
                  _resident((1, d), fixed), _resident(wu.shape, fixed), _resident(wd.shape, fixed)],
        out_specs=pl.BlockSpec((tm, d), row),
        out_shape=jax.ShapeDtypeStruct((n, d), F32),
        compiler_params=_params(("parallel",), 60),
        name="post_attn",
    )(x2, mod3, oa, ob, gates, wa, wb, wo, g, wu, wd)


def kernel(x, c, positions, ada_w, ada_b, norm_mix_g, norm_mlp_g, w_in, qk_gain_a, lambda_a, subln_g_a,
           qk_gain_b, w_branch_a, w_branch_b, gate_bias, w_out, w_mlp_up, w_mlp_down):
    b, s, d = x.shape
    depth = ada_w.shape[0]
    tabs = _rope_tables(positions)
    mod = _ada_mod(c, ada_w, ada_b)
    rep = LANES // HEAD_DIM
    x2 = x.reshape(b * s, d)
    for l in range(depth):
        mod3 = mod[l].reshape(b, 1, 6 * d)
        qa, ka, va, qbs, kbs, vbs, gates = _in_projection(
            x2, mod3, norm_mix_g[l].reshape(1, d), w_in[l].astype(BF16), tabs,
            jnp.tile(qk_gain_a[l], (1, rep)), jnp.tile(qk_gain_b[l], (1, rep)),
            gate_bias[l].reshape(1, GATE_COLS), s)

        lam_init = 0.8 - 0.6 * math.exp(-0.3 * l)
        oa = _diff_attention(qa.reshape(b, s, WIDTH_A), ka.reshape(b, s, WIDTH_A),
                             va.reshape(b, s, WIDTH_A), lambda_a[l], subln_g_a[l].reshape(VDIM_A, 1),
                             lam_init)
        ob = _dilated_attention(qbs, kbs, vbs)

        x2 = _post_attention(x2, mod3, oa.reshape(b * s, WIDTH_A), ob, gates,
                             w_branch_a[l].astype(BF16), w_branch_b[l].astype(BF16), w_out[l].astype(BF16),
                             norm_mlp_g[l].reshape(1, d), w_mlp_up[l].astype(BF16),
                             w_mlp_down[l].astype(BF16), s)
    return x2.reshape(b, s, d)
```

```python
import math
from functools import partial

import jax
import jax.numpy as jnp
from jax import lax
from jax.experimental import pallas as pl
from jax.experimental.pallas import tpu as pltpu

D_MODEL = 1024
HEAD_DIM = 64
HEADS_A = 8
VDIM_A = 2 * HEAD_DIM
WIDTH_A = HEADS_A * VDIM_A
DILATED_GROUPS = ((128, 1), (512, 4), (2048, 16))
N_GROUPS_B = 3
HEADS_PER_GROUP_B = 4
WIDTH_B = HEADS_PER_GROUP_B * HEAD_DIM
QKV_B_COLS = N_GROUPS_B * WIDTH_B
D_FF = 4 * D_MODEL
ROPE_THETA = 500000.0
ROT_HALF = HEAD_DIM // 4 // 2
EPS = 1e-6
NEG_INF = -1e30
GATE_COLS = 2 * D_MODEL
IN_COLS = 3 * WIDTH_A + 3 * QKV_B_COLS + GATE_COLS
SM_SCALE = HEAD_DIM ** -0.5
LOG2E = math.log2(math.e)

LANES = 128
PROJ_CHUNK = 256
TOKEN_TILE = 512
VMEM_BYTES = 64 * 1024 * 1024
MIB = 1024 * 1024

BF16 = jnp.bfloat16
F32 = jnp.float32
NT_DIMS = (((1,), (1,)), ((), ()))


def _params(semantics, vmem_mib):
    return pltpu.CompilerParams(dimension_semantics=semantics,
                                vmem_limit_bytes=min(vmem_mib * MIB, VMEM_BYTES - 4 * MIB))


def _resident(shape, index_map):
    return pl.BlockSpec(shape, index_map, pipeline_mode=pl.Buffered(1))


def _rope_kernel(pos_ref, cos_ref, sin_ref):
    pos = pos_ref[...].astype(F32)
    for f in range(ROT_HALF):
        ang = pos * (ROPE_THETA ** (-(2.0 * f) / (2 * ROT_HALF)))
        cos_ref[f] = jnp.cos(ang)
        sin_ref[f] = jnp.sin(ang)


def _rope_tables(positions):
    n = positions.size
    pos2 = positions.reshape(n // LANES, LANES)
    cos, sin = pl.pallas_call(
        _rope_kernel,
        out_shape=[jax.ShapeDtypeStruct((ROT_HALF, n // LANES, LANES), F32)] * 2,
    )(pos2)
    cos = cos.reshape(ROT_HALF, n).T
    sin = sin.reshape(ROT_HALF, n).T
    one = jnp.ones((n, HEAD_DIM - 2 * ROT_HALF), F32)
    zero = jnp.zeros((n, HEAD_DIM - 2 * ROT_HALF), F32)
    z8 = jnp.zeros((n, ROT_HALF), F32)
    c_tab = jnp.concatenate([cos, cos, one], axis=1)
    s1_tab = jnp.concatenate([z8, sin, zero], axis=1)
    s2_tab = jnp.concatenate([sin, z8, zero], axis=1)
    rep = LANES // HEAD_DIM
    return tuple(jnp.tile(t, (1, rep)) for t in (c_tab, s1_tab, s2_tab))


def _ada_kernel(c_ref, w_ref, b_ref, o_ref):
    c = c_ref[...]
    cond = (c * jax.nn.sigmoid(c)).astype(BF16)
    o_ref[0] = jnp.dot(cond, w_ref[0].astype(BF16), preferred_element_type=F32) + b_ref[0]


def _ada_mod(c, ada_w, ada_b):
    depth, d, n = ada_w.shape
    b = c.shape[0]
    tn = 768
    return pl.pallas_call(
        _ada_kernel,
        grid=(depth, n // tn),
        in_specs=[pl.BlockSpec((b, d), lambda l, j: (0, 0)),
                  pl.BlockSpec((1, d, tn), lambda l, j: (l, 0, j)),
                  pl.BlockSpec((1, 1, tn), lambda l, j: (l, 0, j))],
        out_specs=pl.BlockSpec((1, b, tn), lambda l, j: (l, 0, j)),
        out_shape=jax.ShapeDtypeStruct((depth, b, n), F32),
        compiler_params=_params(("parallel", "parallel"), 32),
    )(c, ada_w, ada_b.reshape(depth, 1, n))


def _head_norm_rope(acc, ssum, gain, c_tab, s1_tab, s2_tab):
    y = acc * lax.rsqrt(ssum * (1.0 / HEAD_DIM) + EPS)
    halves = []
    for j in range(0, acc.shape[1], LANES):
        yj = y[:, j:j + LANES] * gain
        halves.append(yj * c_tab + pltpu.roll(yj, ROT_HALF, 1) * s1_tab
                      - pltpu.roll(yj, LANES - ROT_HALF, 1) * s2_tab)
    return halves


def _inproj_kernel(x_ref, mod_ref, g_ref, w_ref, c_ref, s1_ref, s2_ref, gain_a_ref, gain_b_ref,
                   gbias_ref, ones_ref, qa_ref, ka_ref, va_ref, *rest):
    qb_refs, kb_refs, vb_refs = rest[0:3], rest[3:6], rest[6:9]
    gates_ref, h_scr, y_scr = rest[9:12]
    tm = x_ref.shape[0]
    x = x_ref[...]
    ms = jnp.mean(x * x, axis=-1, keepdims=True)
    shift = mod_ref[0, :, 0:D_MODEL]
    scale = mod_ref[0, :, D_MODEL:2 * D_MODEL]
    h = x * lax.rsqrt(ms + EPS) * g_ref[...] * (1.0 + scale) + shift
    h_scr[...] = h.astype(BF16)

    c_tab, s1_tab, s2_tab = c_ref[...], s1_ref[...], s2_ref[...]
    cw = PROJ_CHUNK

    def split_residues(out_ref, dilation):
        for r in range(dilation):
            for j in range(WIDTH_B // LANES):
                out_ref[0, r, :, j * LANES:(j + 1) * LANES] = (
                    y_scr[j, pl.ds(r, tm // dilation, stride=dilation), :].astype(BF16))

    tasks = []

    def normed(out_ref, width, gain):
        def epilogue(acc, ssum, c):
            for j, half in enumerate(_head_norm_rope(acc, ssum, gain, c_tab, s1_tab, s2_tab)):
                out_ref[:, c + j * LANES:c + (j + 1) * LANES] = half.astype(BF16)
        tasks.extend((True, partial(epilogue, c=c)) for c in range(0, width, cw))

    def plain(out_ref, width):
        def epilogue(acc, ssum, c):
            out_ref[:, c:c + cw] = acc.astype(BF16)
        tasks.extend((False, partial(epilogue, c=c)) for c in range(0, width, cw))

    def normed_groups(out_refs, gain):
        def epilogue(acc, ssum, g):
            for j, half in enumerate(_head_norm_rope(acc, ssum, gain, c_tab, s1_tab, s2_tab)):
                y_scr[j] = half
            split_residues(out_refs[g], DILATED_GROUPS[g][1])
        tasks.extend((True, partial(epilogue, g=g)) for g in range(N_GROUPS_B))

    def plain_groups(out_refs):
        def epilogue(acc, ssum, g):
            for j in range(0, WIDTH_B, LANES):
                y_scr[j // LANES] = acc[:, j:j + LANES]
            split_residues(out_refs[g], DILATED_GROUPS[g][1])
        tasks.extend((False, partial(epilogue, g=g)) for g in range(N_GROUPS_B))

    def gate(acc, ssum, c):
        gates_ref[:, c:c + cw] = jax.nn.sigmoid(acc + gbias_ref[:, c:c + cw]).astype(BF16)

    normed(qa_ref, WIDTH_A, gain_a_ref[0:1, :] * (SM_SCALE * LOG2E))
    normed(ka_ref, WIDTH_A, gain_a_ref[1:2, :])
    plain(va_ref, WIDTH_A)
    normed_groups(qb_refs, gain_b_ref[0:1, :] * SM_SCALE)
    normed_groups(kb_refs, gain_b_ref[1:2, :])
    plain_groups(vb_refs)
    tasks.extend((False, partial(gate, c=c)) for c in range(0, GATE_COLS, cw))
    assert len(tasks) * cw == IN_COLS and WIDTH_B == cw

    def proj(n):
        return jnp.dot(h_scr[...], w_ref[:, n * cw:(n + 1) * cw], preferred_element_type=F32)

    def head_sums(n, acc):
        if not tasks[n][0]:
            return None
        return jnp.dot((acc * acc).astype(BF16), ones_ref[...], preferred_element_type=F32)

    last = len(tasks) - 1
    accs = {0: proj(0), 1: proj(1)}
    sums = {0: head_sums(0, accs[0])}
    for n in range(len(tasks)):
        if n + 2 <= last:
            accs[n + 2] = proj(n + 2)
        if n + 1 <= last:
            sums[n + 1] = head_sums(n + 1, accs[n + 1])
        tasks[n][1](accs.pop(n), sums.pop(n))


def _in_projection(x2, mod3, g, w_bf, tabs, gain_a, gain_b, gate_bias, seq):
    n, d = x2.shape
    tm = TOKEN_TILE
    tiles = seq // tm
    heads = jnp.arange(PROJ_CHUNK) // HEAD_DIM
    seg_ones = (heads[:, None] == heads[None, :]).astype(BF16)
    row = lambda i: (i, 0)
    fixed = lambda i: (0, 0)
    res_block = lambda i: (i // tiles, 0, i % tiles, 0)
    wide = [pl.BlockSpec((tm, WIDTH_A), row)] * 3
    wide_shape = [jax.ShapeDtypeStruct((n, WIDTH_A), BF16)] * 3
    grp = [pl.BlockSpec((1, dil, tm // dil, WIDTH_B), res_block) for _, dil in DILATED_GROUPS] * 3
    grp_shape = [jax.ShapeDtypeStruct((n // seq, dil, seq // dil, WIDTH_B), BF16) for _, dil in DILATED_GROUPS] * 3
    outs = pl.pallas_call(
        _inproj_kernel,
        grid=(n // tm,),
        in_specs=[pl.BlockSpec((tm, d), row),
                  pl.BlockSpec((1, 1, mod3.shape[2]), lambda i: (i // tiles, 0, 0)),
                  _resident((1, d), fixed),
                  _resident((d, IN_COLS), fixed),
                  pl.BlockSpec((tm, LANES), row),
                  pl.BlockSpec((tm, LANES), row),
                  pl.BlockSpec((tm, LANES), row),
                  _resident((2, LANES), fixed),
                  _resident((2, LANES), fixed),
                  _resident((1, GATE_COLS), fixed),
                  _resident((PROJ_CHUNK, PROJ_CHUNK), fixed)],
        out_specs=wide + grp + [pl.BlockSpec((tm, GATE_COLS), row)],
        out_shape=wide_shape + grp_shape + [jax.ShapeDtypeStruct((n, GATE_COLS), BF16)],
        scratch_shapes=[pltpu.VMEM((tm, d), BF16), pltpu.VMEM((WIDTH_B // LANES, tm, LANES), F32)],
        compiler_params=_params(("parallel",), 56),
        name="in_proj",
    )(x2, mod3, g, w_bf, *tabs, gain_a, gain_b, gate_bias, seg_ones)
    return outs[0], outs[1], outs[2], outs[3:6], outs[6:9], outs[9:12], outs[12]


DA_TQ = 256
DA_BLOCKS = 16
DA_TK = 512
DA_FINISH_AFTER = 1024
DA_ONES_ROWS = 16


def _diff_attn_kernel(lam_ref, g_ref, q_ref, k_ref, v_ref, o_ref, sa_scr, sb_scr, ma_scr, mb_scr,
                      acc_scr, vta_scr, vtb_scr, *, nq, lam_init):
    seq = k_ref.shape[1]
    tq = DA_TQ
    t = pl.program_id(0)

    def load_values(vt_scr):
        vt_scr[0:VDIM_A, :] = v_ref[0].T
        vt_scr[VDIM_A:VDIM_A + DA_ONES_ROWS, :] = jnp.ones((DA_ONES_ROWS, seq), BF16)

    @pl.when(t == 0)
    def _():
        sb_scr[...] = jnp.zeros(sb_scr.shape, F32)
        mb_scr[...] = jnp.zeros(mb_scr.shape, F32)
        acc_scr[...] = jnp.ones(acc_scr.shape, F32)
        load_values(vta_scr)
        load_values(vtb_scr)

    def finish(acc_r, rows):
        o = acc_r[0:VDIM_A, :] / acc_r[VDIM_A:VDIM_A + 1, :]
        lv = lam_ref[...]
        lam = (jnp.exp(jnp.sum(lv[0:1] * lv[1:2], axis=-1, keepdims=True))
               - jnp.exp(jnp.sum(lv[2:3] * lv[3:4], axis=-1, keepdims=True)) + lam_init)
        out = o[:, :tq] - lam * o[:, tq:]
        ms = jnp.mean(out * out, axis=0, keepdims=True)
        y = out * lax.rsqrt(ms + EPS) * g_ref[...] * (1.0 - lam_init)
        o_ref[0, rows, :] = y.T.astype(BF16)

    def step(rows, vt_scr, s_w, m_w, acc_w, s_r, m_r, acc_r):
        q = q_ref[0, rows, :]
        lane = lax.broadcasted_iota(jnp.int32, q.shape, 1)
        zero = jnp.zeros_like(q)
        qs = jnp.concatenate([jnp.where(lane < HEAD_DIM, q, zero),
                              jnp.where(lane < HEAD_DIM, zero, q)], axis=0)
        m_prev = m_r[0:1]
        m_new = None
        acc = None
        for c in range(0, seq, DA_TK):
            st = lax.dot_general(k_ref[0, c:c + DA_TK, :], qs, NT_DIMS, preferred_element_type=F32)
            s_w[c:c + DA_TK, :] = st
            mc = jnp.max(st, axis=0, keepdims=True)
            m_new = mc if m_new is None else jnp.maximum(m_new, mc)
            pt = jnp.exp2((s_r[c:c + DA_TK, :] - m_prev).astype(BF16))
            part = jnp.dot(vt_scr[:, c:c + DA_TK], pt, preferred_element_type=F32)
            acc = part if acc is None else acc + part
            if c == DA_FINISH_AFTER:
                finish(acc_r, rows)
        m_w[...] = jnp.broadcast_to(m_new, m_w.shape)
        acc_w[...] = acc

    score_bufs = ((sa_scr, ma_scr), (sb_scr, mb_scr))
    vt_bufs = (vta_scr, vtb_scr)
    assert DA_BLOCKS == nq

    def two_blocks(jj, carry):
        @pl.when((jj == 0) & (t > 0))
        def _():
            load_values(vtb_scr)

        @pl.when(jj == 1)
        def _():
            load_values(vta_scr)

        for half in range(2):
            j = 2 * jj + half
            rows = pl.ds(pl.multiple_of(j * tq, tq), tq)
            (s_w, m_w), (s_r, m_r) = score_bufs[half], score_bufs[1 - half]
            step(rows, vt_bufs[half], s_w, m_w, acc_scr.at[(j + DA_BLOCKS - 1) % DA_BLOCKS], s_r, m_r,
                 acc_scr.at[j])
        return carry

    lax.fori_loop(0, DA_BLOCKS // 2, two_blocks, 0)


def _diff_attention(qa, ka, va, lam_p, subln_g, lam_init):
    b, s, _ = qa.shape
    nq = s // DA_TQ
    assert nq % DA_BLOCKS == 0 and DA_BLOCKS % 2 == 0
    per_head = nq // DA_BLOCKS
    pairs = b * HEADS_A * per_head

    def pair(t):
        t = jnp.clip(t, 0, pairs - 1)
        return t // (HEADS_A * per_head), (t // per_head) % HEADS_A, t % per_head

    def q_map(t):
        bi, h, i = pair(t)
        return bi, i, h

    def kv_map(t):
        bi, h, _ = pair(t)
        return bi, 0, h

    def o_map(t):
        bi, h, i = pair(t - 1)
        return bi, i, h

    score_buf = pltpu.VMEM((s, 2 * DA_TQ), F32)
    max_buf = pltpu.VMEM((8, 2 * DA_TQ), F32)
    acc_buf = pltpu.VMEM((DA_BLOCKS, VDIM_A + DA_ONES_ROWS, 2 * DA_TQ), F32)
    return pl.pallas_call(
        partial(_diff_attn_kernel, nq=nq, lam_init=lam_init),
        grid=(pairs + 1,),
        in_specs=[pl.BlockSpec(lam_p.shape, lambda t: (0, 0)),
                  pl.BlockSpec((VDIM_A, 1), lambda t: (0, 0)),
                  pl.BlockSpec((1, DA_BLOCKS * DA_TQ, VDIM_A), q_map),
                  pl.BlockSpec((1, s, VDIM_A), kv_map),
                  pl.BlockSpec((1, s, VDIM_A), kv_map)],
        out_specs=pl.BlockSpec((1, DA_BLOCKS * DA_TQ, VDIM_A), o_map),
        out_shape=jax.ShapeDtypeStruct((b, s, WIDTH_A), BF16),
        scratch_shapes=[score_buf, score_buf, max_buf, max_buf, acc_buf,
                        pltpu.VMEM((VDIM_A + DA_ONES_ROWS, s), BF16),
                        pltpu.VMEM((VDIM_A + DA_ONES_ROWS, s), BF16)],
        compiler_params=_params(("arbitrary",), 40),
        name="diff_attn",
    )(lam_p, subln_g, qa, ka, va)


DIL_TQ = 128
DIL_SKEW = 1
DIL_WIN = 256


def _band_scores(q, kw):
    head = lax.broadcasted_iota(jnp.int32, q.shape, 1) // HEAD_DIM
    zero = jnp.zeros_like(q)
    qs = jnp.concatenate([jnp.where(head == h, q, zero) for h in range(HEADS_PER_GROUP_B)], axis=0)
    return lax.dot_general(qs, kw, NT_DIMS, preferred_element_type=F32)


def _band_softmax(s, rel, offset, side):
    s = jnp.where(jnp.abs(rel + offset) <= side, s, NEG_INF)
    m = jnp.max(s, axis=-1, keepdims=True)
    p = jnp.exp(s - m)
    l = jnp.sum(p, axis=-1, keepdims=True)
    return p.astype(BF16), m, l


def _band_output(p, m, l, vw, nq):
    o = jnp.dot(p, vw, preferred_element_type=F32) / l
    lse = jnp.broadcast_to(m + jnp.log(l), o.shape)
    head = lax.broadcasted_iota(jnp.int32, (nq, WIDTH_B), 1) // HEAD_DIM
    o_sel = jnp.zeros((nq, WIDTH_B), F32)
    lse_sel = jnp.zeros((nq, WIDTH_B), F32)
    for h in range(HEADS_PER_GROUP_B):
        rs = slice(h * nq, (h + 1) * nq)
        o_sel = jnp.where(head == h, o[rs], o_sel)
        lse_sel = jnp.where(head == h, lse[rs], lse_sel)
    return o_sel, lse_sel


def _dilated_kernel(*refs):
    q_refs, k_refs, v_refs = refs[0:3], refs[3:6], refs[6:9]
    ob_ref, o_scr, lse_scr = refs[9:12]
    tile = ob_ref.shape[0]
    i = pl.program_id(1)

    blocks = []
    for g, (window, dil) in enumerate(DILATED_GROUPS):
        side = window // (2 * dil)
        length = k_refs[g].shape[2]
        per_res = tile // dil
        nq = min(per_res, DIL_TQ)
        rows = lax.broadcasted_iota(jnp.int32, (HEADS_PER_GROUP_B * nq, DIL_WIN), 0)
        cols = lax.broadcasted_iota(jnp.int32, (HEADS_PER_GROUP_B * nq, DIL_WIN), 1)
        rel = (rows & (nq - 1)) - cols
        for r in range(dil):
            for u in range(per_res // nq):
                t0 = i * per_res + u * nq
                if length == DIL_WIN:
                    ks = 0
                else:
                    ks = pl.multiple_of(jnp.clip(t0 - side, 0, length - DIL_WIN), side)
                blocks.append(dict(g=g, dil=dil, r=r, u=u, nq=nq, t0=t0, ks=ks, side=side, rel=rel))

    def scores(bk):
        q = q_refs[bk["g"]][0, bk["r"], bk["u"] * bk["nq"]:(bk["u"] + 1) * bk["nq"], :]
        kw = k_refs[bk["g"]][0, bk["r"], pl.ds(bk["ks"], DIL_WIN), :]
        bk["s"] = _band_scores(q, kw)

    def softmax(bk):
        bk["pml"] = _band_softmax(bk.pop("s"), bk["rel"], bk["t0"] - bk["ks"], bk["side"])

    def output(bk):
        g, dil, r, u, nq = bk["g"], bk["dil"], bk["r"], bk["u"], bk["nq"]
        vw = v_refs[g][0, r, pl.ds(bk["ks"], DIL_WIN), :]
        o, lse = _band_output(*bk.pop("pml"), vw, nq)
        rows = pl.ds(u * nq * dil + r, nq, stride=dil) if dil > 1 else pl.ds(u * nq, nq)
        for j in range(WIDTH_B // LANES):
            o_scr[g, j, rows, :] = o[:, j * LANES:(j + 1) * LANES]
            lse_scr[g, j, rows, :] = lse[:, j * LANES:(j + 1) * LANES]

    for n in range(len(blocks) + 2 * DIL_SKEW):
        if n < len(blocks):
            scores(blocks[n])
        if 0 <= n - DIL_SKEW < len(blocks):
            softmax(blocks[n - DIL_SKEW])
        if 0 <= n - 2 * DIL_SKEW < len(blocks):
            output(blocks[n - 2 * DIL_SKEW])

    for j in range(WIDTH_B // LANES):
        l0, l1, l2 = lse_scr[0, j], lse_scr[1, j], lse_scr[2, j]
        m = jnp.maximum(jnp.maximum(l0, l1), l2)
        e0, e1, e2 = jnp.exp(l0 - m), jnp.exp(l1 - m), jnp.exp(l2 - m)
        ob = (e0 * o_scr[0, j] + e1 * o_scr[1, j] + e2 * o_scr[2, j]) / (e0 + e1 + e2)
        ob_ref[:, j * LANES:(j + 1) * LANES] = ob.astype(BF16)


def _dilated_attention(qs, ks, vs):
    b = qs[0].shape[0]
    seq = qs[0].shape[1] * qs[0].shape[2]
    tile = TOKEN_TILE
    tiles = seq // tile
    q_specs = [pl.BlockSpec((1, dil, tile // dil, WIDTH_B), lambda bi, i: (bi, 0, i, 0))
               for _, dil in DILATED_GROUPS]
    kv_specs = [pl.BlockSpec((1, dil, seq // dil, WIDTH_B), lambda bi, i: (bi, 0, 0, 0),
                             pipeline_mode=pl.Buffered(1))
                for _, dil in DILATED_GROUPS]
    return pl.pallas_call(
        _dilated_kernel,
        grid=(b, tiles),
        in_specs=q_specs + kv_specs + kv_specs,
        out_specs=pl.BlockSpec((tile, WIDTH_B), lambda bi, i: (bi * tiles + i, 0)),
        out_shape=jax.ShapeDtypeStruct((b * seq, WIDTH_B), BF16),
        scratch_shapes=[pltpu.VMEM((N_GROUPS_B, WIDTH_B // LANES, tile, LANES), F32)] * 2,
        compiler_params=_params(("parallel", "arbitrary"), 48),
        name="dilated_attn",
    )(*qs, *ks, *vs)


def _post_kernel(x_ref, mod_ref, oa_ref, ob_ref, gates_ref, wa_ref, wb_ref, wo_ref, g_ref, wu_ref, wd_ref,
                 out_ref):
    br_a = jnp.dot(oa_ref[...], wa_ref[...], preferred_element_type=F32)
    br_b = jnp.dot(ob_ref[...], wb_ref[...], preferred_element_type=F32)
    g_a = gates_ref[:, 0:D_MODEL].astype(F32)
    g_b = gates_ref[:, D_MODEL:GATE_COLS].astype(F32)
    y = jnp.dot((g_a * br_a + g_b * br_b).astype(BF16), wo_ref[...], preferred_element_type=F32)
    mod = lambda k: mod_ref[0, :, k * D_MODEL:(k + 1) * D_MODEL]
    x = x_ref[...] + mod(2) * y

    ms = jnp.mean(x * x, axis=-1, keepdims=True)
    h = (x * lax.rsqrt(ms + EPS) * g_ref[...] * (1.0 + mod(4)) + mod(3)).astype(BF16)
    u = jnp.maximum(jnp.dot(h, wu_ref[...], preferred_element_type=F32), 0.0)
    dn = jnp.dot((u * u).astype(BF16), wd_ref[...], preferred_element_type=F32)
    out_ref[...] = x + mod(5) * dn


def _post_attention(x2, mod3, oa, ob, gates, wa, wb, wo, g, wu, wd, seq):
    n, d = x2.shape
    tm = TOKEN_TILE
    row = lambda i: (i, 0)
    fixed = lambda i: (0, 0)
    return pl.pallas_call(
        _post_kernel,
        grid=(n // tm,),
        in_specs=[pl.BlockSpec((tm, d), row),
                  pl.BlockSpec((1, 1, mod3.shape[2]), lambda i: ((i * tm) // seq, 0, 0)),
                  pl.BlockSpec((tm, WIDTH_A), row),
                  pl.BlockSpec((tm, WIDTH_B), row),
                  pl.BlockSpec((tm, GATE_COLS), row),
                  _resident(wa.shape, fixed), _resident(wb.shape, fixed), _resident(wo.shape, fixed),
                  _resident((1, d), fixed), _resident(wu.shape, fixed), _resident(wd.shape, fixed)],
        out_specs=pl.BlockSpec((tm, d), row),
        out_shape=jax.ShapeDtypeStruct((n, d), F32),
        compiler_params=_params(("parallel",), 60),
        name="post_attn",
    )(x2, mod3, oa, ob, gates, wa, wb, wo, g, wu, wd)


def kernel(x, c, positions, ada_w, ada_b, norm_mix_g, norm_mlp_g, w_in, qk_gain_a, lambda_a, subln_g_a,
           qk_gain_b, w_branch_a, w_branch_b, gate_bias, w_out, w_mlp_up, w_mlp_down):
    b, s, d = x.shape
    depth = ada_w.shape[0]
    tabs = _rope_tables(positions)
    mod = _ada_mod(c, ada_w, ada_b)
    rep = LANES // HEAD_DIM
    x2 = x.reshape(b * s, d)
    for l in range(depth):
        mod3 = mod[l].reshape(b, 1, 6 * d)
        qa, ka, va, qbs, kbs, vbs, gates = _in_projection(
            x2, mod3, norm_mix_g[l].reshape(1, d), w_in[l].astype(BF16), tabs,
            jnp.tile(qk_gain_a[l], (1, rep)), jnp.tile(qk_gain_b[l], (1, rep)),
            gate_bias[l].reshape(1, GATE_COLS), s)

        lam_init = 0.8 - 0.6 * math.exp(-0.3 * l)
        oa = _diff_attention(qa.reshape(b, s, WIDTH_A), ka.reshape(b, s, WIDTH_A),
                             va.reshape(b, s, WIDTH_A), lambda_a[l], subln_g_a[l].reshape(VDIM_A, 1),
                             lam_init)
        ob = _dilated_attention(qbs, kbs, vbs)

        x2 = _post_attention(x2, mod3, oa.reshape(b * s, WIDTH_A), ob, gates,
                             w_branch_a[l].astype(BF16), w_branch_b[l].astype(BF16), w_out[l].astype(BF16),
                             norm_mlp_g[l].reshape(1, d), w_mlp_up[l].astype(BF16),
                             w_mlp_down[l].astype(BF16), s)
    return x2.reshape(b, s, d)
```
